```python
import math
import jax, jax.numpy as jnp
from jax import lax
import numpy as np

D_MODEL = 1024
BATCH = 2
SEQ = 8192
DEPTH = 2
DEC_BATCH = 32
DEC_SEQ = 1
PAST_LEN = 16384
PAGE_SIZE = 128

HEAD_DIM = 64
HEADS_PER_GROUP = 4
DIL_GROUPS = ((128, 1), (512, 4), (2048, 16))
N_ATTN_HEADS = HEADS_PER_GROUP * len(DIL_GROUPS)
QKV_W = N_ATTN_HEADS * HEAD_DIM
ATTN_OUT = HEADS_PER_GROUP * HEAD_DIM
Q_BLOCK = 128
NUM_BUCKETS = 32
MAX_DISTANCE = 2048
D_INNER = 2 * D_MODEL
SSD_HEAD_DIM = 64
SSD_HEADS = D_INNER // SSD_HEAD_DIM
SSD_GROUPS = 8
D_STATE = 128
D_CONV = 4
CONV_DIM = D_INNER + 2 * SSD_GROUPS * D_STATE
SSD_CHUNK = 128
D_FF = 4 * D_MODEL
EPS = 1e-6
N_IN = 3 * QKV_W + D_INNER + CONV_DIM + SSD_HEADS + 2 * D_MODEL

kernel_name = 'hybrid_dilated_attn_ssd_decoder_step'


def _rmsnorm(x, g):
    xf = x.astype(jnp.float32)
    y = xf * lax.rsqrt(jnp.mean(xf * xf, axis=-1, keepdims=True) + EPS)
    return (y * g.astype(jnp.float32)).astype(x.dtype)


def _gated_rmsnorm(y, z, g):
    b, L = y.shape[:2]
    h = (y * jax.nn.silu(z)).astype(jnp.float32).reshape(b, L, SSD_GROUPS, D_INNER // SSD_GROUPS)
    h = h * lax.rsqrt(jnp.mean(h * h, axis=-1, keepdims=True) + EPS)
    return (h.reshape(b, L, D_INNER) * g.astype(jnp.float32)).astype(y.dtype)


def _split_cols(t):
    sizes = (QKV_W, QKV_W, QKV_W, D_INNER, CONV_DIM, SSD_HEADS, 2 * D_MODEL)
    outs, start = [], 0
    for n in sizes:
        outs.append(t[..., start:start + n])
        start += n
    return outs


def _t5_bucket(dist):
    max_exact = NUM_BUCKETS // 2
    df = jnp.maximum(dist, 1).astype(jnp.float32)
    large = max_exact + (jnp.log(df / max_exact) / math.log(MAX_DISTANCE / max_exact)
                         * (NUM_BUCKETS - max_exact)).astype(jnp.int32)
    large = jnp.minimum(large, NUM_BUCKETS - 1)
    return jnp.where(dist < max_exact, dist, large)


def _dilated_group(q, k, v, bias, dil, q_off):
    b, Lq, H, hd = q.shape
    Lk = k.shape[1]
    nk = bias.shape[0]
    qb = min(Q_BLOCK, Lq)
    nb = -(-Lq // qb)
    pad = nb * qb - Lq
    qp = jnp.pad(q, ((0, 0), (0, pad), (0, 0), (0, 0)))
    qp = qp.reshape(b, nb, qb, H, hd).transpose(1, 0, 2, 3, 4)
    steps = jnp.arange(nk, dtype=jnp.int32) * dil
    scale = hd ** -0.5
    bias_t = bias.astype(jnp.float32).T

    def block(args):
        bi, qblk = args
        pos = q_off + bi * qb + jnp.arange(qb, dtype=jnp.int32)[:, None] - steps[None, :]
        valid = pos >= 0
        idx = jnp.clip(pos, 0, Lk - 1)
        kg = k[:, idx]
        vg = v[:, idx]
        s = jnp.einsum('bqhd,bqjhd->bqhj', qblk, kg).astype(jnp.float32) * scale + bias_t
        s = jnp.where(valid[None, :, None, :], s, -jnp.inf)
        lse = jax.nn.logsumexp(s, axis=-1)
        p = jnp.exp(s - lse[..., None]).astype(v.dtype)
        o = jnp.einsum('bqhj,bqjhd->bqhd', p, vg)
        return o, lse

    o, lse = lax.map(block, (jnp.arange(nb, dtype=jnp.int32), qp))
    o = o.transpose(1, 0, 2, 3, 4).reshape(b, nb * qb, H, hd)[:, :Lq]
    lse = lse.transpose(1, 0, 2, 3).reshape(b, nb * qb, H)[:, :Lq]
    return o, lse


def _dilated_mixer(q, k, v, rel_bias, kv_bufs):
    b, L = q.shape[:2]
    outs, lses, new_bufs = [], [], []
    for g, (win, dil) in enumerate(DIL_GROUPS):
        hs = slice(g * HEADS_PER_GROUP, (g + 1) * HEADS_PER_GROUP)
        qg, kg, vg = q[:, :, hs], k[:, :, hs], v[:, :, hs]
        nk = win // dil + 1
        bias = rel_bias[_t5_bucket(jnp.arange(nk, dtype=jnp.int32) * dil)][:, hs]
        rows = jnp.stack([kg, vg], axis=2)
        if kv_bufs is None:
            kk, vv, off = kg, vg, 0
            keep = min(win, L)
            new_bufs.append(rows[:, L - keep:])
        else:
            buf = kv_bufs[g].astype(q.dtype)
            off = buf.shape[1]
            kk = jnp.concatenate([buf[:, :, 0], kg], axis=1)
            vv = jnp.concatenate([buf[:, :, 1], vg], axis=1)
            new_bufs.append(jnp.concatenate([buf, rows], axis=1)[:, -off:])
        o, lse = _dilated_group(qg, kk, vv, bias, dil, off)
        outs.append(o)
        lses.append(lse)
    w = jax.nn.softmax(jnp.stack(lses, axis=0), axis=0)
    o = jnp.einsum('gblh,gblhd->blhd', w.astype(q.dtype), jnp.stack(outs, axis=0))
    return o.reshape(b, L, ATTN_OUT), new_bufs


def _segsum(a):
    T = a.shape[-1]
    x = jnp.broadcast_to(a[..., None], a.shape + (T,))
    x = jnp.where(jnp.tril(jnp.ones((T, T), bool), -1), x, 0.0)
    s = jnp.cumsum(x, axis=-2)
    return jnp.where(jnp.tril(jnp.ones((T, T), bool)), s, -jnp.inf)


def _ssd(x, dt, A, Bm, Cm, h0):
    b, L, H, P = x.shape
    G, N = Bm.shape[2], Bm.shape[3]
    E = H // G
    Q = min(SSD_CHUNK, L)
    nc = -(-L // Q)
    pad = nc * Q - L

    def padf(t):
        return jnp.pad(t, [(0, 0), (0, pad)] + [(0, 0)] * (t.ndim - 2))

    xf = padf(x.astype(jnp.float32) * dt[..., None]).reshape(b, nc, Q, G, E, P)
    a = padf(dt * A).reshape(b, nc, Q, G, E).transpose(0, 1, 3, 4, 2)
    Bc = padf(Bm.astype(jnp.float32)).reshape(b, nc, Q, G, N)
    Cc = padf(Cm.astype(jnp.float32)).reshape(b, nc, Q, G, N)
    a_cs = jnp.cumsum(a, axis=-1)
    lmat = jnp.exp(_segsum(a))
    cb = jnp.einsum('bclgn,bcsgn->bcgls', Cc, Bc)
    y_diag = jnp.einsum('bcgels,bcsgep->bclgep', cb[:, :, :, None] * lmat, xf)
    decay_states = jnp.exp(a_cs[..., -1:] - a_cs).transpose(0, 1, 4, 2, 3)
    states = jnp.einsum('bclgn,bclgep->bcgepn', Bc, xf * decay_states[..., None])
    states = jnp.concatenate([h0.reshape(b, 1, G, E, P, N), states], axis=1)
    chunk_decay = jnp.pad(a_cs[..., -1].transpose(0, 2, 3, 1), ((0, 0), (0, 0), (0, 0), (1, 0)))
    dc = jnp.exp(_segsum(chunk_decay))
    new_states = jnp.einsum('bgezc,bcgepn->bzgepn', dc, states)
    y_off = jnp.einsum('bclgn,bcgepn->bclgep', Cc, new_states[:, :-1]) \
        * jnp.exp(a_cs).transpose(0, 1, 4, 2, 3)[..., None]
    y = (y_diag + y_off).reshape(b, nc * Q, H, P)[:, :L]
    return y, new_states[:, -1].reshape(b, H, P, N)


def _causal_conv(xbc, buf, w, bias):
    L = xbc.shape[1]
    xp = jnp.concatenate([buf, xbc], axis=1)
    y = bias + xp[:, 0:L] * w[0]
    for t in range(1, D_CONV):
        y = y + xp[:, t:t + L] * w[t]
    return jax.nn.silu(y), xp[:, L:]


def _ssd_mixer(z, xbc, dt_raw, conv_buf, h0, conv_w, conv_b, dt_bias, a_log, d_skip, norm_g):
    b, L = z.shape[:2]
    xbc, new_buf = _causal_conv(xbc, conv_buf, conv_w, conv_b)
    xs = xbc[..., :D_INNER].reshape(b, L, SSD_HEADS, SSD_HEAD_DIM)
    Bm = xbc[..., D_INNER:D_INNER + SSD_GROUPS * D_STATE].reshape(b, L, SSD_GROUPS, D_STATE)
    Cm = xbc[..., D_INNER + SSD_GROUPS * D_STATE:].reshape(b, L, SSD_GROUPS, D_STATE)
    dt = jax.nn.softplus(dt_raw.astype(jnp.float32) + dt_bias.astype(jnp.float32))
    A = -jnp.exp(a_log.astype(jnp.float32))
    y, h = _ssd(xs, dt, A, Bm, Cm, h0)
    y = y + d_skip.astype(jnp.float32)[:, None] * xs.astype(jnp.float32)
    y = _gated_rmsnorm(y.reshape(b, L, D_INNER).astype(z.dtype), z, norm_g)
    return y, new_buf, h


def _trunk(x, c, kv_caches, ssm_state, conv_state, rel_bias, w_ada, b_ada, norm1_g, norm2_g,
           w_in, conv_w, conv_b, dt_bias, a_log, d_skip, ssd_norm_g, w_o_attn, w_o_ssd,
           w_out, w_up, w_down, final_g):
    b, L, _ = x.shape
    new_kv = [[] for _ in DIL_GROUPS]
    new_ssm, new_conv = [], []
    for l in range(DEPTH):
        if kv_caches is None:
            bufs = None
            conv_buf = jnp.zeros((b, D_CONV - 1, CONV_DIM), x.dtype)
            h0 = jnp.zeros((b, SSD_HEADS, SSD_HEAD_DIM, D_STATE), jnp.float32)
        else:
            bufs = [cache[l] for cache in kv_caches]
            conv_buf = conv_state[l].astype(x.dtype)
            h0 = ssm_state[l].astype(jnp.float32)
        mod = jax.nn.silu(c) @ w_ada[l] + b_ada[l]
        sh1, sc1, g1, sh2, sc2, g2 = jnp.split(mod[:, None, :], 6, axis=-1)
        h = _rmsnorm(x, norm1_g[l]) * (1 + sc1) + sh1
        q, k, v, z, xbc, dt_raw, gates = _split_cols(h @ w_in[l])
        q = q.reshape(b, L, N_ATTN_HEADS, HEAD_DIM)
        k = k.reshape(b, L, N_ATTN_HEADS, HEAD_DIM)
        v = v.reshape(b, L, N_ATTN_HEADS, HEAD_DIM)
        attn, bufs_new = _dilated_mixer(q, k, v, rel_bias, bufs)
        ssd, conv_new, h_new = _ssd_mixer(z, xbc, dt_raw, conv_buf, h0, conv_w[l], conv_b[l],
                                          dt_bias[l], a_log[l], d_skip[l], ssd_norm_g[l])
        g_attn, g_ssd = jnp.split(jax.nn.sigmoid(gates), 2, axis=-1)
        merged = g_attn * (attn @ w_o_attn[l]) + g_ssd * (ssd @ w_o_ssd[l])
        x = x + g1 * (merged @ w_out[l])
        h = _rmsnorm(x, norm2_g[l]) * (1 + sc2) + sh2
        x = x + g2 * (jnp.square(jax.nn.relu(h @ w_up[l])) @ w_down[l])
        for g in range(len(DIL_GROUPS)):
            new_kv[g].append(bufs_new[g])
        new_ssm.append(h_new.astype(x.dtype))
        new_conv.append(conv_new)
    y = _rmsnorm(x, final_g)
    return y, [jnp.stack(n, axis=0) for n in new_kv], jnp.stack(new_ssm, axis=0), jnp.stack(new_conv, axis=0)


def setup_inputs(seed: int = 0) -> dict:
    key = jax.random.key(seed)
    ks = jax.random.split(key, 32)
    f32 = jnp.float32

    def nrm(k, shape, s):
        return jax.random.normal(k, shape, f32) * s

    lb = [min(w, PAST_LEN) for (w, _) in DIL_GROUPS]
    dt0 = jnp.exp(jax.random.uniform(ks[17], (DEPTH, SSD_HEADS), f32, math.log(1e-3), math.log(1e-1)))
    return {
        'x_prompt': nrm(ks[0], (BATCH, SEQ, D_MODEL), 1.0),
        'x_sample': nrm(ks[1], (DEC_BATCH, DEC_SEQ, D_MODEL), 1.0),
        'cache_kv_g0': nrm(ks[2], (DEPTH, DEC_BATCH, lb[0], 2, HEADS_PER_GROUP, HEAD_DIM), 1.0),
        'cache_kv_g1': nrm(ks[3], (DEPTH, DEC_BATCH, lb[1], 2, HEADS_PER_GROUP, HEAD_DIM), 1.0),
        'cache_kv_g2': nrm(ks[4], (DEPTH, DEC_BATCH, lb[2], 2, HEADS_PER_GROUP, HEAD_DIM), 1.0),
        'state_ssm': nrm(ks[5], (DEPTH, DEC_BATCH, SSD_HEADS, SSD_HEAD_DIM, D_STATE), 0.1),
        'state_conv': nrm(ks[6], (DEPTH, DEC_BATCH, D_CONV - 1, CONV_DIM), 1.0),
        'c_prompt': nrm(ks[7], (BATCH, D_MODEL), 1.0),
        'c_sample': nrm(ks[8], (DEC_BATCH, D_MODEL), 1.0),
        'rel_bias': nrm(ks[9], (NUM_BUCKETS, N_ATTN_HEADS), 0.5),
        'w_ada': nrm(ks[10], (DEPTH, D_MODEL, 6 * D_MODEL), 0.5 * D_MODEL ** -0.5),
        'b_ada': nrm(ks[11], (DEPTH, 6 * D_MODEL), 0.02),
        'norm1_g': 1.0 + nrm(ks[12], (DEPTH, D_MODEL), 0.02),
        'norm2_g': 1.0 + nrm(ks[13], (DEPTH, D_MODEL), 0.02),
        'w_in': nrm(ks[14], (DEPTH, D_MODEL, N_IN), D_MODEL ** -0.5),
        'conv_w': nrm(ks[15], (DEPTH, D_CONV, CONV_DIM), D_CONV ** -0.5),
        'conv_b': nrm(ks[16], (DEPTH, CONV_DIM), 0.02),
        'dt_bias': dt0 + jnp.log(-jnp.expm1(-dt0)),
        'a_log': jnp.log(jax.random.uniform(ks[18], (DEPTH, SSD_HEADS), f32, 1.0, 16.0)),
        'd_skip': 1.0 + nrm(ks[19], (DEPTH, SSD_HEADS), 0.02),
        'ssd_norm_g': 1.0 + nrm(ks[20], (DEPTH, D_INNER), 0.02),
        'w_o_attn': nrm(ks[21], (DEPTH, ATTN_OUT, D_MODEL), ATTN_OUT ** -0.5),
        'w_o_ssd': nrm(ks[22], (DEPTH, D_INNER, D_MODEL), D_INNER ** -0.5),
        'w_out': nrm(ks[23], (DEPTH, D_MODEL, D_MODEL), D_MODEL ** -0.5),
        'w_up': nrm(ks[24], (DEPTH, D_MODEL, D_FF), D_MODEL ** -0.5),
        'w_down': nrm(ks[25], (DEPTH, D_FF, D_MODEL), D_FF ** -0.5),
        'final_g': 1.0 + nrm(ks[26], (D_MODEL,), 0.02),
    }


def reference(x_prompt, x_sample, cache_kv_g0, cache_kv_g1, cache_kv_g2, state_ssm, state_conv,
              c_prompt, c_sample, rel_bias, w_ada, b_ada, norm1_g, norm2_g, w_in, conv_w, conv_b,
              dt_bias, a_log, d_skip, ssd_norm_g, w_o_attn, w_o_ssd, w_out, w_up, w_down, final_g):
    weights = (rel_bias, w_ada, b_ada, norm1_g, norm2_g, w_in, conv_w, conv_b, dt_bias, a_log,
               d_skip, ssd_norm_g, w_o_attn, w_o_ssd, w_out, w_up, w_down, final_g)
    y_prompt, kv_p, ssm_prompt, conv_prompt = _trunk(x_prompt, c_prompt, None, None, None, *weights)
    y_sample, kv_s, ssm_sample, conv_sample = _trunk(
        x_sample, c_sample, (cache_kv_g0, cache_kv_g1, cache_kv_g2), state_ssm, state_conv, *weights)
    return (y_prompt, y_sample, kv_p[0], kv_p[1], kv_p[2], ssm_prompt, conv_prompt,
            kv_s[0], kv_s[1], kv_s[2], ssm_sample, conv_sample)
```

```python
import functools
import math

import jax
import jax.numpy as jnp
import numpy as np
from jax import lax
from jax.experimental import pallas as pl
from jax.experimental.pallas import tpu as pltpu

F32 = jnp.float32
BF16 = jnp.bfloat16

D_MODEL = 1024
HEAD_DIM = 64
HEADS_PER_GROUP = 4
DIL_GROUPS = ((128, 1), (512, 4), (2048, 16))
N_GROUPS = len(DIL_GROUPS)
GROUP_W = HEADS_PER_GROUP * HEAD_DIM
QKV_W = N_GROUPS * GROUP_W
BAND = 128
NUM_BUCKETS = 32
MAX_DISTANCE = 2048
D_INNER = 2 * D_MODEL
SSD_HEADS = 32
SSD_P = 64
SSD_GROUPS = 8
SSD_GROUP_W = D_INNER // SSD_GROUPS
D_STATE = 128
D_CONV = 4
CONV_DIM = D_INNER + 2 * SSD_GROUPS * D_STATE
D_FF = 4 * D_MODEL
EPS = 1e-6
ATTN_SCALE = HEAD_DIM ** -0.5
NEG = -1e30

SUBLANES = 8
LANES = 128
V7X_VMEM_LIMIT = 52 * 1024 * 1024

OFF_Z = 0
OFF_GATE = D_INNER
OFF_XBC = 2 * D_INNER
OFF_Q = OFF_XBC + CONV_DIM
OFF_K = OFF_Q + QKV_W
OFF_V = OFF_K + QKV_W
OFF_DT = OFF_V + QKV_W
DT_W = 2 * LANES
PROJ_W = OFF_DT + DT_W

NT_DIMS = (((1,), (1,)), ((), ()))
TN_DIMS = (((0,), (0,)), ((), ()))


def _params(semantics, vmem=V7X_VMEM_LIMIT):
    return pltpu.CompilerParams(dimension_semantics=semantics, vmem_limit_bytes=vmem)


def _sigmoid(x):
    return 1.0 / (1.0 + jnp.exp(-x))


def _silu(x):
    return x * _sigmoid(x)


def _softplus(x):
    return jnp.maximum(x, 0.0) + jnp.log(1.0 + jnp.exp(-jnp.abs(x)))


def _rms(x):
    return x * lax.rsqrt(jnp.mean(x * x, axis=-1, keepdims=True) + EPS)


def _ada_kernel(c_ref, w_ref, b_ref, o_ref):
    s = _silu(c_ref[...]).astype(BF16)
    o_ref[...] = jnp.dot(s, w_ref[...].astype(BF16), preferred_element_type=F32) + b_ref[...]


def _ada_mod(c_all, w_ada, b_ada):
    depth, d, n = w_ada.shape
    rows = c_all.shape[0]
    tn = 1536
    return pl.pallas_call(
        _ada_kernel,
        grid=(depth, n // tn),
        in_specs=[
            pl.BlockSpec((rows, d), lambda l, j: (0, 0)),
            pl.BlockSpec((None, d, tn), lambda l, j: (l, 0, j)),
            pl.BlockSpec((None, 1, tn), lambda l, j: (l, 0, j)),
        ],
        out_specs=pl.BlockSpec((None, rows, tn), lambda l, j: (l, 0, j)),
        out_shape=jax.ShapeDtypeStruct((depth, rows, n), F32),
        compiler_params=_params(("arbitrary", "arbitrary")),
        name="ada_mod",
    )(c_all, w_ada, b_ada.reshape(depth, 1, n))


def _in_proj_kernel(x_ref, mod_ref, g_ref, w_ref, o_ref, h_scr):
    @pl.when(pl.program_id(2) == 0)
    def _():
        y = _rms(x_ref[...]) * g_ref[...]
        h = y * (1.0 + mod_ref[:, D_MODEL:2 * D_MODEL]) + mod_ref[:, 0:D_MODEL]
        h_scr[...] = h.astype(BF16)

    o_ref[...] = jnp.dot(h_scr[...], w_ref[...], preferred_element_type=F32)


def _in_proj(x, mod, norm_g, w_in_r, layer, tm):
    bx, lx, d = x.shape
    r = mod.shape[2]
    tn = 1536
    return pl.pallas_call(
        _in_proj_kernel,
        grid=(bx, lx // tm, PROJ_W // tn),
        in_specs=[
            pl.BlockSpec((None, tm, d), lambda b, i, n: (b, i, 0)),
            pl.BlockSpec((None, None, r, 2 * d), lambda b, i, n: (layer, b, 0, 0)),
            pl.BlockSpec((None, 1, d), lambda b, i, n: (layer, 0, 0)),
            pl.BlockSpec((None, d, tn), lambda b, i, n: (layer, 0, n)),
        ],
        out_specs=pl.BlockSpec((None, tm, tn), lambda b, i, n: (b, i, n)),
        out_shape=jax.ShapeDtypeStruct((bx, lx, PROJ_W), F32),
        scratch_shapes=[pltpu.VMEM((tm, d), BF16)],
        compiler_params=_params(("arbitrary", "arbitrary", "arbitrary")),
        name="in_proj",
    )(x, mod, norm_g, w_in_r)


def _t5_bucket_np(dist):
    max_exact = NUM_BUCKETS // 2
    df = np.maximum(dist, 1).astype(np.float32)
    ratio = np.log(df / np.float32(max_exact)) / np.float32(math.log(MAX_DISTANCE / max_exact))
    large = max_exact + (ratio * np.float32(NUM_BUCKETS - max_exact)).astype(np.int32)
    large = np.minimum(large, NUM_BUCKETS - 1)
    return np.where(dist < max_exact, dist, large).astype(np.int32)


def _bucket_tiles():
    q = np.arange(BAND)[:, None]
    c = np.arange(2 * BAND)[None, :]
    j = q + BAND - c
    valid = (j >= 0) & (j <= BAND)
    tiles = []
    for _, dil in DIL_GROUPS:
        b = _t5_bucket_np(np.clip(j, 0, BAND) * dil)
        tiles.append(np.where(valid, b, -1))
    return np.stack(tiles).astype(np.int32)


def _bias_kernel(rb_ref, bk_ref, o_ref):
    hh = pl.program_id(0)
    bk = bk_ref[...]
    acc = jnp.full(bk.shape, NEG, F32)
    for b in range(NUM_BUCKETS):
        acc = jnp.where(bk == b, rb_ref[b, hh], acc)
    o_ref[...] = acc


def _bias_tiles(rel_bias):
    n_heads = rel_bias.shape[1]
    return pl.pallas_call(
        _bias_kernel,
        grid=(n_heads,),
        in_specs=[
            pl.BlockSpec(memory_space=pltpu.SMEM),
            pl.BlockSpec((None, BAND, 2 * BAND), lambda h: (h // HEADS_PER_GROUP, 0, 0)),
        ],
        out_specs=pl.BlockSpec((None, BAND, 2 * BAND), lambda h: (h, 0, 0)),
        out_shape=jax.ShapeDtypeStruct((n_heads, BAND, 2 * BAND), F32),
        compiler_params=_params(("arbitrary",)),
        name="bias_tiles",
    )(rel_bias, jnp.asarray(_bucket_tiles()))


def _attn_kernel(q_ref, kp_ref, kc_ref, vp_ref, vc_ref, bias_ref, o_ref):
    first = pl.program_id(2) == 0
    col = lax.broadcasted_iota(jnp.int32, (BAND, 2 * BAND), 1)
    pen = jnp.where(col < BAND, jnp.where(first, NEG, 0.0), 0.0)
    qb = q_ref[...].astype(BF16)
    kb = jnp.concatenate([kp_ref[...], kc_ref[...]], axis=0).astype(BF16)
    vb = jnp.concatenate([vp_ref[...], vc_ref[...]], axis=0).astype(BF16)
    for h in range(HEADS_PER_GROUP):
        sl = slice(h * HEAD_DIM, (h + 1) * HEAD_DIM)
        s = lax.dot_general(qb[:, sl], kb[:, sl], NT_DIMS, preferred_element_type=F32)
        s = s * ATTN_SCALE + bias_ref[h] + pen
        m = jnp.max(s, axis=-1, keepdims=True)
        p = jnp.exp(s - m)
        l = jnp.sum(p, axis=-1, keepdims=True)
        o = jnp.dot(p.astype(BF16), vb[:, sl], preferred_element_type=F32) / l
        o_ref[:, sl] = o
        o_ref[:, GROUP_W + h * HEAD_DIM:GROUP_W + (h + 1) * HEAD_DIM] = jnp.broadcast_to(
            m + jnp.log(l), (BAND, HEAD_DIM))


def _prompt_attention(proj, bias, g):
    b, l, _ = proj.shape
    dil = DIL_GROUPS[g][1]
    ld = l // dil
    nblk = PROJ_W // GROUP_W
    qb, kb, vb = (off // GROUP_W + g for off in (OFF_Q, OFF_K, OFF_V))
    view = proj.reshape(b, ld, dil * PROJ_W)
    blk = (None, BAND, GROUP_W)

    def cur(cb):
        return pl.BlockSpec(blk, lambda bb, r, i: (bb, i, r * nblk + cb))

    def prev(cb):
        return pl.BlockSpec(blk, lambda bb, r, i: (bb, jnp.maximum(i - 1, 0), r * nblk + cb))

    out = pl.pallas_call(
        _attn_kernel,
        grid=(b, dil, ld // BAND),
        in_specs=[cur(qb), prev(kb), cur(kb), prev(vb), cur(vb),
                  pl.BlockSpec((HEADS_PER_GROUP, BAND, 2 * BAND), lambda bb, r, i: (g, 0, 0))],
        out_specs=pl.BlockSpec((None, BAND, 2 * GROUP_W), lambda bb, r, i: (bb, i, r)),
        out_shape=jax.ShapeDtypeStruct((b, ld, dil * 2 * GROUP_W), F32),
        compiler_params=_params(("arbitrary", "arbitrary", "arbitrary")),
        name=f"prompt_attn_g{g}",
    )(view, view, view, view, view, bias)
    return out.reshape(b, l, 2 * GROUP_W)


def _expand4(arr, g, lane):
    rows = arr.shape[0]
    c = [jnp.broadcast_to(arr[:, 4 * g + e:4 * g + e + 1], (rows, SSD_GROUP_W)) for e in range(4)]
    return jnp.where(lane < SSD_P, c[0], jnp.where(lane < 2 * SSD_P, c[1],
                                                    jnp.where(lane < 3 * SSD_P, c[2], c[3])))


def _cumsum_rows(a):
    row = lax.broadcasted_iota(jnp.int32, a.shape, 0)
    s = 1
    while s < a.shape[0]:
        a = a + jnp.where(row >= s, pltpu.roll(a, s, axis=0), 0.0)
        s *= 2
    return a


def _ssd_kernel(xbc_ref, dt_ref, z_ref, cw_ref, cb_ref, dtb_ref, alog_ref, dskip_ref, ng_ref,
                y_ref, st_ref, cs_ref, xp_scr, xc_scr, stt_scr):
    q = BAND
    c = pl.program_id(1)
    last = c == pl.num_programs(1) - 1

    @pl.when(c == 0)
    def _():
        xp_scr[0:SUBLANES, :] = jnp.zeros((SUBLANES, CONV_DIM), F32)
        stt_scr[...] = jnp.zeros(stt_scr.shape, F32)

    xp_scr[SUBLANES:SUBLANES + q, :] = xbc_ref[...]
    cblk = 512
    for j in range(CONV_DIM // cblk):
        cs = slice(j * cblk, (j + 1) * cblk)
        acc = cb_ref[:, cs]
        for t in range(D_CONV):
            lo = SUBLANES - (D_CONV - 1) + t
            acc = acc + xp_scr[lo:lo + q, cs] * cw_ref[t:t + 1, cs]
        xc_scr[:, cs] = _silu(acc)

    @pl.when(last)
    def _():
        cs_ref[...] = xp_scr[SUBLANES + q - (D_CONV - 1):SUBLANES + q, :]

    xp_scr[0:SUBLANES, :] = xp_scr[q:q + SUBLANES, :]

    dt = _softplus(dt_ref[:, 0:LANES] + dtb_ref[...])
    a = dt * (-jnp.exp(alog_ref[...]))
    acs = _cumsum_rows(a)
    acs_t = acs.T
    a_last = acs[q - 1:q, :]
    ea = jnp.exp(acs)
    w_end = dt * jnp.exp(a_last - acs)
    ea_last = jnp.exp(a_last)

    row = lax.broadcasted_iota(jnp.int32, (q, q), 0)
    colq = lax.broadcasted_iota(jnp.int32, (q, q), 1)
    tril = row >= colq
    lane = lax.broadcasted_iota(jnp.int32, (q, SSD_GROUP_W), 1)
    lane1 = lax.broadcasted_iota(jnp.int32, (1, SSD_GROUP_W), 1)

    for g in range(SSD_GROUPS):
        gs = slice(g * SSD_GROUP_W, (g + 1) * SSD_GROUP_W)
        bg = xc_scr[:, D_INNER + g * D_STATE:D_INNER + (g + 1) * D_STATE]
        cg = xc_scr[:, D_INNER + (SSD_GROUPS + g) * D_STATE:D_INNER + (SSD_GROUPS + g + 1) * D_STATE]
        bgb = bg.astype(BF16)
        cgb = cg.astype(BF16)
        cbm = lax.dot_general(cgb, bgb, NT_DIMS, preferred_element_type=F32)
        xg = xc_scr[:, gs]
        xdt = (xg * _expand4(dt, g, lane)).astype(BF16)
        yd = None
        for e in range(4):
            h = 4 * g + e
            seg = jnp.broadcast_to(acs[:, h:h + 1], (q, q)) - jnp.broadcast_to(acs_t[h:h + 1, :], (q, q))
            lm = jnp.exp(jnp.where(tril, seg, NEG))
            r = jnp.dot((cbm * lm).astype(BF16), xdt, preferred_element_type=F32)
            yd = r if e == 0 else jnp.where(lane >= e * SSD_P, r, yd)
        stg = stt_scr[:, gs]
        yoff = jnp.dot(cgb, stg.astype(BF16), preferred_element_type=F32) * _expand4(ea, g, lane)
        y = yd + yoff + dskip_ref[:, gs] * xg
        xw = (xg * _expand4(w_end, g, lane)).astype(BF16)
        stt_scr[:, gs] = stg * _expand4(ea_last, g, lane1) + jnp.dot(
            bg.T.astype(BF16), xw, preferred_element_type=F32)
        hg = y * _silu(z_ref[:, gs])
        y_ref[:, gs] = (_rms(hg) * ng_ref[:, gs]).astype(BF16)

    @pl.when(last)
    def _():
        for k in range(D_INNER // LANES):
            st_ref[k * LANES:(k + 1) * LANES, :] = stt_scr[:, k * LANES:(k + 1) * LANES].T


def _prompt_ssd(proj, conv_w, conv_b, dt_bias, a_log, d_skip, norm_g, layer):
    b, l, _ = proj.shape
    q = BAND
    vec = lambda w: pl.BlockSpec((None, 1, w), lambda bb, c: (layer, 0, 0))
    return pl.pallas_call(
        _ssd_kernel,
        grid=(b, l // q),
        in_specs=[
            pl.BlockSpec((None, q, CONV_DIM), lambda bb, c: (bb, c, OFF_XBC // CONV_DIM)),
            pl.BlockSpec((None, q, DT_W), lambda bb, c: (bb, c, OFF_DT // DT_W)),
            pl.BlockSpec((None, q, D_INNER), lambda bb, c: (bb, c, OFF_Z // D_INNER)),
            pl.BlockSpec((None, D_CONV, CONV_DIM), lambda bb, c: (layer, 0, 0)),
            vec(CONV_DIM), vec(LANES), vec(LANES), vec(D_INNER), vec(D_INNER),
        ],
        out_specs=[
            pl.BlockSpec((None, q, D_INNER), lambda bb, c: (bb, c, 0)),
            pl.BlockSpec((None, D_INNER, D_STATE), lambda bb, c: (bb, 0, 0)),
            pl.BlockSpec((None, D_CONV - 1, CONV_DIM), lambda bb, c: (bb, 0, 0)),
        ],
        out_shape=[
            jax.ShapeDtypeStruct((b, l, D_INNER), BF16),
            jax.ShapeDtypeStruct((b, D_INNER, D_STATE), F32),
            jax.ShapeDtypeStruct((b, D_CONV - 1, CONV_DIM), F32),
        ],
        scratch_shapes=[
            pltpu.VMEM((q + SUBLANES, CONV_DIM), F32),
            pltpu.VMEM((q, CONV_DIM), F32),
            pltpu.VMEM((D_STATE, D_INNER), F32),
        ],
        compiler_params=_params(("arbitrary", "arbitrary")),
        name="prompt_ssd",
    )(proj, proj, proj, conv_w, conv_b, dt_bias, a_log, d_skip, norm_g)


def _out_proj_kernel(*refs, n_attn):
    a_refs = refs[:n_attn]
    ssd_ref, gate_ref, x_ref, g1_ref, wa_ref, ws_ref, wo_ref, o_ref = refs[n_attn:]
    if n_attn == 1:
        attn = a_refs[0][...]
    else:
        lse = [r[:, GROUP_W:2 * GROUP_W] for r in a_refs]
        m = functools.reduce(jnp.maximum, lse)
        e = [jnp.exp(v - m) for v in lse]
        attn = sum(ev * r[:, 0:GROUP_W] for ev, r in zip(e, a_refs)) / sum(e)
    pa = jnp.dot(attn.astype(BF16), wa_ref[...], preferred_element_type=F32)
    ps = jnp.dot(ssd_ref[...], ws_ref[...], preferred_element_type=F32)
    merged = _sigmoid(gate_ref[:, 0:D_MODEL]) * pa + _sigmoid(gate_ref[:, D_MODEL:2 * D_MODEL]) * ps
    o_ref[...] = x_ref[...] + g1_ref[...] * jnp.dot(
        merged.astype(BF16), wo_ref[...], preferred_element_type=F32)


def _out_proj(attn_list, ssd, proj, x, mod, w_o_attn, w_o_ssd, w_out, layer, tm):
    bx, lx, d = x.shape
    r = mod.shape[2]
    row = lambda w, cb=0: pl.BlockSpec((None, tm, w), lambda b, i: (b, i, cb))
    wgt = lambda k, n: pl.BlockSpec((None, k, n), lambda b, i: (layer, 0, 0))
    attn_w = attn_list[0].shape[-1]
    return pl.pallas_call(
        functools.partial(_out_proj_kernel, n_attn=len(attn_list)),
        grid=(bx, lx // tm),
        in_specs=[row(attn_w)] * len(attn_list) + [
            row(D_INNER), row(2 * D_MODEL, OFF_GATE // (2 * D_MODEL)), row(d),
            pl.BlockSpec((None, None, r, d), lambda b, i: (layer, b, 0, 2)),
            wgt(GROUP_W, d), wgt(D_INNER, d), wgt(d, d),
        ],
        out_specs=row(d),
        out_shape=jax.ShapeDtypeStruct((bx, lx, d), F32),
        compiler_params=_params(("arbitrary", "arbitrary")),
        name="out_proj",
    )(*attn_list, ssd, proj, x, mod, w_o_attn, w_o_ssd, w_out)


def _mlp_kernel(x_ref, mod_ref, g_ref, wu_ref, wd_ref, fg_ref, o_ref, h_scr, acc_scr, *, final):
    f = pl.program_id(2)

    @pl.when(f == 0)
    def _():
        y = _rms(x_ref[...]) * g_ref[...]
        h = y * (1.0 + mod_ref[:, D_MODEL:2 * D_MODEL]) + mod_ref[:, 0:D_MODEL]
        h_scr[...] = h.astype(BF16)
        acc_scr[...] = jnp.zeros(acc_scr.shape, F32)

    u = jnp.maximum(jnp.dot(h_scr[...], wu_ref[...], preferred_element_type=F32), 0.0)
    acc_scr[...] += jnp.dot((u * u).astype(BF16), wd_ref[...], preferred_element_type=F32)

    @pl.when(f == pl.num_programs(2) - 1)
    def _():
        x2 = x_ref[...] + mod_ref[:, 2 * D_MODEL:3 * D_MODEL] * acc_scr[...]
        if final:
            x2 = _rms(x2) * fg_ref[...]
        o_ref[...] = x2


def _mlp(x, mod, norm_g, w_up, w_down, final_g, layer, tm, final):
    bx, lx, d = x.shape
    r = mod.shape[2]
    tf = 1024
    return pl.pallas_call(
        functools.partial(_mlp_kernel, final=final),
        grid=(bx, lx // tm, D_FF // tf),
        in_specs=[
            pl.BlockSpec((None, tm, d), lambda b, i, f: (b, i, 0)),
            pl.BlockSpec((None, None, r, 3 * d), lambda b, i, f: (layer, b, 0, 1)),
            pl.BlockSpec((None, 1, d), lambda b, i, f: (layer, 0, 0)),
            pl.BlockSpec((None, d, tf), lambda b, i, f: (layer, 0, f)),
            pl.BlockSpec((None, tf, d), lambda b, i, f: (layer, f, 0)),
            pl.BlockSpec((1, d), lambda b, i, f: (0, 0)),
        ],
        out_specs=pl.BlockSpec((None, tm, d), lambda b, i, f: (b, i, 0)),
        out_shape=jax.ShapeDtypeStruct((bx, lx, d), F32),
        scratch_shapes=[pltpu.VMEM((tm, d), BF16), pltpu.VMEM((tm, d), F32)],
        compiler_params=_params(("arbitrary", "arbitrary", "arbitrary")),
        name="mlp",
    )(x, mod, norm_g, w_up, w_down, final_g)


def _dec_attn_kernel(proj_ref, c0_ref, c1_ref, c2_ref, bias_ref, o_ref):
    row = lax.broadcasted_iota(jnp.int32, (SUBLANES, GROUP_W), 0)
    lane = lax.broadcasted_iota(jnp.int32, (SUBLANES, GROUP_W), 1)
    hmask = jnp.logical_and(lane >= row * HEAD_DIM, lane < (row + 1) * HEAD_DIM)
    parts = []
    for g, c_ref in enumerate((c0_ref, c1_ref, c2_ref)):
        qv = proj_ref[:, OFF_Q + g * GROUP_W:OFF_Q + (g + 1) * GROUP_W]
        kn = proj_ref[:, OFF_K + g * GROUP_W:OFF_K + (g + 1) * GROUP_W]
        vn = proj_ref[:, OFF_V + g * GROUP_W:OFF_V + (g + 1) * GROUP_W]
        qbd = jnp.where(hmask, jnp.broadcast_to(qv, (SUBLANES, GROUP_W)), 0.0)
        kc = jnp.concatenate([c_ref[:, 0, h, :] for h in range(HEADS_PER_GROUP)], axis=1)
        vc = jnp.concatenate([c_ref[:, 1, h, :] for h in range(HEADS_PER_GROUP)], axis=1)
        bias = bias_ref[g]
        s = lax.dot_general(qbd.astype(BF16), kc.astype(BF16), NT_DIMS, preferred_element_type=F32)
        s = s * ATTN_SCALE + bias[:, 0:BAND]
        s_self = jnp.sum(qbd * kn, axis=-1, keepdims=True) * ATTN_SCALE + bias[:, BAND:BAND + 1]
        m = jnp.maximum(jnp.max(s, axis=-1, keepdims=True), s_self)
        p = jnp.exp(s - m)
        p_self = jnp.exp(s_self - m)
        l = jnp.sum(p, axis=-1, keepdims=True) + p_self
        acc = jnp.dot(p.astype(BF16), vc.astype(BF16), preferred_element_type=F32) + p_self * vn
        parts.append((m, l, acc))
    m_all = functools.reduce(jnp.maximum, [p[0] for p in parts])
    num = sum(jnp.exp(m - m_all) * acc for m, _, acc in parts)
    den = sum(jnp.exp(m - m_all) * l for m, l, _ in parts)
    o_ref[...] = jnp.sum(jnp.where(hmask, num / den, 0.0), axis=0, keepdims=True)


def _decode_attention(proj_s, caches, bias_dec, layer):
    bs = proj_s.shape[1]
    specs = []
    views = []
    for cache, (win, dil) in zip(caches, DIL_GROUPS):
        depth, _, lb = cache.shape[:3]
        views.append(cache.reshape(depth, bs, lb // dil, dil, 2, HEADS_PER_GROUP, HEAD_DIM))
        specs.append(pl.BlockSpec((None, None, BAND, None, 2, HEADS_PER_GROUP, HEAD_DIM),
                                  lambda b: (layer, b, 0, 0, 0, 0, 0)))
    out = pl.pallas_call(
        _dec_attn_kernel,
        grid=(bs,),
        in_specs=[pl.BlockSpec((None, 1, PROJ_W), lambda b: (b, 0, 0))] + specs + [
            pl.BlockSpec((N_GROUPS, SUBLANES, 2 * BAND), lambda b: (0, 0, 0))],
        out_specs=pl.BlockSpec((None, 1, GROUP_W), lambda b: (b, 0, 0)),
        out_shape=jax.ShapeDtypeStruct((bs, 1, GROUP_W), F32),
        compiler_params=_params(("arbitrary",)),
        name="decode_attn",
    )(proj_s.reshape(bs, 1, PROJ_W), *views, bias_dec)
    return out.reshape(1, bs, GROUP_W)


def _dec_ssd_kernel(proj_ref, cs_ref, st_ref, cw_ref, cb_ref, dtb_ref, alog_ref, dskip_ref, ng_ref,
                    y_ref, nst_ref, ncs_ref, xc_scr, xdt_scr, dae_scr, yacc_scr):
    b = pl.program_id(0)
    hi = lax.Precision.HIGHEST

    @pl.when(b == 0)
    def _():
        xr = proj_ref[:, OFF_XBC:OFF_XBC + CONV_DIM]
        acc = cb_ref[...]
        for t in range(D_CONV - 1):
            acc = acc + cs_ref[t] * cw_ref[t:t + 1, :]
        acc = acc + xr * cw_ref[D_CONV - 1:D_CONV, :]
        xc = _silu(acc)
        xc_scr[...] = xc
        for t in range(D_CONV - 2):
            ncs_ref[t] = cs_ref[t + 1]
        ncs_ref[D_CONV - 2] = xr
        dt = _softplus(proj_ref[:, OFF_DT:OFF_DT + LANES] + dtb_ref[...])
        da = jnp.exp(dt * (-jnp.exp(alog_ref[...])))
        onehot = (lax.shift_right_logical(lax.broadcasted_iota(jnp.int32, (LANES, D_INNER), 1), 6)
                  == lax.broadcasted_iota(jnp.int32, (LANES, D_INNER), 0)).astype(F32)
        xdt_scr[...] = xc[:, 0:D_INNER] * jnp.dot(dt, onehot, precision=hi, preferred_element_type=F32)
        dae_scr[...] = jnp.dot(da, onehot, precision=hi, preferred_element_type=F32)

    r8 = lax.broadcasted_iota(jnp.int32, (SUBLANES, D_INNER), 0)
    rows2 = jnp.where(r8 == 0, jnp.broadcast_to(xdt_scr[pl.ds(b, 1), :], (SUBLANES, D_INNER)),
                      jnp.where(r8 == 1, jnp.broadcast_to(dae_scr[pl.ds(b, 1), :], (SUBLANES, D_INNER)), 0.0))
    s8 = lax.broadcasted_iota(jnp.int32, (SUBLANES, 2 * D_STATE), 0)
    l8 = lax.broadcasted_iota(jnp.int32, (SUBLANES, 2 * D_STATE), 1)
    sel = jnp.logical_or(jnp.logical_and(s8 == 0, l8 < D_STATE),
                         jnp.logical_and(s8 == 1, l8 >= D_STATE)).astype(F32)
    cols = lax.dot_general(rows2, sel, TN_DIMS, precision=hi, preferred_element_type=F32)

    xc_row = xc_scr[pl.ds(b, 1), :]
    c8 = lax.broadcasted_iota(jnp.int32, (SUBLANES, D_STATE), 0)
    for g in range(SSD_GROUPS):
        rs = slice(g * SSD_GROUP_W, (g + 1) * SSD_GROUP_W)
        bg = xc_row[:, D_INNER + g * D_STATE:D_INNER + (g + 1) * D_STATE]
        cg = xc_row[:, D_INNER + (SSD_GROUPS + g) * D_STATE:D_INNER + (SSD_GROUPS + g + 1) * D_STATE]
        hn = st_ref[rs, :] * cols[rs, D_STATE:2 * D_STATE] + cols[rs, 0:D_STATE] * bg
        nst_ref[rs, :] = hn
        cpad = jnp.where(c8 == 0, jnp.broadcast_to(cg, (SUBLANES, D_STATE)), 0.0)
        yg = lax.dot_general(cpad, hn, NT_DIMS, precision=hi, preferred_element_type=F32)
        yacc_scr[pl.ds(b, 1), rs] = yg[0:1, :]

    @pl.when(b == pl.num_programs(0) - 1)
    def _():
        for g in range(SSD_GROUPS):
            gs = slice(g * SSD_GROUP_W, (g + 1) * SSD_GROUP_W)
            y = yacc_scr[:, gs] + dskip_ref[:, gs] * xc_scr[:, gs]
            hg = y * _silu(proj_ref[:, OFF_Z + g * SSD_GROUP_W:OFF_Z + (g + 1) * SSD_GROUP_W])
            y_ref[:, gs] = (_rms(hg) * ng_ref[:, gs]).astype(BF16)


def _decode_ssd(proj_s, conv_state_t, state, conv_w, conv_b, dt_bias, a_log, d_skip, norm_g, layer):
    bs = proj_s.shape[1]
    vec = lambda w: pl.BlockSpec((None, 1, w), lambda b: (layer, 0, 0))
    return pl.pallas_call(
        _dec_ssd_kernel,
        grid=(bs,),
        in_specs=[
            pl.BlockSpec((None, bs, PROJ_W), lambda b: (0, 0, 0)),
            pl.BlockSpec((None, D_CONV - 1, bs, CONV_DIM), lambda b: (layer, 0, 0, 0)),
            pl.BlockSpec((None, None, D_INNER, D_STATE), lambda b: (layer, b, 0, 0)),
            pl.BlockSpec((None, D_CONV, CONV_DIM), lambda b: (layer, 0, 0)),
            vec(CONV_DIM), vec(LANES), vec(LANES), vec(D_INNER), vec(D_INNER),
        ],
        out_specs=[
            pl.BlockSpec((None, bs, D_INNER), lambda b: (0, 0, 0)),
            pl.BlockSpec((None, D_INNER, D_STATE), lambda b: (b, 0, 0)),
            pl.BlockSpec((D_CONV - 1, bs, CONV_DIM), lambda b: (0, 0, 0)),
        ],
        out_shape=[
            jax.ShapeDtypeStruct((1, bs, D_INNER), BF16),
            jax.ShapeDtypeStruct((bs, D_INNER, D_STATE), F32),
            jax.ShapeDtypeStruct((D_CONV - 1, bs, CONV_DIM), F32),
        ],
        scratch_shapes=[
            pltpu.VMEM((bs, CONV_DIM), F32),
            pltpu.VMEM((bs, D_INNER), F32),
            pltpu.VMEM((bs, D_INNER), F32),
            pltpu.VMEM((bs, D_INNER), F32),
        ],
        compiler_params=_params(("arbitrary",)),
        name="decode_ssd",
    )(proj_s, conv_state_t, state, conv_w, conv_b, dt_bias, a_log, d_skip, norm_g)


CACHE_BATCH_CHUNKS = 4


def _cache_copies(c_refs, n_refs, o_refs, sem):
    copies = []
    for c_ref, n_ref, o_ref in zip(c_refs, n_refs, o_refs):
        depth, bs, lb = c_ref.shape[:3]
        nb = bs // CACHE_BATCH_CHUNKS
        for l in range(depth):
            for j in range(CACHE_BATCH_CHUNKS):
                copies.append(pltpu.make_async_copy(
                    c_ref.at[l, pl.ds(j * nb, nb), pl.ds(1, lb - 1)],
                    o_ref.at[l, pl.ds(j * nb, nb), pl.ds(0, lb - 1)],
                    sem.at[len(copies)]))
            copies.append(pltpu.make_async_copy(
                n_ref.at[l], o_ref.at[l, pl.ds(0, bs), pl.ds(lb - 1, 1)], sem.at[len(copies)]))
    return copies


def _cache_kernel(*refs):
    n = N_GROUPS
    copies = _cache_copies(refs[0:n], refs[n:2 * n], refs[2 * n:3 * n], refs[3 * n])
    for cp in copies:
        cp.start()
    for cp in copies:
        cp.wait()


def _cache_update(caches, new_rows):
    depth = caches[0].shape[0]
    n_copies = N_GROUPS * depth * (CACHE_BATCH_CHUNKS + 1)
    any_spec = pl.BlockSpec(memory_space=pl.ANY)
    return pl.pallas_call(
        _cache_kernel,
        in_specs=[any_spec] * (2 * N_GROUPS),
        out_specs=[any_spec] * N_GROUPS,
        out_shape=[jax.ShapeDtypeStruct(c.shape, c.dtype) for c in caches],
        scratch_shapes=[pltpu.SemaphoreType.DMA((n_copies,))],
        name="cache_update",
    )(*caches, *new_rows)


def _reorder_w_in(w_in):
    q0, z0, x0, d0, g0 = 0, 3 * QKV_W, 3 * QKV_W + D_INNER, 3 * QKV_W + D_INNER + CONV_DIM, \
        3 * QKV_W + D_INNER + CONV_DIM + SSD_HEADS
    pieces = [w_in[..., z0:x0], w_in[..., g0:g0 + 2 * D_MODEL], w_in[..., x0:d0], w_in[..., q0:z0],
              w_in[..., d0:g0], jnp.zeros(w_in.shape[:-1] + (DT_W - SSD_HEADS,), w_in.dtype)]
    return jnp.concatenate(pieces, axis=-1).astype(BF16)


def _pad_lanes(v, width):
    return jnp.pad(v, ((0, 0), (0, width - v.shape[-1])))[:, None, :]


def kernel(x_prompt, x_sample, cache_kv_g0, cache_kv_g1, cache_kv_g2, state_ssm, state_conv, c_prompt,
           c_sample, rel_bias, w_ada, b_ada, norm1_g, norm2_g, w_in, conv_w, conv_b, dt_bias, a_log, d_skip,
           ssd_norm_g, w_o_attn, w_o_ssd, w_out, w_up, w_down, final_g):
    depth = w_in.shape[0]
    b, l, d = x_prompt.shape
    bs = x_sample.shape[0]
    caches = (cache_kv_g0, cache_kv_g1, cache_kv_g2)
    assert d == D_MODEL and x_sample.shape[1] == 1 and l % (DIL_GROUPS[-1][1] * BAND) == 0
    assert all(win == BAND * dil for win, dil in DIL_GROUPS)
    assert all(c.shape[2] == win for c, (win, _) in zip(caches, DIL_GROUPS))
    assert bs % SUBLANES == 0 and bs % CACHE_BATCH_CHUNKS == 0

    w_in_r = _reorder_w_in(w_in)
    w_oa, w_os, w_o, w_u, w_d = (w.astype(BF16) for w in (w_o_attn, w_o_ssd, w_out, w_up, w_down))
    n1, n2 = norm1_g[:, None, :], norm2_g[:, None, :]
    conv_b3, ng3 = conv_b[:, None, :], ssd_norm_g[:, None, :]
    dtb3, alog3 = _pad_lanes(dt_bias, LANES), _pad_lanes(a_log, LANES)
    dskip3 = jnp.repeat(d_skip, SSD_P, axis=-1)[:, None, :]
    fg = final_g[None, :]

    rows = b + bs
    rows_pad = -(-rows // SUBLANES) * SUBLANES
    c_all = jnp.pad(jnp.concatenate([c_prompt, c_sample], axis=0), ((0, rows_pad - rows), (0, 0)))
    mod = _ada_mod(c_all, w_ada, b_ada)
    mod_p = mod[:, :b, None, :]
    mod_s = mod[:, None, b:rows, :]

    bias = _bias_tiles(rel_bias)
    bias_dec = jnp.pad(bias[:, 0, :].reshape(N_GROUPS, HEADS_PER_GROUP, 2 * BAND),
                       ((0, 0), (0, SUBLANES - HEADS_PER_GROUP), (0, 0)))

    xp = x_prompt
    xs = x_sample.reshape(1, bs, d)
    conv_state_t = jnp.swapaxes(state_conv, 1, 2)
    state_r = state_ssm.reshape(depth, bs, D_INNER, D_STATE)
    tm_p = 1024 if l % 1024 == 0 else BAND

    kv_p = [[] for _ in DIL_GROUPS]
    new_rows = [[] for _ in DIL_GROUPS]
    ssm_p, conv_p, ssm_s, conv_s = [], [], [], []
    for layer in range(depth):
        final = layer == depth - 1
        proj = _in_proj(xp, mod_p, n1, w_in_r, layer, tm_p)
        attn = [_prompt_attention(proj, bias, g) for g in range(N_GROUPS)]
        ssd, st, cs = _prompt_ssd(proj, conv_w, conv_b3, dtb3, alog3, dskip3, ng3, layer)
        x1 = _out_proj(attn, ssd, proj, xp, mod_p, w_oa, w_os, w_o, layer, min(256, l))
        xp = _mlp(x1, mod_p, n2, w_u, w_d, fg, layer, tm_p, final)
        for g, (win, _) in enumerate(DIL_GROUPS):
            keep = min(win, l)
            kk = proj[:, l - keep:, OFF_K + g * GROUP_W:OFF_K + (g + 1) * GROUP_W]
            vv = proj[:, l - keep:, OFF_V + g * GROUP_W:OFF_V + (g + 1) * GROUP_W]
            kv_p[g].append(jnp.stack([kk, vv], axis=2).reshape(b, keep, 2, HEADS_PER_GROUP, HEAD_DIM))
        ssm_p.append(st.reshape(b, SSD_HEADS, SSD_P, D_STATE))
        conv_p.append(cs)
        proj_s = _in_proj(xs, mod_s, n1, w_in_r, layer, bs)
        attn_s = _decode_attention(proj_s, caches, bias_dec, layer)
        ssd_s, st_s, cs_s = _decode_ssd(proj_s, conv_state_t, state_r, conv_w, conv_b3, dtb3, alog3,
                                        dskip3, ng3, layer)
        x1s = _out_proj([attn_s], ssd_s, proj_s, xs, mod_s, w_oa, w_os, w_o, layer, bs)
        xs = _mlp(x1s, mod_s, n2, w_u, w_d, fg, layer, bs, final)
        for g in range(N_GROUPS):
            kk = proj_s[0, :, OFF_K + g * GROUP_W:OFF_K + (g + 1) * GROUP_W]
            vv = proj_s[0, :, OFF_V + g * GROUP_W:OFF_V + (g + 1) * GROUP_W]
            new_rows[g].append(jnp.stack([kk, vv], axis=1).reshape(bs, 1, 2, HEADS_PER_GROUP, HEAD_DIM))
        ssm_s.append(st_s.reshape(bs, SSD_HEADS, SSD_P, D_STATE))
        conv_s.append(jnp.swapaxes(cs_s, 0, 1))

    kv_s = _cache_update(caches, [jnp.stack(n, axis=0) for n in new_rows])
    return (xp, xs.reshape(bs, 1, d),
            jnp.stack(kv_p[0], axis=0), jnp.stack(kv_p[1], axis=0), jnp.stack(kv_p[2], axis=0),
            jnp.stack(ssm_p, axis=0), jnp.stack(conv_p, axis=0),
            kv_s[0], kv_s[1], kv_s[2],
            jnp.stack(ssm_s, axis=0), jnp.stack(conv_s, axis=0))
```

```python
import functools
import math

import jax
import jax.numpy as jnp
import numpy as np
from jax import lax
from jax.experimental import pallas as pl
from jax.experimental.pallas import tpu as pltpu

F32 = jnp.float32
BF16 = jnp.bfloat16

D_MODEL = 1024
HEAD_DIM = 64
HEADS_PER_GROUP = 4
DIL_GROUPS = ((128, 1), (512, 4), (2048, 16))
N_GROUPS = len(DIL_GROUPS)
GROUP_W = HEADS_PER_GROUP * HEAD_DIM
QKV_W = N_GROUPS * GROUP_W
BAND = 128
NUM_BUCKETS = 32
MAX_DISTANCE = 2048
D_INNER = 2 * D_MODEL
SSD_HEADS = 32
SSD_P = 64
SSD_GROUPS = 8
SSD_GROUP_W = D_INNER // SSD_GROUPS
D_STATE = 128
D_CONV = 4
CONV_DIM = D_INNER + 2 * SSD_GROUPS * D_STATE
D_FF = 4 * D_MODEL
EPS = 1e-6
ATTN_SCALE = HEAD_DIM ** -0.5
NEG = -1e30

SUBLANES = 8
LANES = 128
V7X_VMEM_LIMIT = 52 * 1024 * 1024

OFF_Z = 0
OFF_GATE = D_INNER
OFF_XBC = 2 * D_INNER
OFF_Q = OFF_XBC + CONV_DIM
OFF_K = OFF_Q + QKV_W
OFF_V = OFF_K + QKV_W
OFF_DT = OFF_V + QKV_W
DT_W = 2 * LANES
PROJ_W = OFF_DT + DT_W

NT_DIMS = (((1,), (1,)), ((), ()))
TN_DIMS = (((0,), (0,)), ((), ()))


def _params(semantics, vmem=V7X_VMEM_LIMIT):
    return pltpu.CompilerParams(dimension_semantics=semantics, vmem_limit_bytes=vmem)


def _sigmoid(x):
    return 1.0 / (1.0 + jnp.exp(-x))


def _silu(x):
    return x * _sigmoid(x)


def _softplus(x):
    return jnp.maximum(x, 0.0) + jnp.log(1.0 + jnp.exp(-jnp.abs(x)))


def _rms(x):
    return x * lax.rsqrt(jnp.mean(x * x, axis=-1, keepdims=True) + EPS)


def _ada_kernel(c_ref, w_ref, b_ref, o_ref):
    s = _silu(c_ref[...]).astype(BF16)
    o_ref[...] = jnp.dot(s, w_ref[...].astype(BF16), preferred_element_type=F32) + b_ref[...]


def _ada_mod(c_all, w_ada, b_ada):
    depth, d, n = w_ada.shape
    rows = c_all.shape[0]
    tn = 1536
    return pl.pallas_call(
        _ada_kernel,
        grid=(depth, n // tn),
        in_specs=[
            pl.BlockSpec((rows, d), lambda l, j: (0, 0)),
            pl.BlockSpec((None, d, tn), lambda l, j: (l, 0, j)),
            pl.BlockSpec((None, 1, tn), lambda l, j: (l, 0, j)),
        ],
        out_specs=pl.BlockSpec((None, rows, tn), lambda l, j: (l, 0, j)),
        out_shape=jax.ShapeDtypeStruct((depth, rows, n), F32),
        compiler_params=_params(("arbitrary", "arbitrary")),
        name="ada_mod",
    )(c_all, w_ada, b_ada.reshape(depth, 1, n))


def _in_proj_kernel(x_ref, mod_ref, g_ref, w_ref, o_ref, h_scr):
    @pl.when(pl.program_id(2) == 0)
    def _():
        y = _rms(x_ref[...]) * g_ref[...]
        h = y * (1.0 + mod_ref[:, D_MODEL:2 * D_MODEL]) + mod_ref[:, 0:D_MODEL]
        h_scr[...] = h.astype(BF16)

    o_ref[...] = jnp.dot(h_scr[...], w_ref[...], preferred_element_type=F32)


def _in_proj(x, mod, norm_g, w_in_r, layer, tm):
    bx, lx, d = x.shape
    r = mod.shape[2]
    tn = 1536
    return pl.pallas_call(
        _in_proj_kernel,
        grid=(bx, lx // tm, PROJ_W // tn),
        in_specs=[
            pl.BlockSpec((None, tm, d), lambda b, i, n: (b, i, 0)),
            pl.BlockSpec((None, None, r, 2 * d), lambda b, i, n: (layer, b, 0, 0)),
            pl.BlockSpec((None, 1, d), lambda b, i, n: (layer, 0, 0)),
            pl.BlockSpec((None, d, tn), lambda b, i, n: (layer, 0, n)),
        ],
        out_specs=pl.BlockSpec((None, tm, tn), lambda b, i, n: (b, i, n)),
        out_shape=jax.ShapeDtypeStruct((bx, lx, PROJ_W), F32),
        scratch_shapes=[pltpu.VMEM((tm, d), BF16)],
        compiler_params=_params(("arbitrary", "arbitrary", "arbitrary")),
        name="in_proj",
    )(x, mod, norm_g, w_in_r)


def _t5_bucket_np(dist):
    max_exact = NUM_BUCKETS // 2
    df = np.maximum(dist, 1).astype(np.float32)
    ratio = np.log(df / np.float32(max_exact)) / np.float32(math.log(MAX_DISTANCE / max_exact))
    large = max_exact + (ratio * np.float32(NUM_BUCKETS - max_exact)).astype(np.int32)
    large = np.minimum(large, NUM_BUCKETS - 1)
    return np.where(dist < max_exact, dist, large).astype(np.int32)


def _bucket_tiles():
    q = np.arange(BAND)[:, None]
    c = np.arange(2 * BAND)[None, :]
    j = q + BAND - c
    valid = (j >= 0) & (j <= BAND)
    tiles = []
    for _, dil in DIL_GROUPS:
        b = _t5_bucket_np(np.clip(j, 0, BAND) * dil)
        tiles.append(np.where(valid, b, -1))
    return np.stack(tiles).astype(np.int32)


def _bucket_rows():
    width = max(win for win, _ in DIL_GROUPS)
    i = np.arange(width)
    rows = []
    for win, dil in DIL_GROUPS:
        dist = win - i
        valid = (i < win) & (dist % dil == 0)
        rows.append(np.where(valid, _t5_bucket_np(np.clip(dist, 0, win)), -1))
    return np.broadcast_to(np.stack(rows)[:, None, :], (N_GROUPS, SUBLANES, width)).astype(np.int32)


def _bias_kernel(rb_ref, bk_ref, o_ref):
    hh = pl.program_id(0)
    bk = bk_ref[...]
    acc = jnp.full(bk.shape, NEG, F32)
    for b in range(NUM_BUCKETS):
        acc = jnp.where(bk == b, rb_ref[b, hh], acc)
    o_ref[...] = acc


def _bias_lookup(rel_bias, buckets):
    n_heads = rel_bias.shape[1]
    blk = (None,) + buckets.shape[1:]
    return pl.pallas_call(
        _bias_kernel,
        grid=(n_heads,),
        in_specs=[
            pl.BlockSpec(memory_space=pltpu.SMEM),
            pl.BlockSpec(blk, lambda h: (h // HEADS_PER_GROUP, 0, 0)),
        ],
        out_specs=pl.BlockSpec(blk, lambda h: (h, 0, 0)),
        out_shape=jax.ShapeDtypeStruct((n_heads,) + buckets.shape[1:], F32),
        compiler_params=_params(("arbitrary",)),
        name="bias_lookup",
    )(rel_bias, jnp.asarray(buckets))


HEADS_PER_TILE = LANES // HEAD_DIM


def _attn_kernel(q_ref, kp_ref, kc_ref, vp_ref, vc_ref, bias_ref, o_ref, lse_ref, *, dil):
    first = pl.program_id(1) == 0
    col = lax.broadcasted_iota(jnp.int32, (BAND, 2 * BAND), 1)
    pen = jnp.where(col < BAND, jnp.where(first, NEG, 0.0), 0.0)
    lane = lax.broadcasted_iota(jnp.int32, (BAND, LANES), 1)

    def residue(r, carry):
        rows = pl.ds(r, BAND, stride=dil) if dil > 1 else pl.ds(0, BAND)
        q = q_ref[rows, :]
        kb = jnp.concatenate([kp_ref[rows, :], kc_ref[rows, :]], axis=0).astype(BF16)
        vb = jnp.concatenate([vp_ref[rows, :], vc_ref[rows, :]], axis=0).astype(BF16)
        o, lse = None, None
        for h in range(HEADS_PER_TILE):
            mine = jnp.logical_and(lane >= h * HEAD_DIM, lane < (h + 1) * HEAD_DIM)
            qh = jnp.where(mine, q, 0.0).astype(BF16)
            s = lax.dot_general(qh, kb, NT_DIMS, preferred_element_type=F32)
            s = s * ATTN_SCALE + bias_ref[h] + pen
            m = jnp.max(s, axis=-1, keepdims=True)
            p = jnp.exp(s - m)
            l = jnp.sum(p, axis=-1, keepdims=True)
            oh = jnp.dot(p.astype(BF16), vb, preferred_element_type=F32) / l
            lh = jnp.broadcast_to(m + jnp.log(l), (BAND, LANES))
            o = oh if h == 0 else jnp.where(mine, oh, o)
            lse = lh if h == 0 else jnp.where(mine, lh, lse)
        o_ref[rows, :] = o
        lse_ref[rows, :] = lse
        return carry

    if dil > 1:
        lax.fori_loop(0, dil, residue, 0)
    else:
        residue(0, 0)


def _prompt_attention(proj, bias, g):
    b, l, _ = proj.shape
    dil = DIL_GROUPS[g][1]
    rows = BAND * dil
    tiles = GROUP_W // LANES
    qb, kb, vb = (off // LANES + g * tiles for off in (OFF_Q, OFF_K, OFF_V))
    blk = (None, rows, LANES)
    cur = lambda cb: pl.BlockSpec(blk, lambda bb, i, t: (bb, i, cb + t))
    prev = lambda cb: pl.BlockSpec(blk, lambda bb, i, t: (bb, jnp.maximum(i - 1, 0), cb + t))
    out_spec = pl.BlockSpec(blk, lambda bb, i, t: (bb, i, t))
    return pl.pallas_call(
        functools.partial(_attn_kernel, dil=dil),
        grid=(b, l // rows, tiles),
        in_specs=[cur(qb), prev(kb), cur(kb), prev(vb), cur(vb),
                  pl.BlockSpec((HEADS_PER_TILE, BAND, 2 * BAND), lambda bb, i, t: (g * tiles + t, 0, 0))],
        out_specs=[out_spec, out_spec],
        out_shape=[jax.ShapeDtypeStruct((b, l, GROUP_W), F32)] * 2,
        compiler_params=_params(("arbitrary", "arbitrary", "arbitrary")),
        name=f"prompt_attn_g{g}",
    )(proj, proj, proj, proj, proj, bias)


def _expand4(arr, g, lane):
    rows = arr.shape[0]
    c = [jnp.broadcast_to(arr[:, 4 * g + e:4 * g + e + 1], (rows, SSD_GROUP_W)) for e in range(4)]
    return jnp.where(lane < SSD_P, c[0], jnp.where(lane < 2 * SSD_P, c[1],
                                                    jnp.where(lane < 3 * SSD_P, c[2], c[3])))


def _cumsum_rows(a):
    row = lax.broadcasted_iota(jnp.int32, a.shape, 0)
    s = 1
    while s < a.shape[0]:
        a = a + jnp.where(row >= s, pltpu.roll(a, s, axis=0), 0.0)
        s *= 2
    return a


def _ssd_kernel(xbc_ref, dt_ref, z_ref, cw_ref, cb_ref, dtb_ref, alog_ref, dskip_ref, ng_ref,
                y_ref, st_ref, cs_ref, xp_scr, xc_scr, stt_scr):
    q = BAND
    c = pl.program_id(1)
    last = c == pl.num_programs(1) - 1

    @pl.when(c == 0)
    def _():
        xp_scr[0:SUBLANES, :] = jnp.zeros((SUBLANES, CONV_DIM), F32)
        stt_scr[...] = jnp.zeros(stt_scr.shape, F32)

    xp_scr[SUBLANES:SUBLANES + q, :] = xbc_ref[...]
    cblk = 512
    for j in range(CONV_DIM // cblk):
        cs = slice(j * cblk, (j + 1) * cblk)
        acc = cb_ref[:, cs]
        for t in range(D_CONV):
            lo = SUBLANES - (D_CONV - 1) + t
            acc = acc + xp_scr[lo:lo + q, cs] * cw_ref[t:t + 1, cs]
        xc_scr[:, cs] = _silu(acc)

    @pl.when(last)
    def _():
        cs_ref[...] = xp_scr[SUBLANES + q - (D_CONV - 1):SUBLANES + q, :]

    xp_scr[0:SUBLANES, :] = xp_scr[q:q + SUBLANES, :]

    dt = _softplus(dt_ref[:, 0:LANES] + dtb_ref[...])
    a = dt * (-jnp.exp(alog_ref[...]))
    acs = _cumsum_rows(a)
    acs_t = acs.T
    a_last = acs[q - 1:q, :]
    ea = jnp.exp(acs)
    w_end = dt * jnp.exp(a_last - acs)
    ea_last = jnp.exp(a_last)

    row = lax.broadcasted_iota(jnp.int32, (q, q), 0)
    colq = lax.broadcasted_iota(jnp.int32, (q, q), 1)
    tril = row >= colq
    lane = lax.broadcasted_iota(jnp.int32, (q, SSD_GROUP_W), 1)
    lane1 = lax.broadcasted_iota(jnp.int32, (1, SSD_GROUP_W), 1)

    for g in range(SSD_GROUPS):
        gs = slice(g * SSD_GROUP_W, (g + 1) * SSD_GROUP_W)
        bg = xc_scr[:, D_INNER + g * D_STATE:D_INNER + (g + 1) * D_STATE]
        cg = xc_scr[:, D_INNER + (SSD_GROUPS + g) * D_STATE:D_INNER + (SSD_GROUPS + g + 1) * D_STATE]
        bgb = bg.astype(BF16)
        cgb = cg.astype(BF16)
        cbm = lax.dot_general(cgb, bgb, NT_DIMS, preferred_element_type=F32)
        xg = xc_scr[:, gs]
        xdt = (xg * _expand4(dt, g, lane)).astype(BF16)
        yd = None
        for e in range(4):
            h = 4 * g + e
            seg = jnp.broadcast_to(acs[:, h:h + 1], (q, q)) - jnp.broadcast_to(acs_t[h:h + 1, :], (q, q))
            lm = jnp.exp(jnp.where(tril, seg, NEG))
            r = jnp.dot((cbm * lm).astype(BF16), xdt, preferred_element_type=F32)
            yd = r if e == 0 else jnp.where(lane >= e * SSD_P, r, yd)
        stg = stt_scr[:, gs]
        yoff = jnp.dot(cgb, stg.astype(BF16), preferred_element_type=F32) * _expand4(ea, g, lane)
        y = yd + yoff + dskip_ref[:, gs] * xg
        xw = (xg * _expand4(w_end, g, lane)).astype(BF16)
        stt_scr[:, gs] = stg * _expand4(ea_last, g, lane1) + jnp.dot(
            bg.T.astype(BF16), xw, preferred_element_type=F32)
        hg = y * _silu(z_ref[:, gs])
        y_ref[:, gs] = (_rms(hg) * ng_ref[:, gs]).astype(BF16)

    @pl.when(last)
    def _():
        for k in range(D_INNER // LANES):
            st_ref[k * LANES:(k + 1) * LANES, :] = stt_scr[:, k * LANES:(k + 1) * LANES].T


def _prompt_ssd(proj, conv_w, conv_b, dt_bias, a_log, d_skip, norm_g, layer):
    b, l, _ = proj.shape
    q = BAND
    vec = lambda w: pl.BlockSpec((None, 1, w), lambda bb, c: (layer, 0, 0))
    return pl.pallas_call(
        _ssd_kernel,
        grid=(b, l // q),
        in_specs=[
            pl.BlockSpec((None, q, CONV_DIM), lambda bb, c: (bb, c, OFF_XBC // CONV_DIM)),
            pl.BlockSpec((None, q, DT_W), lambda bb, c: (bb, c, OFF_DT // DT_W)),
            pl.BlockSpec((None, q, D_INNER), lambda bb, c: (bb, c, OFF_Z // D_INNER)),
            pl.BlockSpec((None, D_CONV, CONV_DIM), lambda bb, c: (layer, 0, 0)),
            vec(CONV_DIM), vec(LANES), vec(LANES), vec(D_INNER), vec(D_INNER),
        ],
        out_specs=[
            pl.BlockSpec((None, q, D_INNER), lambda bb, c: (bb, c, 0)),
            pl.BlockSpec((None, D_INNER, D_STATE), lambda bb, c: (bb, 0, 0)),
            pl.BlockSpec((None, D_CONV - 1, CONV_DIM), lambda bb, c: (bb, 0, 0)),
        ],
        out_shape=[
            jax.ShapeDtypeStruct((b, l, D_INNER), BF16),
            jax.ShapeDtypeStruct((b, D_INNER, D_STATE), F32),
            jax.ShapeDtypeStruct((b, D_CONV - 1, CONV_DIM), F32),
        ],
        scratch_shapes=[
            pltpu.VMEM((q + SUBLANES, CONV_DIM), F32),
            pltpu.VMEM((q, CONV_DIM), F32),
            pltpu.VMEM((D_STATE, D_INNER), F32),
        ],
        compiler_params=_params(("arbitrary", "arbitrary")),
        name="prompt_ssd",
    )(proj, proj, proj, conv_w, conv_b, dt_bias, a_log, d_skip, norm_g)


def _out_proj_kernel(*refs, n_attn):
    a_refs = refs[:n_attn]
    ssd_ref, gate_ref, x_ref, g1_ref, wa_ref, ws_ref, wo_ref, o_ref = refs[n_attn:]
    if n_attn == 1:
        attn = a_refs[0][...]
    else:
        outs, lse = a_refs[0::2], [r[...] for r in a_refs[1::2]]
        m = functools.reduce(jnp.maximum, lse)
        e = [jnp.exp(v - m) for v in lse]
        attn = sum(ev * r[...] for ev, r in zip(e, outs)) / sum(e)
    pa = jnp.dot(attn.astype(BF16), wa_ref[...], preferred_element_type=F32)
    ps = jnp.dot(ssd_ref[...], ws_ref[...], preferred_element_type=F32)
    merged = _sigmoid(gate_ref[:, 0:D_MODEL]) * pa + _sigmoid(gate_ref[:, D_MODEL:2 * D_MODEL]) * ps
    o_ref[...] = x_ref[...] + g1_ref[...] * jnp.dot(
        merged.astype(BF16), wo_ref[...], preferred_element_type=F32)


def _out_proj(attn_list, ssd, proj, x, mod, w_o_attn, w_o_ssd, w_out, layer, tm):
    bx, lx, d = x.shape
    r = mod.shape[2]
    row = lambda w, cb=0: pl.BlockSpec((None, tm, w), lambda b, i: (b, i, cb))
    wgt = lambda k, n: pl.BlockSpec((None, k, n), lambda b, i: (layer, 0, 0))
    attn_w = attn_list[0].shape[-1]
    return pl.pallas_call(
        functools.partial(_out_proj_kernel, n_attn=len(attn_list)),
        grid=(bx, lx // tm),
        in_specs=[row(attn_w)] * len(attn_list) + [
            row(D_INNER), row(2 * D_MODEL, OFF_GATE // (2 * D_MODEL)), row(d),
            pl.BlockSpec((None, None, r, d), lambda b, i: (layer, b, 0, 2)),
            wgt(GROUP_W, d), wgt(D_INNER, d), wgt(d, d),
        ],
        out_specs=row(d),
        out_shape=jax.ShapeDtypeStruct((bx, lx, d), F32),
        compiler_params=_params(("arbitrary", "arbitrary")),
        name="out_proj",
    )(*attn_list, ssd, proj, x, mod, w_o_attn, w_o_ssd, w_out)


def _mlp_kernel(x_ref, mod_ref, g_ref, wu_ref, wd_ref, fg_ref, o_ref, h_scr, acc_scr, *, final):
    f = pl.program_id(2)

    @pl.when(f == 0)
    def _():
        y = _rms(x_ref[...]) * g_ref[...]
        h = y * (1.0 + mod_ref[:, D_MODEL:2 * D_MODEL]) + mod_ref[:, 0:D_MODEL]
        h_scr[...] = h.astype(BF16)
        acc_scr[...] = jnp.zeros(acc_scr.shape, F32)

    u = jnp.maximum(jnp.dot(h_scr[...], wu_ref[...], preferred_element_type=F32), 0.0)
    acc_scr[...] += jnp.dot((u * u).astype(BF16), wd_ref[...], preferred_element_type=F32)

    @pl.when(f == pl.num_programs(2) - 1)
    def _():
        x2 = x_ref[...] + mod_ref[:, 2 * D_MODEL:3 * D_MODEL] * acc_scr[...]
        if final:
            x2 = _rms(x2) * fg_ref[...]
        o_ref[...] = x2


def _mlp(x, mod, norm_g, w_up, w_down, final_g, layer, tm, final):
    bx, lx, d = x.shape
    r = mod.shape[2]
    tf = 1024
    return pl.pallas_call(
        functools.partial(_mlp_kernel, final=final),
        grid=(bx, lx // tm, D_FF // tf),
        in_specs=[
            pl.BlockSpec((None, tm, d), lambda b, i, f: (b, i, 0)),
            pl.BlockSpec((None, None, r, 3 * d), lambda b, i, f: (layer, b, 0, 1)),
            pl.BlockSpec((None, 1, d), lambda b, i, f: (layer, 0, 0)),
            pl.BlockSpec((None, d, tf), lambda b, i, f: (layer, 0, f)),
            pl.BlockSpec((None, tf, d), lambda b, i, f: (layer, f, 0)),
            pl.BlockSpec((1, d), lambda b, i, f: (0, 0)),
        ],
        out_specs=pl.BlockSpec((None, tm, d), lambda b, i, f: (b, i, 0)),
        out_shape=jax.ShapeDtypeStruct((bx, lx, d), F32),
        scratch_shapes=[pltpu.VMEM((tm, d), BF16), pltpu.VMEM((tm, d), F32)],
        compiler_params=_params(("arbitrary", "arbitrary", "arbitrary")),
        name="mlp",
    )(x, mod, norm_g, w_up, w_down, final_g)


def _dec_attn_kernel(*refs, n_alias):
    n = N_GROUPS
    proj_ref, c_refs, b_refs, bself_ref = refs[0], refs[1:1 + n], refs[1 + n:1 + 2 * n], refs[1 + 2 * n]
    o_ref, oc_refs = refs[2 + 2 * n + n_alias], refs[3 + 2 * n + n_alias:3 + 3 * n + n_alias]
    row = lax.broadcasted_iota(jnp.int32, (SUBLANES, GROUP_W), 0)
    lane = lax.broadcasted_iota(jnp.int32, (SUBLANES, GROUP_W), 1)
    hmask = jnp.logical_and(lane >= row * HEAD_DIM, lane < (row + 1) * HEAD_DIM)
    sel = jnp.logical_or(jnp.logical_and(row == 0, lane < LANES),
                         jnp.logical_and(row == 1, lane >= LANES)).astype(F32)
    last_lane = lax.broadcasted_iota(jnp.int32, (GROUP_W, LANES), 1) == LANES - 1
    parts = []
    for g in range(n):
        c_ref, oc_ref = c_refs[g], oc_refs[g]
        lb = c_ref.shape[-1]
        qv = proj_ref[:, OFF_Q + g * GROUP_W:OFF_Q + (g + 1) * GROUP_W]
        kn = proj_ref[:, OFF_K + g * GROUP_W:OFF_K + (g + 1) * GROUP_W]
        vn = proj_ref[:, OFF_V + g * GROUP_W:OFF_V + (g + 1) * GROUP_W]
        qbd = jnp.where(hmask, jnp.broadcast_to(qv, (SUBLANES, GROUP_W)), 0.0)
        kt = jnp.concatenate([c_ref[0, h] for h in range(HEADS_PER_GROUP)], axis=0)
        vt = jnp.concatenate([c_ref[1, h] for h in range(HEADS_PER_GROUP)], axis=0)
        s = jnp.dot(qbd.astype(BF16), kt.astype(BF16), preferred_element_type=F32)
        s = s * ATTN_SCALE + b_refs[g][...]
        s_self = jnp.sum(qbd * kn, axis=-1, keepdims=True) * ATTN_SCALE + bself_ref[g][:, 0:1]
        m = jnp.maximum(jnp.max(s, axis=-1, keepdims=True), s_self)
        p = jnp.exp(s - m)
        p_self = jnp.exp(s_self - m)
        l = jnp.sum(p, axis=-1, keepdims=True) + p_self
        acc = lax.dot_general(p.astype(BF16), vt.astype(BF16), NT_DIMS,
                              preferred_element_type=F32) + p_self * vn
        parts.append((m, l, acc))
        rows2 = jnp.where(row == 0, jnp.broadcast_to(kn, (SUBLANES, GROUP_W)),
                          jnp.where(row == 1, jnp.broadcast_to(vn, (SUBLANES, GROUP_W)), 0.0))
        cols = lax.dot_general(rows2, sel, TN_DIMS, precision=lax.Precision.HIGHEST,
                               preferred_element_type=F32)
        for kv, t in ((0, kt), (1, vt)):
            rolled = pltpu.roll(t, lb - 1, axis=1)
            tail = jnp.where(last_lane, cols[:, kv * LANES:(kv + 1) * LANES], rolled[:, lb - LANES:lb])
            new = tail if lb == LANES else jnp.concatenate([rolled[:, 0:lb - LANES], tail], axis=1)
            for h in range(HEADS_PER_GROUP):
                oc_ref[kv, h] = new[h * HEAD_DIM:(h + 1) * HEAD_DIM, :]
    m_all = functools.reduce(jnp.maximum, [p[0] for p in parts])
    num = sum(jnp.exp(m - m_all) * acc for m, _, acc in parts)
    den = sum(jnp.exp(m - m_all) * l for m, l, _ in parts)
    o_ref[...] = jnp.sum(jnp.where(hmask, num / den, 0.0), axis=0, keepdims=True)


def _decode_attention(proj_s, caches_t, bias_dec, bias_self, prev_out, layer):
    bs = proj_s.shape[1]
    n = N_GROUPS
    cspec = lambda c: pl.BlockSpec((None, None) + c.shape[2:], lambda b: (layer, b, 0, 0, 0, 0))
    in_specs = [pl.BlockSpec((None, 1, PROJ_W), lambda b: (b, 0, 0))]
    in_specs += [cspec(c) for c in caches_t]
    in_specs += [pl.BlockSpec(bd.shape, lambda b: (0, 0)) for bd in bias_dec]
    in_specs += [pl.BlockSpec(bias_self.shape, lambda b: (0, 0, 0))]
    args = [proj_s.reshape(bs, 1, PROJ_W), *caches_t, *bias_dec, bias_self]
    aliases = {}
    if prev_out is not None:
        in_specs += [pl.BlockSpec(memory_space=pl.ANY)] * n
        aliases = {len(args) + j: 1 + j for j in range(n)}
        args += list(prev_out)
    out = pl.pallas_call(
        functools.partial(_dec_attn_kernel, n_alias=len(aliases)),
        grid=(bs,),
        in_specs=in_specs,
        out_specs=[pl.BlockSpec((None, 1, GROUP_W), lambda b: (b, 0, 0))] + [cspec(c) for c in caches_t],
        out_shape=[jax.ShapeDtypeStruct((bs, 1, GROUP_W), F32)] + [
            jax.ShapeDtypeStruct(c.shape, c.dtype) for c in caches_t],
        input_output_aliases=aliases,
        compiler_params=_params(("arbitrary",)),
        name="decode_attn",
    )(*args)
    return out[0].reshape(1, bs, GROUP_W), out[1:]


def _dec_ssd_kernel(*refs, n_alias):
    proj_ref, cs_ref, st_ref, cw_ref, cb_ref, dtb_ref, alog_ref, dskip_ref, ng_ref = refs[:9]
    y_ref, nst_ref, ncs_ref, xc_scr, xdt_scr, dae_scr, yacc_scr = refs[9 + n_alias:]
    b = pl.program_id(0)
    hi = lax.Precision.HIGHEST

    @pl.when(b == 0)
    def _():
        xr = proj_ref[:, OFF_XBC:OFF_XBC + CONV_DIM]
        acc = cb_ref[...]
        for t in range(D_CONV - 1):
            acc = acc + cs_ref[t] * cw_ref[t:t + 1, :]
        acc = acc + xr * cw_ref[D_CONV - 1:D_CONV, :]
        xc = _silu(acc)
        xc_scr[...] = xc
        for t in range(D_CONV - 2):
            ncs_ref[t] = cs_ref[t + 1]
        ncs_ref[D_CONV - 2] = xr
        dt = _softplus(proj_ref[:, OFF_DT:OFF_DT + LANES] + dtb_ref[...])
        da = jnp.exp(dt * (-jnp.exp(alog_ref[...])))
        onehot = (lax.shift_right_logical(lax.broadcasted_iota(jnp.int32, (LANES, D_INNER), 1), 6)
                  == lax.broadcasted_iota(jnp.int32, (LANES, D_INNER), 0)).astype(F32)
        xdt_scr[...] = xc[:, 0:D_INNER] * jnp.dot(dt, onehot, precision=hi, preferred_element_type=F32)
        dae_scr[...] = jnp.dot(da, onehot, precision=hi, preferred_element_type=F32)

    r8 = lax.broadcasted_iota(jnp.int32, (SUBLANES, D_INNER), 0)
    rows2 = jnp.where(r8 == 0, jnp.broadcast_to(xdt_scr[pl.ds(b, 1), :], (SUBLANES, D_INNER)),
                      jnp.where(r8 == 1, jnp.broadcast_to(dae_scr[pl.ds(b, 1), :], (SUBLANES, D_INNER)), 0.0))
    s8 = lax.broadcasted_iota(jnp.int32, (SUBLANES, 2 * D_STATE), 0)
    l8 = lax.broadcasted_iota(jnp.int32, (SUBLANES, 2 * D_STATE), 1)
    sel = jnp.logical_or(jnp.logical_and(s8 == 0, l8 < D_STATE),
                         jnp.logical_and(s8 == 1, l8 >= D_STATE)).astype(F32)
    cols = lax.dot_general(rows2, sel, TN_DIMS, precision=hi, preferred_element_type=F32)

    xc_row = xc_scr[pl.ds(b, 1), :]
    c8 = lax.broadcasted_iota(jnp.int32, (SUBLANES, D_STATE), 0)
    for g in range(SSD_GROUPS):
        rs = slice(g * SSD_GROUP_W, (g + 1) * SSD_GROUP_W)
        bg = xc_row[:, D_INNER + g * D_STATE:D_INNER + (g + 1) * D_STATE]
        cg = xc_row[:, D_INNER + (SSD_GROUPS + g) * D_STATE:D_INNER + (SSD_GROUPS + g + 1) * D_STATE]
        hn = st_ref[rs, :] * cols[rs, D_STATE:2 * D_STATE] + cols[rs, 0:D_STATE] * bg
        nst_ref[rs, :] = hn
        cpad = jnp.where(c8 == 0, jnp.broadcast_to(cg, (SUBLANES, D_STATE)), 0.0)
        yg = lax.dot_general(cpad, hn, NT_DIMS, precision=hi, preferred_element_type=F32)
        yacc_scr[pl.ds(b, 1), rs] = yg[0:1, :]

    @pl.when(b == pl.num_programs(0) - 1)
    def _():
        for g in range(SSD_GROUPS):
            gs = slice(g * SSD_GROUP_W, (g + 1) * SSD_GROUP_W)
            y = yacc_scr[:, gs] + dskip_ref[:, gs] * xc_scr[:, gs]
            hg = y * _silu(proj_ref[:, OFF_Z + g * SSD_GROUP_W:OFF_Z + (g + 1) * SSD_GROUP_W])
            y_ref[:, gs] = (_rms(hg) * ng_ref[:, gs]).astype(BF16)


def _decode_ssd(proj_s, conv_state_t, state, conv_w, conv_b, dt_bias, a_log, d_skip, norm_g, prev_state, layer):
    bs = proj_s.shape[1]
    vec = lambda w: pl.BlockSpec((None, 1, w), lambda b: (layer, 0, 0))
    st_spec = pl.BlockSpec((None, None, D_INNER, D_STATE), lambda b: (layer, b, 0, 0))
    args = [proj_s, conv_state_t, state, conv_w, conv_b, dt_bias, a_log, d_skip, norm_g]
    extra, aliases = [], {}
    if prev_state is not None:
        extra, aliases = [pl.BlockSpec(memory_space=pl.ANY)], {len(args): 1}
        args.append(prev_state)
    return pl.pallas_call(
        functools.partial(_dec_ssd_kernel, n_alias=len(aliases)),
        grid=(bs,),
        in_specs=[
            pl.BlockSpec((None, bs, PROJ_W), lambda b: (0, 0, 0)),
            pl.BlockSpec((None, D_CONV - 1, bs, CONV_DIM), lambda b: (layer, 0, 0, 0)),
            st_spec,
            pl.BlockSpec((None, D_CONV, CONV_DIM), lambda b: (layer, 0, 0)),
            vec(CONV_DIM), vec(LANES), vec(LANES), vec(D_INNER), vec(D_INNER),
        ] + extra,
        out_specs=[
            pl.BlockSpec((None, bs, D_INNER), lambda b: (0, 0, 0)),
            st_spec,
            pl.BlockSpec((D_CONV - 1, bs, CONV_DIM), lambda b: (0, 0, 0)),
        ],
        out_shape=[
            jax.ShapeDtypeStruct((1, bs, D_INNER), BF16),
            jax.ShapeDtypeStruct(state.shape, F32),
            jax.ShapeDtypeStruct((D_CONV - 1, bs, CONV_DIM), F32),
        ],
        input_output_aliases=aliases,
        scratch_shapes=[
            pltpu.VMEM((bs, CONV_DIM), F32),
            pltpu.VMEM((bs, D_INNER), F32),
            pltpu.VMEM((bs, D_INNER), F32),
            pltpu.VMEM((bs, D_INNER), F32),
        ],
        compiler_params=_params(("arbitrary",)),
        name="decode_ssd",
    )(*args)


def _reorder_w_in(w_in):
    q0, z0, x0, d0, g0 = 0, 3 * QKV_W, 3 * QKV_W + D_INNER, 3 * QKV_W + D_INNER + CONV_DIM, \
        3 * QKV_W + D_INNER + CONV_DIM + SSD_HEADS
    pieces = [w_in[..., z0:x0], w_in[..., g0:g0 + 2 * D_MODEL], w_in[..., x0:d0], w_in[..., q0:z0],
              w_in[..., d0:g0], jnp.zeros(w_in.shape[:-1] + (DT_W - SSD_HEADS,), w_in.dtype)]
    return jnp.concatenate(pieces, axis=-1).astype(BF16)


def _pad_lanes(v, width):
    return jnp.pad(v, ((0, 0), (0, width - v.shape[-1])))[:, None, :]


def kernel(x_prompt, x_sample, cache_kv_g0, cache_kv_g1, cache_kv_g2, state_ssm, state_conv, c_prompt,
           c_sample, rel_bias, w_ada, b_ada, norm1_g, norm2_g, w_in, conv_w, conv_b, dt_bias, a_log, d_skip,
           ssd_norm_g, w_o_attn, w_o_ssd, w_out, w_up, w_down, final_g):
    depth = w_in.shape[0]
    b, l, d = x_prompt.shape
    bs = x_sample.shape[0]
    caches = (cache_kv_g0, cache_kv_g1, cache_kv_g2)
    assert d == D_MODEL and x_sample.shape[1] == 1 and l % (DIL_GROUPS[-1][1] * BAND) == 0
    assert all(win == BAND * dil for win, dil in DIL_GROUPS)
    assert all(c.shape[2] == win for c, (win, _) in zip(caches, DIL_GROUPS))
    assert bs % SUBLANES == 0

    w_in_r = _reorder_w_in(w_in)
    w_oa, w_os, w_o, w_u, w_d = (w.astype(BF16) for w in (w_o_attn, w_o_ssd, w_out, w_up, w_down))
    n1, n2 = norm1_g[:, None, :], norm2_g[:, None, :]
    conv_b3, ng3 = conv_b[:, None, :], ssd_norm_g[:, None, :]
    dtb3, alog3 = _pad_lanes(dt_bias, LANES), _pad_lanes(a_log, LANES)
    dskip3 = jnp.repeat(d_skip, SSD_P, axis=-1)[:, None, :]
    fg = final_g[None, :]

    rows = b + bs
    rows_pad = -(-rows // SUBLANES) * SUBLANES
    c_all = jnp.pad(jnp.concatenate([c_prompt, c_sample], axis=0), ((0, rows_pad - rows), (0, 0)))
    mod = _ada_mod(c_all, w_ada, b_ada)
    mod_p = mod[:, :b, None, :]
    mod_s = mod[:, None, b:rows, :]

    bias = _bias_lookup(rel_bias, _bucket_tiles())
    head_rows = ((0, SUBLANES - HEADS_PER_GROUP), (0, 0))
    bias_rows = _bias_lookup(rel_bias, _bucket_rows())[:, 0, :]
    bias_dec = [jnp.pad(bias_rows[HEADS_PER_GROUP * g:HEADS_PER_GROUP * (g + 1), :win], head_rows)
                for g, (win, _) in enumerate(DIL_GROUPS)]
    self_bias = bias[:, 0, BAND].reshape(N_GROUPS, HEADS_PER_GROUP, 1)
    bias_self = jnp.pad(jnp.broadcast_to(self_bias, (N_GROUPS, HEADS_PER_GROUP, LANES)), ((0, 0),) + head_rows)

    xp = x_prompt
    xs = x_sample.reshape(1, bs, d)
    conv_state_t = jnp.swapaxes(state_conv, 1, 2)
    state_r = state_ssm.reshape(depth, bs, D_INNER, D_STATE)
    caches_t = [jnp.transpose(c, (0, 1, 3, 4, 5, 2)) for c in caches]
    tm_p = 1024 if l % 1024 == 0 else BAND

    kv_p = [[] for _ in DIL_GROUPS]
    ssm_p, conv_p, conv_s = [], [], []
    kv_s, st_s = None, None
    for layer in range(depth):
        final = layer == depth - 1
        proj = _in_proj(xp, mod_p, n1, w_in_r, layer, tm_p)
        attn = [a for g in range(N_GROUPS) for a in _prompt_attention(proj, bias, g)]
        ssd, st, cs = _prompt_ssd(proj, conv_w, conv_b3, dtb3, alog3, dskip3, ng3, layer)
        x1 = _out_proj(attn, ssd, proj, xp, mod_p, w_oa, w_os, w_o, layer, min(256, l))
        xp = _mlp(x1, mod_p, n2, w_u, w_d, fg, layer, tm_p, final)
        for g, (win, _) in enumerate(DIL_GROUPS):
            keep = min(win, l)
            kk = proj[:, l - keep:, OFF_K + g * GROUP_W:OFF_K + (g + 1) * GROUP_W]
            vv = proj[:, l - keep:, OFF_V + g * GROUP_W:OFF_V + (g + 1) * GROUP_W]
            kv_p[g].append(jnp.stack([kk, vv], axis=2).reshape(b, keep, 2, HEADS_PER_GROUP, HEAD_DIM))
        ssm_p.append(st.reshape(b, SSD_HEADS, SSD_P, D_STATE))
        conv_p.append(cs)
        proj_s = _in_proj(xs, mod_s, n1, w_in_r, layer, bs)
        attn_s, kv_s = _decode_attention(proj_s, caches_t, bias_dec, bias_self, kv_s, layer)
        ssd_s, st_s, cs_s = _decode_ssd(proj_s, conv_state_t, state_r, conv_w, conv_b3, dtb3, alog3,
                                        dskip3, ng3, st_s, layer)
        x1s = _out_proj([attn_s], ssd_s, proj_s, xs, mod_s, w_oa, w_os, w_o, layer, bs)
        xs = _mlp(x1s, mod_s, n2, w_u, w_d, fg, layer, bs, final)
        conv_s.append(jnp.swapaxes(cs_s, 0, 1))

    kv_s = [jnp.transpose(o, (0, 1, 5, 2, 3, 4)) for o in kv_s]
    return (xp, xs.reshape(bs, 1, d),
            jnp.stack(kv_p[0], axis=0), jnp.stack(kv_p[1], axis=0), jnp.stack(kv_p[2], axis=0),
            jnp.stack(ssm_p, axis=0), jnp.stack(conv_p, axis=0),
            kv_s[0], kv_s[1], kv_s[2],
            st_s.reshape(depth, bs, SSD_HEADS, SSD_P, D_STATE), jnp.stack(conv_s, axis=0))
```

```python
import functools
import math

import jax
import jax.numpy as jnp
import numpy as np
from jax import lax
from jax.experimental import pallas as pl
from jax.experimental.pallas import tpu as pltpu

F32 = jnp.float32
BF16 = jnp.bfloat16

D_MODEL = 1024
HEAD_DIM = 64
HEADS_PER_GROUP = 4
DIL_GROUPS = ((128, 1), (512, 4), (2048, 16))
N_GROUPS = len(DIL_GROUPS)
GROUP_W = HEADS_PER_GROUP * HEAD_DIM
QKV_W = N_GROUPS * GROUP_W
BAND = 128
NUM_BUCKETS = 32
MAX_DISTANCE = 2048
D_INNER = 2 * D_MODEL
SSD_HEADS = 32
SSD_P = 64
SSD_GROUPS = 8
SSD_GROUP_W = D_INNER // SSD_GROUPS
D_STATE = 128
D_CONV = 4
CONV_DIM = D_INNER + 2 * SSD_GROUPS * D_STATE
D_FF = 4 * D_MODEL
EPS = 1e-6
ATTN_SCALE = HEAD_DIM ** -0.5
NEG = -1e30

SUBLANES = 8
LANES = 128
V7X_VMEM_LIMIT = 52 * 1024 * 1024

OFF_Z = 0
OFF_GATE = D_INNER
OFF_XBC = 2 * D_INNER
OFF_Q = OFF_XBC + CONV_DIM
OFF_K = OFF_Q + QKV_W
OFF_V = OFF_K + QKV_W
OFF_DT = OFF_V + QKV_W
DT_W = 2 * LANES
PROJ_W = OFF_DT + DT_W

NT_DIMS = (((1,), (1,)), ((), ()))
TN_DIMS = (((0,), (0,)), ((), ()))


def _params(semantics, vmem=V7X_VMEM_LIMIT):
    return pltpu.CompilerParams(dimension_semantics=semantics, vmem_limit_bytes=vmem)


def _sigmoid(x):
    return 0.5 * (jnp.tanh(0.5 * x) + 1.0)


def _silu(x):
    return x * _sigmoid(x)


def _softplus(x):
    return jnp.maximum(x, 0.0) + jnp.log(1.0 + jnp.exp(-jnp.abs(x)))


def _rms(x):
    return x * lax.rsqrt(jnp.mean(x * x, axis=-1, keepdims=True) + EPS)


def _ada_kernel(c_ref, w_ref, b_ref, o_ref):
    s = _silu(c_ref[...]).astype(BF16)
    o_ref[...] = jnp.dot(s, w_ref[...].astype(BF16), preferred_element_type=F32) + b_ref[...]


def _ada_mod(c_all, w_ada, b_ada):
    depth, d, n = w_ada.shape
    rows = c_all.shape[0]
    tn = 1536
    return pl.pallas_call(
        _ada_kernel,
        grid=(depth, n // tn),
        in_specs=[
            pl.BlockSpec((rows, d), lambda l, j: (0, 0)),
            pl.BlockSpec((None, d, tn), lambda l, j: (l, 0, j)),
            pl.BlockSpec((None, 1, tn), lambda l, j: (l, 0, j)),
        ],
        out_specs=pl.BlockSpec((None, rows, tn), lambda l, j: (l, 0, j)),
        out_shape=jax.ShapeDtypeStruct((depth, rows, n), F32),
        compiler_params=_params(("arbitrary", "arbitrary")),
        name="ada_mod",
    )(c_all, w_ada, b_ada.reshape(depth, 1, n))


def _in_proj_kernel(x_ref, mod_ref, g_ref, w_ref, o_ref, h_scr):
    @pl.when(pl.program_id(2) == 0)
    def _():
        y = _rms(x_ref[...]) * g_ref[...]
        h = y * (1.0 + mod_ref[:, D_MODEL:2 * D_MODEL]) + mod_ref[:, 0:D_MODEL]
        h_scr[...] = h.astype(BF16)

    o_ref[...] = jnp.dot(h_scr[...], w_ref[...], preferred_element_type=F32)


def _in_proj(x, mod, norm_g, w_in_r, layer, tm):
    bx, lx, d = x.shape
    r = mod.shape[2]
    tn = 1536
    return pl.pallas_call(
        _in_proj_kernel,
        grid=(bx, lx // tm, PROJ_W // tn),
        in_specs=[
            pl.BlockSpec((None, tm, d), lambda b, i, n: (b, i, 0)),
            pl.BlockSpec((None, None, r, 2 * d), lambda b, i, n: (layer, b, 0, 0)),
            pl.BlockSpec((None, 1, d), lambda b, i, n: (layer, 0, 0)),
            pl.BlockSpec((None, d, tn), lambda b, i, n: (layer, 0, n)),
        ],
        out_specs=pl.BlockSpec((None, tm, tn), lambda b, i, n: (b, i, n)),
        out_shape=jax.ShapeDtypeStruct((bx, lx, PROJ_W), F32),
        scratch_shapes=[pltpu.VMEM((tm, d), BF16)],
        compiler_params=_params(("arbitrary", "arbitrary", "arbitrary")),
        name="in_proj",
    )(x, mod, norm_g, w_in_r)


def _t5_bucket_np(dist):
    max_exact = NUM_BUCKETS // 2
    df = np.maximum(dist, 1).astype(np.float32)
    ratio = np.log(df / np.float32(max_exact)) / np.float32(math.log(MAX_DISTANCE / max_exact))
    large = max_exact + (ratio * np.float32(NUM_BUCKETS - max_exact)).astype(np.int32)
    large = np.minimum(large, NUM_BUCKETS - 1)
    return np.where(dist < max_exact, dist, large).astype(np.int32)


def _bucket_tiles():
    q = np.arange(BAND)[:, None]
    c = np.arange(2 * BAND)[None, :]
    j = q + BAND - c
    valid = (j >= 0) & (j <= BAND)
    tiles = []
    for _, dil in DIL_GROUPS:
        b = _t5_bucket_np(np.clip(j, 0, BAND) * dil)
        tiles.append(np.where(valid, b, -1))
    return np.stack(tiles).astype(np.int32)


def _bucket_rows():
    width = max(win for win, _ in DIL_GROUPS)
    i = np.arange(width)
    rows = []
    for win, dil in DIL_GROUPS:
        dist = win - i
        valid = (i < win) & (dist % dil == 0)
        rows.append(np.where(valid, _t5_bucket_np(np.clip(dist, 0, win)), -1))
    return np.broadcast_to(np.stack(rows)[:, None, :], (N_GROUPS, SUBLANES, width)).astype(np.int32)


def _bias_kernel(rb_ref, bk_ref, o_ref):
    hh = pl.program_id(0)
    bk = bk_ref[...]
    acc = jnp.full(bk.shape, NEG, F32)
    for b in range(NUM_BUCKETS):
        acc = jnp.where(bk == b, rb_ref[b, hh], acc)
    o_ref[...] = acc


def _bias_lookup(rel_bias, buckets):
    n_heads = rel_bias.shape[1]
    blk = (None,) + buckets.shape[1:]
    return pl.pallas_call(
        _bias_kernel,
        grid=(n_heads,),
        in_specs=[
            pl.BlockSpec(memory_space=pltpu.SMEM),
            pl.BlockSpec(blk, lambda h: (h // HEADS_PER_GROUP, 0, 0)),
        ],
        out_specs=pl.BlockSpec(blk, lambda h: (h, 0, 0)),
        out_shape=jax.ShapeDtypeStruct((n_heads,) + buckets.shape[1:], F32),
        compiler_params=_params(("arbitrary",)),
        name="bias_lookup",
    )(rel_bias, jnp.asarray(buckets))


HEADS_PER_TILE = LANES // HEAD_DIM


ATTN_TILES_IN_FLIGHT = 4


def _attn_kernel(q_ref, kp_ref, kc_ref, vp_ref, vc_ref, bias_ref, o_ref, lse_ref, *, dil, periods):
    period = BAND * dil
    first = pl.program_id(1) == 0
    col = lax.broadcasted_iota(jnp.int32, (BAND, 2 * BAND), 1)
    pen_first = jnp.where(col < BAND, jnp.where(first, NEG, 0.0), 0.0)
    lane = lax.broadcasted_iota(jnp.int32, (BAND, LANES), 1)

    def rows_of(r, tau):
        return pl.ds(r + tau * period, BAND, stride=dil) if dil > 1 else pl.ds(tau * period, BAND)

    def tile(r, tau):
        rows = rows_of(r, tau)
        q = q_ref[rows, :]
        if tau == 0:
            kprev, vprev = kp_ref[rows_of(r, 0), :], vp_ref[rows_of(r, 0), :]
        else:
            kprev, vprev = kc_ref[rows_of(r, tau - 1), :], vc_ref[rows_of(r, tau - 1), :]
        kb = jnp.concatenate([kprev, kc_ref[rows, :]], axis=0).astype(BF16)
        vb = jnp.concatenate([vprev, vc_ref[rows, :]], axis=0).astype(BF16)
        o, lse = None, None
        for h in range(HEADS_PER_TILE):
            mine = jnp.logical_and(lane >= h * HEAD_DIM, lane < (h + 1) * HEAD_DIM)
            qh = jnp.where(mine, q, 0.0).astype(BF16)
            s = lax.dot_general(qh, kb, NT_DIMS, preferred_element_type=F32)
            s = s * ATTN_SCALE + bias_ref[h]
            if tau == 0:
                s = s + pen_first
            m = jnp.max(s, axis=-1, keepdims=True)
            p = jnp.exp(s - m)
            l = jnp.sum(p, axis=-1, keepdims=True)
            oh = jnp.dot(p.astype(BF16), vb, preferred_element_type=F32) / l
            lh = jnp.broadcast_to(m + jnp.log(l), (BAND, LANES))
            o = oh if h == 0 else jnp.where(mine, oh, o)
            lse = lh if h == 0 else jnp.where(mine, lh, lse)
        o_ref[rows, :] = o
        lse_ref[rows, :] = lse

    def residue(r, carry):
        for tau in range(periods):
            tile(r, tau)
        return carry

    if dil > 1:
        lax.fori_loop(0, dil, residue, 0, unroll=max(1, min(dil, ATTN_TILES_IN_FLIGHT // periods)))
    else:
        residue(0, 0)


def _prompt_attention(proj, bias, g):
    b, l, _ = proj.shape
    dil = DIL_GROUPS[g][1]
    period = BAND * dil
    periods = max(1, ATTN_TILES_IN_FLIGHT // dil)
    rows = period * periods
    tiles = GROUP_W // LANES
    qb, kb, vb = (off // LANES + g * tiles for off in (OFF_Q, OFF_K, OFF_V))
    blk = (None, rows, LANES)
    cur = lambda cb: pl.BlockSpec(blk, lambda bb, i, t: (bb, i, cb + t))
    prev = lambda cb: pl.BlockSpec((None, period, LANES),
                                   lambda bb, i, t: (bb, jnp.maximum(i * periods - 1, 0), cb + t))
    out_spec = pl.BlockSpec(blk, lambda bb, i, t: (bb, i, t))
    return pl.pallas_call(
        functools.partial(_attn_kernel, dil=dil, periods=periods),
        grid=(b, l // rows, tiles),
        in_specs=[cur(qb), prev(kb), cur(kb), prev(vb), cur(vb),
                  pl.BlockSpec((HEADS_PER_TILE, BAND, 2 * BAND), lambda bb, i, t: (g * tiles + t, 0, 0))],
        out_specs=[out_spec, out_spec],
        out_shape=[jax.ShapeDtypeStruct((b, l, GROUP_W), F32)] * 2,
        compiler_params=_params(("arbitrary", "arbitrary", "arbitrary")),
        name=f"prompt_attn_g{g}",
    )(proj, proj, proj, proj, proj, bias)


def _expand4(arr, g, lane):
    rows = arr.shape[0]
    c = [jnp.broadcast_to(arr[:, 4 * g + e:4 * g + e + 1], (rows, SSD_GROUP_W)) for e in range(4)]
    return jnp.where(lane < SSD_P, c[0], jnp.where(lane < 2 * SSD_P, c[1],
                                                    jnp.where(lane < 3 * SSD_P, c[2], c[3])))


def _cumsum_rows(a):
    row = lax.broadcasted_iota(jnp.int32, a.shape, 0)
    s = 1
    while s < a.shape[0]:
        a = a + jnp.where(row >= s, pltpu.roll(a, s, axis=0), 0.0)
        s *= 2
    return a


def _ssd_kernel(xbc_ref, dt_ref, z_ref, cw_ref, cb_ref, dtb_ref, alog_ref, dskip_ref, ng_ref,
                y_ref, st_ref, cs_ref, xp_scr, xc_scr, stt_scr):
    q = BAND
    c = pl.program_id(1)
    last = c == pl.num_programs(1) - 1

    @pl.when(c == 0)
    def _():
        xp_scr[0:SUBLANES, :] = jnp.zeros((SUBLANES, CONV_DIM), F32)
        stt_scr[...] = jnp.zeros(stt_scr.shape, F32)

    xp_scr[SUBLANES:SUBLANES + q, :] = xbc_ref[...]
    cblk = 512
    for j in range(CONV_DIM // cblk):
        cs = slice(j * cblk, (j + 1) * cblk)
        acc = cb_ref[:, cs]
        for t in range(D_CONV):
            lo = SUBLANES - (D_CONV - 1) + t
            acc = acc + xp_scr[lo:lo + q, cs] * cw_ref[t:t + 1, cs]
        xc_scr[:, cs] = _silu(acc)

    @pl.when(last)
    def _():
        cs_ref[...] = xp_scr[SUBLANES + q - (D_CONV - 1):SUBLANES + q, :]

    xp_scr[0:SUBLANES, :] = xp_scr[q:q + SUBLANES, :]

    dt = _softplus(dt_ref[:, 0:LANES] + dtb_ref[...])
    a = dt * (-jnp.exp(alog_ref[...]))
    acs = _cumsum_rows(a)
    acs_t = acs.T
    dt_t = dt.T
    ea_last = jnp.exp(acs[q - 1:q, :])
    w_end_t = dt_t * jnp.exp(jnp.broadcast_to(acs_t[:, q - 1:q], (q, q)) - acs_t)

    row = lax.broadcasted_iota(jnp.int32, (q, q), 0)
    colq = lax.broadcasted_iota(jnp.int32, (q, q), 1)
    tril = row >= colq
    lane = lax.broadcasted_iota(jnp.int32, (q, SSD_GROUP_W), 1)
    lane1 = lax.broadcasted_iota(jnp.int32, (1, SSD_GROUP_W), 1)

    for g in range(SSD_GROUPS):
        gs = slice(g * SSD_GROUP_W, (g + 1) * SSD_GROUP_W)
        bg = xc_scr[:, D_INNER + g * D_STATE:D_INNER + (g + 1) * D_STATE]
        cg = xc_scr[:, D_INNER + (SSD_GROUPS + g) * D_STATE:D_INNER + (SSD_GROUPS + g + 1) * D_STATE]
        cbm = lax.dot_general(cg.astype(BF16), bg.astype(BF16), NT_DIMS, preferred_element_type=F32)
        bg_t = bg.T
        xg = xc_scr[:, gs]
        xgb = xg.astype(BF16)
        stg = stt_scr[:, gs]
        rhs = jnp.concatenate([xgb, stg.astype(BF16)], axis=0)
        y, snew = None, None
        for e in range(4):
            h = 4 * g + e
            col_h = jnp.broadcast_to(acs[:, h:h + 1], (q, q))
            row_h = lambda v: jnp.broadcast_to(v[h:h + 1, :], (q, q))
            intra = cbm * jnp.exp(jnp.where(tril, col_h - row_h(acs_t), NEG)) * row_h(dt_t)
            inter = cg * jnp.exp(col_h)
            r = jnp.dot(jnp.concatenate([intra, inter], axis=1).astype(BF16), rhs,
                        preferred_element_type=F32)
            sr = jnp.dot((bg_t * row_h(w_end_t)).astype(BF16), xgb, preferred_element_type=F32)
            mine = lane >= e * SSD_P
            y = r if e == 0 else jnp.where(mine, r, y)
            snew = sr if e == 0 else jnp.where(mine, sr, snew)
        y = y + dskip_ref[:, gs] * xg
        stt_scr[:, gs] = stg * _expand4(ea_last, g, lane1) + snew
        hg = y * _silu(z_ref[:, gs])
        y_ref[:, gs] = (_rms(hg) * ng_ref[:, gs]).astype(BF16)

    @pl.when(last)
    def _():
        for k in range(D_INNER // LANES):
            st_ref[k * LANES:(k + 1) * LANES, :] = stt_scr[:, k * LANES:(k + 1) * LANES].T


def _prompt_ssd(proj, conv_w, conv_b, dt_bias, a_log, d_skip, norm_g, layer):
    b, l, _ = proj.shape
    q = BAND
    vec = lambda w: pl.BlockSpec((None, 1, w), lambda bb, c: (layer, 0, 0))
    return pl.pallas_call(
        _ssd_kernel,
        grid=(b, l // q),
        in_specs=[
            pl.BlockSpec((None, q, CONV_DIM), lambda bb, c: (bb, c, OFF_XBC // CONV_DIM)),
            pl.BlockSpec((None, q, DT_W), lambda bb, c: (bb, c, OFF_DT // DT_W)),
            pl.BlockSpec((None, q, D_INNER), lambda bb, c: (bb, c, OFF_Z // D_INNER)),
            pl.BlockSpec((None, D_CONV, CONV_DIM), lambda bb, c: (layer, 0, 0)),
            vec(CONV_DIM), vec(LANES), vec(LANES), vec(D_INNER), vec(D_INNER),
        ],
        out_specs=[
            pl.BlockSpec((None, q, D_INNER), lambda bb, c: (bb, c, 0)),
            pl.BlockSpec((None, D_INNER, D_STATE), lambda bb, c: (bb, 0, 0)),
            pl.BlockSpec((None, D_CONV - 1, CONV_DIM), lambda bb, c: (bb, 0, 0)),
        ],
        out_shape=[
            jax.ShapeDtypeStruct((b, l, D_INNER), BF16),
            jax.ShapeDtypeStruct((b, D_INNER, D_STATE), F32),
            jax.ShapeDtypeStruct((b, D_CONV - 1, CONV_DIM), F32),
        ],
        scratch_shapes=[
            pltpu.VMEM((q + SUBLANES, CONV_DIM), F32),
            pltpu.VMEM((q, CONV_DIM), F32),
            pltpu.VMEM((D_STATE, D_INNER), F32),
        ],
        compiler_params=_params(("arbitrary", "arbitrary")),
        name="prompt_ssd",
    )(proj, proj, proj, conv_w, conv_b, dt_bias, a_log, d_skip, norm_g)


def _out_proj_kernel(*refs, n_attn):
    a_refs = refs[:n_attn]
    ssd_ref, gate_ref, x_ref, g1_ref, wa_ref, ws_ref, wo_ref, o_ref = refs[n_attn:]
    if n_attn == 1:
        attn = a_refs[0][...]
    else:
        outs, lse = a_refs[0::2], [r[...] for r in a_refs[1::2]]
        m = functools.reduce(jnp.maximum, lse)
        e = [jnp.exp(v - m) for v in lse]
        attn = sum(ev * r[...] for ev, r in zip(e, outs)) / sum(e)
    pa = jnp.dot(attn.astype(BF16), wa_ref[...], preferred_element_type=F32)
    ps = jnp.dot(ssd_ref[...], ws_ref[...], preferred_element_type=F32)
    merged = _sigmoid(gate_ref[:, 0:D_MODEL]) * pa + _sigmoid(gate_ref[:, D_MODEL:2 * D_MODEL]) * ps
    o_ref[...] = x_ref[...] + g1_ref[...] * jnp.dot(
        merged.astype(BF16), wo_ref[...], preferred_element_type=F32)


def _out_proj(attn_list, ssd, proj, x, mod, w_o_attn, w_o_ssd, w_out, layer, tm):
    bx, lx, d = x.shape
    r = mod.shape[2]
    row = lambda w, cb=0: pl.BlockSpec((None, tm, w), lambda b, i: (b, i, cb))
    wgt = lambda k, n: pl.BlockSpec((None, k, n), lambda b, i: (layer, 0, 0))
    attn_w = attn_list[0].shape[-1]
    return pl.pallas_call(
        functools.partial(_out_proj_kernel, n_attn=len(attn_list)),
        grid=(bx, lx // tm),
        in_specs=[row(attn_w)] * len(attn_list) + [
            row(D_INNER), row(2 * D_MODEL, OFF_GATE // (2 * D_MODEL)), row(d),
            pl.BlockSpec((None, None, r, d), lambda b, i: (layer, b, 0, 2)),
            wgt(GROUP_W, d), wgt(D_INNER, d), wgt(d, d),
        ],
        out_specs=row(d),
        out_shape=jax.ShapeDtypeStruct((bx, lx, d), F32),
        compiler_params=_params(("arbitrary", "arbitrary")),
        name="out_proj",
    )(*attn_list, ssd, proj, x, mod, w_o_attn, w_o_ssd, w_out)


def _mlp_kernel(x_ref, mod_ref, g_ref, wu_ref, wd_ref, fg_ref, o_ref, h_scr, acc_scr, *, final):
    f = pl.program_id(2)

    @pl.when(f == 0)
    def _():
        y = _rms(x_ref[...]) * g_ref[...]
        h = y * (1.0 + mod_ref[:, D_MODEL:2 * D_MODEL]) + mod_ref[:, 0:D_MODEL]
        h_scr[...] = h.astype(BF16)
        acc_scr[...] = jnp.zeros(acc_scr.shape, F32)

    u = jnp.maximum(jnp.dot(h_scr[...], wu_ref[...], preferred_element_type=F32), 0.0)
    acc_scr[...] += jnp.dot((u * u).astype(BF16), wd_ref[...], preferred_element_type=F32)

    @pl.when(f == pl.num_programs(2) - 1)
    def _():
        x2 = x_ref[...] + mod_ref[:, 2 * D_MODEL:3 * D_MODEL] * acc_scr[...]
        if final:
            x2 = _rms(x2) * fg_ref[...]
        o_ref[...] = x2


def _mlp(x, mod, norm_g, w_up, w_down, final_g, layer, tm, final):
    bx, lx, d = x.shape
    r = mod.shape[2]
    tf = 1024
    return pl.pallas_call(
        functools.partial(_mlp_kernel, final=final),
        grid=(bx, lx // tm, D_FF // tf),
        in_specs=[
            pl.BlockSpec((None, tm, d), lambda b, i, f: (b, i, 0)),
            pl.BlockSpec((None, None, r, 3 * d), lambda b, i, f: (layer, b, 0, 1)),
            pl.BlockSpec((None, 1, d), lambda b, i, f: (layer, 0, 0)),
            pl.BlockSpec((None, d, tf), lambda b, i, f: (layer, 0, f)),
            pl.BlockSpec((None, tf, d), lambda b, i, f: (layer, f, 0)),
            pl.BlockSpec((1, d), lambda b, i, f: (0, 0)),
        ],
        out_specs=pl.BlockSpec((None, tm, d), lambda b, i, f: (b, i, 0)),
        out_shape=jax.ShapeDtypeStruct((bx, lx, d), F32),
        scratch_shapes=[pltpu.VMEM((tm, d), BF16), pltpu.VMEM((tm, d), F32)],
        compiler_params=_params(("arbitrary", "arbitrary", "arbitrary")),
        name="mlp",
    )(x, mod, norm_g, w_up, w_down, final_g)


def _dec_attn_kernel(*refs, n_alias):
    n = N_GROUPS
    proj_ref, c_refs, b_refs, bself_ref = refs[0], refs[1:1 + n], refs[1 + n:1 + 2 * n], refs[1 + 2 * n]
    o_ref, oc_refs = refs[2 + 2 * n + n_alias], refs[3 + 2 * n + n_alias:3 + 3 * n + n_alias]
    row = lax.broadcasted_iota(jnp.int32, (SUBLANES, GROUP_W), 0)
    lane = lax.broadcasted_iota(jnp.int32, (SUBLANES, GROUP_W), 1)
    hmask = jnp.logical_and(lane >= row * HEAD_DIM, lane < (row + 1) * HEAD_DIM)
    sel = jnp.logical_or(jnp.logical_and(row == 0, lane < LANES),
                         jnp.logical_and(row == 1, lane >= LANES)).astype(F32)
    last_lane = lax.broadcasted_iota(jnp.int32, (GROUP_W, LANES), 1) == LANES - 1
    parts = []
    for g in range(n):
        c_ref, oc_ref = c_refs[g], oc_refs[g]
        lb = c_ref.shape[-1]
        qv = proj_ref[:, OFF_Q + g * GROUP_W:OFF_Q + (g + 1) * GROUP_W]
        kn = proj_ref[:, OFF_K + g * GROUP_W:OFF_K + (g + 1) * GROUP_W]
        vn = proj_ref[:, OFF_V + g * GROUP_W:OFF_V + (g + 1) * GROUP_W]
        qbd = jnp.where(hmask, jnp.broadcast_to(qv, (SUBLANES, GROUP_W)), 0.0)
        kt = jnp.concatenate([c_ref[0, h] for h in range(HEADS_PER_GROUP)], axis=0)
        vt = jnp.concatenate([c_ref[1, h] for h in range(HEADS_PER_GROUP)], axis=0)
        s = jnp.dot(qbd.astype(BF16), kt.astype(BF16), preferred_element_type=F32)
        s = s * ATTN_SCALE + b_refs[g][...]
        s_self = jnp.sum(qbd * kn, axis=-1, keepdims=True) * ATTN_SCALE + bself_ref[g][:, 0:1]
        m = jnp.maximum(jnp.max(s, axis=-1, keepdims=True), s_self)
        p = jnp.exp(s - m)
        p_self = jnp.exp(s_self - m)
        l = jnp.sum(p, axis=-1, keepdims=True) + p_self
        acc = lax.dot_general(p.astype(BF16), vt.astype(BF16), NT_DIMS,
                              preferred_element_type=F32) + p_self * vn
        parts.append((m, l, acc))
        rows2 = jnp.where(row == 0, jnp.broadcast_to(kn, (SUBLANES, GROUP_W)),
                          jnp.where(row == 1, jnp.broadcast_to(vn, (SUBLANES, GROUP_W)), 0.0))
        cols = lax.dot_general(rows2, sel, TN_DIMS, precision=lax.Precision.HIGHEST,
                               preferred_element_type=F32)
        for kv, t in ((0, kt), (1, vt)):
            rolled = pltpu.roll(t, lb - 1, axis=1)
            tail = jnp.where(last_lane, cols[:, kv * LANES:(kv + 1) * LANES], rolled[:, lb - LANES:lb])
            new = tail if lb == LANES else jnp.concatenate([rolled[:, 0:lb - LANES], tail], axis=1)
            for h in range(HEADS_PER_GROUP):
                oc_ref[kv, h] = new[h * HEAD_DIM:(h + 1) * HEAD_DIM, :]
    m_all = functools.reduce(jnp.maximum, [p[0] for p in parts])
    num = sum(jnp.exp(m - m_all) * acc for m, _, acc in parts)
    den = sum(jnp.exp(m - m_all) * l for m, l, _ in parts)
    o_ref[...] = jnp.sum(jnp.where(hmask, num / den, 0.0), axis=0, keepdims=True)


def _decode_attention(proj_s, caches_t, bias_dec, bias_self, prev_out, layer):
    bs = proj_s.shape[1]
    n = N_GROUPS
    cspec = lambda c: pl.BlockSpec((None, None) + c.shape[2:], lambda b: (layer, b, 0, 0, 0, 0))
    in_specs = [pl.BlockSpec((None, 1, PROJ_W), lambda b: (b, 0, 0))]
    in_specs += [cspec(c) for c in caches_t]
    in_specs += [pl.BlockSpec(bd.shape, lambda b: (0, 0)) for bd in bias_dec]
    in_specs += [pl.BlockSpec(bias_self.shape, lambda b: (0, 0, 0))]
    args = [proj_s.reshape(bs, 1, PROJ_W), *caches_t, *bias_dec, bias_self]
    aliases = {}
    if prev_out is not None:
        in_specs += [pl.BlockSpec(memory_space=pl.ANY)] * n
        aliases = {len(args) + j: 1 + j for j in range(n)}
        args += list(prev_out)
    out = pl.pallas_call(
        functools.partial(_dec_attn_kernel, n_alias=len(aliases)),
        grid=(bs,),
        in_specs=in_specs,
        out_specs=[pl.BlockSpec((None, 1, GROUP_W), lambda b: (b, 0, 0))] + [cspec(c) for c in caches_t],
        out_shape=[jax.ShapeDtypeStruct((bs, 1, GROUP_W), F32)] + [
            jax.ShapeDtypeStruct(c.shape, c.dtype) for c in caches_t],
        input_output_aliases=aliases,
        compiler_params=_params(("arbitrary",)),
        name="decode_attn",
    )(*args)
    return out[0].reshape(1, bs, GROUP_W), out[1:]


def _dec_ssd_kernel(*refs, n_alias):
    proj_ref, cs_ref, st_ref, cw_ref, cb_ref, dtb_ref, alog_ref, dskip_ref, ng_ref = refs[:9]
    y_ref, nst_ref, ncs_ref, xc_scr, xdt_scr, dae_scr, yacc_scr = refs[9 + n_alias:]
    b = pl.program_id(0)
    hi = lax.Precision.HIGHEST

    @pl.when(b == 0)
    def _():
        xr = proj_ref[:, OFF_XBC:OFF_XBC + CONV_DIM]
        acc = cb_ref[...]
        for t in range(D_CONV - 1):
            acc = acc + cs_ref[t] * cw_ref[t:t + 1, :]
        acc = acc + xr * cw_ref[D_CONV - 1:D_CONV, :]
        xc = _silu(acc)
        xc_scr[...] = xc
        for t in range(D_CONV - 2):
            ncs_ref[t] = cs_ref[t + 1]
        ncs_ref[D_CONV - 2] = xr
        dt = _softplus(proj_ref[:, OFF_DT:OFF_DT + LANES] + dtb_ref[...])
        da = jnp.exp(dt * (-jnp.exp(alog_ref[...])))
        onehot = (lax.shift_right_logical(lax.broadcasted_iota(jnp.int32, (LANES, D_INNER), 1), 6)
                  == lax.broadcasted_iota(jnp.int32, (LANES, D_INNER), 0)).astype(F32)
        xdt_scr[...] = xc[:, 0:D_INNER] * jnp.dot(dt, onehot, precision=hi, preferred_element_type=F32)
        dae_scr[...] = jnp.dot(da, onehot, precision=hi, preferred_element_type=F32)

    r8 = lax.broadcasted_iota(jnp.int32, (SUBLANES, D_INNER), 0)
    rows2 = jnp.where(r8 == 0, jnp.broadcast_to(xdt_scr[pl.ds(b, 1), :], (SUBLANES, D_INNER)),
                      jnp.where(r8 == 1, jnp.broadcast_to(dae_scr[pl.ds(b, 1), :], (SUBLANES, D_INNER)), 0.0))
    s8 = lax.broadcasted_iota(jnp.int32, (SUBLANES, 2 * D_STATE), 0)
    l8 = lax.broadcasted_iota(jnp.int32, (SUBLANES, 2 * D_STATE), 1)
    sel = jnp.logical_or(jnp.logical_and(s8 == 0, l8 < D_STATE),
                         jnp.logical_and(s8 == 1, l8 >= D_STATE)).astype(F32)
    cols = lax.dot_general(rows2, sel, TN_DIMS, precision=hi, preferred_element_type=F32)

    xc_row = xc_scr[pl.ds(b, 1), :]
    c8 = lax.broadcasted_iota(jnp.int32, (SUBLANES, D_STATE), 0)
    for g in range(SSD_GROUPS):
        rs = slice(g * SSD_GROUP_W, (g + 1) * SSD_GROUP_W)
        bg = xc_row[:, D_INNER + g * D_STATE:D_INNER + (g + 1) * D_STATE]
        cg = xc_row[:, D_INNER + (SSD_GROUPS + g) * D_STATE:D_INNER + (SSD_GROUPS + g + 1) * D_STATE]
        hn = st_ref[rs, :] * cols[rs, D_STATE:2 * D_STATE] + cols[rs, 0:D_STATE] * bg
        nst_ref[rs, :] = hn
        cpad = jnp.where(c8 == 0, jnp.broadcast_to(cg, (SUBLANES, D_STATE)), 0.0)
        yg = lax.dot_general(cpad, hn, NT_DIMS, precision=hi, preferred_element_type=F32)
        yacc_scr[pl.ds(b, 1), rs] = yg[0:1, :]

    @pl.when(b == pl.num_programs(0) - 1)
    def _():
        for g in range(SSD_GROUPS):
            gs = slice(g * SSD_GROUP_W, (g + 1) * SSD_GROUP_W)
            y = yacc_scr[:, gs] + dskip_ref[:, gs] * xc_scr[:, gs]
            hg = y * _silu(proj_ref[:, OFF_Z + g * SSD_GROUP_W:OFF_Z + (g + 1) * SSD_GROUP_W])
            y_ref[:, gs] = (_rms(hg) * ng_ref[:, gs]).astype(BF16)


def _decode_ssd(proj_s, conv_state_t, state, conv_w, conv_b, dt_bias, a_log, d_skip, norm_g, prev_state, layer):
    bs = proj_s.shape[1]
    vec = lambda w: pl.BlockSpec((None, 1, w), lambda b: (layer, 0, 0))
    st_spec = pl.BlockSpec((None, None, D_INNER, D_STATE), lambda b: (layer, b, 0, 0))
    args = [proj_s, conv_state_t, state, conv_w, conv_b, dt_bias, a_log, d_skip, norm_g]
    extra, aliases = [], {}
    if prev_state is not None:
        extra, aliases = [pl.BlockSpec(memory_space=pl.ANY)], {len(args): 1}
        args.append(prev_state)
    return pl.pallas_call(
        functools.partial(_dec_ssd_kernel, n_alias=len(aliases)),
        grid=(bs,),
        in_specs=[
            pl.BlockSpec((None, bs, PROJ_W), lambda b: (0, 0, 0)),
            pl.BlockSpec((None, D_CONV - 1, bs, CONV_DIM), lambda b: (layer, 0, 0, 0)),
            st_spec,
            pl.BlockSpec((None, D_CONV, CONV_DIM), lambda b: (layer, 0, 0)),
            vec(CONV_DIM), vec(LANES), vec(LANES), vec(D_INNER), vec(D_INNER),
        ] + extra,
        out_specs=[
            pl.BlockSpec((None, bs, D_INNER), lambda b: (0, 0, 0)),
            st_spec,
            pl.BlockSpec((D_CONV - 1, bs, CONV_DIM), lambda b: (0, 0, 0)),
        ],
        out_shape=[
            jax.ShapeDtypeStruct((1, bs, D_INNER), BF16),
            jax.ShapeDtypeStruct(state.shape, F32),
            jax.ShapeDtypeStruct((D_CONV - 1, bs, CONV_DIM), F32),
        ],
        input_output_aliases=aliases,
        scratch_shapes=[
            pltpu.VMEM((bs, CONV_DIM), F32),
            pltpu.VMEM((bs, D_INNER), F32),
            pltpu.VMEM((bs, D_INNER), F32),
            pltpu.VMEM((bs, D_INNER), F32),
        ],
        compiler_params=_params(("arbitrary",)),
        name="decode_ssd",
    )(*args)


def _reorder_w_in(w_in):
    q0, z0, x0, d0, g0 = 0, 3 * QKV_W, 3 * QKV_W + D_INNER, 3 * QKV_W + D_INNER + CONV_DIM, \
        3 * QKV_W + D_INNER + CONV_DIM + SSD_HEADS
    pieces = [w_in[..., z0:x0], w_in[..., g0:g0 + 2 * D_MODEL], w_in[..., x0:d0], w_in[..., q0:z0],
              w_in[..., d0:g0], jnp.zeros(w_in.shape[:-1] + (DT_W - SSD_HEADS,), w_in.dtype)]
    return jnp.concatenate(pieces, axis=-1).astype(BF16)


def _pad_lanes(v, width):
    return jnp.pad(v, ((0, 0), (0, width - v.shape[-1])))[:, None, :]


def kernel(x_prompt, x_sample, cache_kv_g0, cache_kv_g1, cache_kv_g2, state_ssm, state_conv, c_prompt,
           c_sample, rel_bias, w_ada, b_ada, norm1_g, norm2_g, w_in, conv_w, conv_b, dt_bias, a_log, d_skip,
           ssd_norm_g, w_o_attn, w_o_ssd, w_out, w_up, w_down, final_g):
    depth = w_in.shape[0]
    b, l, d = x_prompt.shape
    bs = x_sample.shape[0]
    caches = (cache_kv_g0, cache_kv_g1, cache_kv_g2)
    assert d == D_MODEL and x_sample.shape[1] == 1 and l % (DIL_GROUPS[-1][1] * BAND) == 0
    assert all(win == BAND * dil for win, dil in DIL_GROUPS)
    assert all(c.shape[2] == win for c, (win, _) in zip(caches, DIL_GROUPS))
    assert bs % SUBLANES == 0

    w_in_r = _reorder_w_in(w_in)
    w_oa, w_os, w_o, w_u, w_d = (w.astype(BF16) for w in (w_o_attn, w_o_ssd, w_out, w_up, w_down))
    n1, n2 = norm1_g[:, None, :], norm2_g[:, None, :]
    conv_b3, ng3 = conv_b[:, None, :], ssd_norm_g[:, None, :]
    dtb3, alog3 = _pad_lanes(dt_bias, LANES), _pad_lanes(a_log, LANES)
    dskip3 = jnp.repeat(d_skip, SSD_P, axis=-1)[:, None, :]
    fg = final_g[None, :]

    rows = b + bs
    rows_pad = -(-rows // SUBLANES) * SUBLANES
    c_all = jnp.pad(jnp.concatenate([c_prompt, c_sample], axis=0), ((0, rows_pad - rows), (0, 0)))
    mod = _ada_mod(c_all, w_ada, b_ada)
    mod_p = mod[:, :b, None, :]
    mod_s = mod[:, None, b:rows, :]

    bias = _bias_lookup(rel_bias, _bucket_tiles())
    head_rows = ((0, SUBLANES - HEADS_PER_GROUP), (0, 0))
    bias_rows = _bias_lookup(rel_bias, _bucket_rows())[:, 0, :]
    bias_dec = [jnp.pad(bias_rows[HEADS_PER_GROUP * g:HEADS_PER_GROUP * (g + 1), :win], head_rows)
                for g, (win, _) in enumerate(DIL_GROUPS)]
    self_bias = bias[:, 0, BAND].reshape(N_GROUPS, HEADS_PER_GROUP, 1)
    bias_self = jnp.pad(jnp.broadcast_to(self_bias, (N_GROUPS, HEADS_PER_GROUP, LANES)), ((0, 0),) + head_rows)

    xp = x_prompt
    xs = x_sample.reshape(1, bs, d)
    conv_state_t = jnp.swapaxes(state_conv, 1, 2)
    state_r = state_ssm.reshape(depth, bs, D_INNER, D_STATE)
    caches_t = [jnp.transpose(c, (0, 1, 3, 4, 5, 2)) for c in caches]
    tm_p = 1024 if l % 1024 == 0 else BAND

    kv_p = [[] for _ in DIL_GROUPS]
    ssm_p, conv_p, conv_s = [], [], []
    kv_s, st_s = None, None
    for layer in range(depth):
        final = layer == depth - 1
        proj = _in_proj(xp, mod_p, n1, w_in_r, layer, tm_p)
        attn = [a for g in range(N_GROUPS) for a in _prompt_attention(proj, bias, g)]
        ssd, st, cs = _prompt_ssd(proj, conv_w, conv_b3, dtb3, alog3, dskip3, ng3, layer)
        x1 = _out_proj(attn, ssd, proj, xp, mod_p, w_oa, w_os, w_o, layer, min(256, l))
        xp = _mlp(x1, mod_p, n2, w_u, w_d, fg, layer, tm_p, final)
        for g, (win, _) in enumerate(DIL_GROUPS):
            keep = min(win, l)
            kk = proj[:, l - keep:, OFF_K + g * GROUP_W:OFF_K + (g + 1) * GROUP_W]
            vv = proj[:, l - keep:, OFF_V + g * GROUP_W:OFF_V + (g + 1) * GROUP_W]
            kv_p[g].append(jnp.stack([kk, vv], axis=2).reshape(b, keep, 2, HEADS_PER_GROUP, HEAD_DIM))
        ssm_p.append(st.reshape(b, SSD_HEADS, SSD_P, D_STATE))
        conv_p.append(cs)
        proj_s = _in_proj(xs, mod_s, n1, w_in_r, layer, bs)
        attn_s, kv_s = _decode_attention(proj_s, caches_t, bias_dec, bias_self, kv_s, layer)
        ssd_s, st_s, cs_s = _decode_ssd(proj_s, conv_state_t, state_r, conv_w, conv_b3, dtb3, alog3,
                                        dskip3, ng3, st_s, layer)
        x1s = _out_proj([attn_s], ssd_s, proj_s, xs, mod_s, w_oa, w_os, w_o, layer, bs)
        xs = _mlp(x1s, mod_s, n2, w_u, w_d, fg, layer, bs, final)
        conv_s.append(jnp.swapaxes(cs_s, 0, 1))

    kv_s = [jnp.transpose(o, (0, 1, 5, 2, 3, 4)) for o in kv_s]
    return (xp, xs.reshape(bs, 1, d),
            jnp.stack(kv_p[0], axis=0), jnp.stack(kv_p[1], axis=0), jnp.stack(kv_p[2], axis=0),
            jnp.stack(ssm_p, axis=0), jnp.stack(conv_p, axis=0),
            kv_s[0], kv_s[1], kv_s[2],
            st_s.reshape(depth, bs, SSD_HEADS, SSD_P, D_STATE), jnp.stack(conv_s, axis=0))
```

```python
import functools
import math

import jax
import jax.numpy as jnp
import numpy as np
from jax import lax
from jax.experimental import pallas as pl
from jax.experimental.pallas import tpu as pltpu

F32 = jnp.float32
BF16 = jnp.bfloat16

D_MODEL = 1024
HEAD_DIM = 64
HEADS_PER_GROUP = 4
DIL_GROUPS = ((128, 1), (512, 4), (2048, 16))
N_GROUPS = len(DIL_GROUPS)
GROUP_W = HEADS_PER_GROUP * HEAD_DIM
QKV_W = N_GROUPS * GROUP_W
BAND = 128
NUM_BUCKETS = 32
MAX_DISTANCE = 2048
D_INNER = 2 * D_MODEL
SSD_HEADS = 32
SSD_P = 64
SSD_GROUPS = 8
SSD_GROUP_W = D_INNER // SSD_GROUPS
D_STATE = 128
D_CONV = 4
CONV_DIM = D_INNER + 2 * SSD_GROUPS * D_STATE
D_FF = 4 * D_MODEL
EPS = 1e-6
ATTN_SCALE = HEAD_DIM ** -0.5
NEG = -1e30
LOG2_E = 1.4426950408889634

SUBLANES = 8
LANES = 128
V7X_VMEM_LIMIT = 52 * 1024 * 1024

OFF_Z = 0
OFF_GATE = D_INNER
OFF_XBC = 2 * D_INNER
OFF_Q = OFF_XBC + CONV_DIM
OFF_K = OFF_Q + QKV_W
OFF_V = OFF_K + QKV_W
OFF_DT = OFF_V + QKV_W
DT_W = 2 * LANES
PROJ_W = OFF_DT + DT_W

NT_DIMS = (((1,), (1,)), ((), ()))
TN_DIMS = (((0,), (0,)), ((), ()))


def _params(semantics, vmem=V7X_VMEM_LIMIT):
    return pltpu.CompilerParams(dimension_semantics=semantics, vmem_limit_bytes=vmem)


def _sigmoid(x):
    return 0.5 * (jnp.tanh(0.5 * x) + 1.0)


def _silu(x):
    h = 0.5 * x
    return h + h * jnp.tanh(h)


def _softplus(x):
    return jnp.maximum(x, 0.0) + jnp.log(1.0 + jnp.exp(-jnp.abs(x)))


def _rms(x):
    return x * lax.rsqrt(jnp.mean(x * x, axis=-1, keepdims=True) + EPS)


def _ada_kernel(c_ref, w_ref, b_ref, o_ref):
    s = _silu(c_ref[...]).astype(BF16)
    o_ref[...] = jnp.dot(s, w_ref[...].astype(BF16), preferred_element_type=F32) + b_ref[...]


def _ada_mod(c_all, w_ada, b_ada):
    depth, d, n = w_ada.shape
    rows = c_all.shape[0]
    tn = 1536
    return pl.pallas_call(
        _ada_kernel,
        grid=(depth, n // tn),
        in_specs=[
            pl.BlockSpec((rows, d), lambda l, j: (0, 0)),
            pl.BlockSpec((None, d, tn), lambda l, j: (l, 0, j)),
            pl.BlockSpec((None, 1, tn), lambda l, j: (l, 0, j)),
        ],
        out_specs=pl.BlockSpec((None, rows, tn), lambda l, j: (l, 0, j)),
        out_shape=jax.ShapeDtypeStruct((depth, rows, n), F32),
        compiler_params=_params(("arbitrary", "arbitrary")),
        name="ada_mod",
    )(c_all, w_ada, b_ada.reshape(depth, 1, n))


def _in_proj_kernel(x_ref, mod_ref, g_ref, w_ref, o_ref, h_scr):
    @pl.when(pl.program_id(2) == 0)
    def _():
        y = _rms(x_ref[...]) * g_ref[...]
        h = y * (1.0 + mod_ref[:, D_MODEL:2 * D_MODEL]) + mod_ref[:, 0:D_MODEL]
        h_scr[...] = h.astype(BF16)

    o_ref[...] = jnp.dot(h_scr[...], w_ref[...], preferred_element_type=F32)


def _in_proj(x, mod, norm_g, w_in_r, layer, tm):
    bx, lx, d = x.shape
    r = mod.shape[2]
    tn = 1536
    return pl.pallas_call(
        _in_proj_kernel,
        grid=(bx, lx // tm, PROJ_W // tn),
        in_specs=[
            pl.BlockSpec((None, tm, d), lambda b, i, n: (b, i, 0)),
            pl.BlockSpec((None, None, r, 2 * d), lambda b, i, n: (layer, b, 0, 0)),
            pl.BlockSpec((None, 1, d), lambda b, i, n: (layer, 0, 0)),
            pl.BlockSpec((None, d, tn), lambda b, i, n: (layer, 0, n)),
        ],
        out_specs=pl.BlockSpec((None, tm, tn), lambda b, i, n: (b, i, n)),
        out_shape=jax.ShapeDtypeStruct((bx, lx, PROJ_W), F32),
        scratch_shapes=[pltpu.VMEM((tm, d), BF16)],
        compiler_params=_params(("arbitrary", "arbitrary", "arbitrary")),
        name="in_proj",
    )(x, mod, norm_g, w_in_r)


def _t5_bucket_np(dist):
    max_exact = NUM_BUCKETS // 2
    df = np.maximum(dist, 1).astype(np.float32)
    ratio = np.log(df / np.float32(max_exact)) / np.float32(math.log(MAX_DISTANCE / max_exact))
    large = max_exact + (ratio * np.float32(NUM_BUCKETS - max_exact)).astype(np.int32)
    large = np.minimum(large, NUM_BUCKETS - 1)
    return np.where(dist < max_exact, dist, large).astype(np.int32)


def _bucket_tiles():
    q = np.arange(BAND)[:, None]
    c = np.arange(2 * BAND)[None, :]
    j = q + BAND - c
    valid = (j >= 0) & (j <= BAND)
    tiles = []
    for _, dil in DIL_GROUPS:
        b = _t5_bucket_np(np.clip(j, 0, BAND) * dil)
        tiles.append(np.where(valid, b, -1))
    return np.stack(tiles).astype(np.int32)


def _bucket_rows():
    width = max(win for win, _ in DIL_GROUPS)
    i = np.arange(width)
    rows = []
    for win, dil in DIL_GROUPS:
        dist = win - i
        valid = (i < win) & (dist % dil == 0)
        rows.append(np.where(valid, _t5_bucket_np(np.clip(dist, 0, win)), -1))
    return np.broadcast_to(np.stack(rows)[:, None, :], (N_GROUPS, SUBLANES, width)).astype(np.int32)


def _bias_kernel(rb_ref, bk_ref, o_ref):
    hh = pl.program_id(0)
    bk = bk_ref[...]
    acc = jnp.full(bk.shape, NEG, F32)
    for b in range(NUM_BUCKETS):
        acc = jnp.where(bk == b, rb_ref[b, hh], acc)
    o_ref[...] = acc


def _bias_lookup(rel_bias, buckets):
    n_heads = rel_bias.shape[1]
    blk = (None,) + buckets.shape[1:]
    return pl.pallas_call(
        _bias_kernel,
        grid=(n_heads,),
        in_specs=[
            pl.BlockSpec(memory_space=pltpu.SMEM),
            pl.BlockSpec(blk, lambda h: (h // HEADS_PER_GROUP, 0, 0)),
        ],
        out_specs=pl.BlockSpec(blk, lambda h: (h, 0, 0)),
        out_shape=jax.ShapeDtypeStruct((n_heads,) + buckets.shape[1:], F32),
        compiler_params=_params(("arbitrary",)),
        name="bias_lookup",
    )(rel_bias, jnp.asarray(buckets))


HEADS_PER_TILE = LANES // HEAD_DIM


ATTN_TILES_IN_FLIGHT = 4


def _attn_kernel(q_ref, kp_ref, kc_ref, vp_ref, vc_ref, bias_ref, o_ref, lse_ref, *, dil, periods):
    period = BAND * dil
    first = pl.program_id(1) == 0
    col = lax.broadcasted_iota(jnp.int32, (BAND, 2 * BAND), 1)
    pen_first = jnp.where(col < BAND, jnp.where(first, NEG, 0.0), 0.0)
    lane = lax.broadcasted_iota(jnp.int32, (BAND, LANES), 1)

    def rows_of(r, tau):
        return pl.ds(r + tau * period, BAND, stride=dil) if dil > 1 else pl.ds(tau * period, BAND)

    def tile(r, tau):
        rows = rows_of(r, tau)
        q = q_ref[rows, :]
        if tau == 0:
            kprev, vprev = kp_ref[rows_of(r, 0), :], vp_ref[rows_of(r, 0), :]
        else:
            kprev, vprev = kc_ref[rows_of(r, tau - 1), :], vc_ref[rows_of(r, tau - 1), :]
        kb = jnp.concatenate([kprev, kc_ref[rows, :]], axis=0).astype(BF16)
        vb = jnp.concatenate([vprev, vc_ref[rows, :]], axis=0).astype(BF16)
        o, lse = None, None
        for h in range(HEADS_PER_TILE):
            mine = jnp.logical_and(lane >= h * HEAD_DIM, lane < (h + 1) * HEAD_DIM)
            qh = jnp.where(mine, q, 0.0).astype(BF16)
            s = lax.dot_general(qh, kb, NT_DIMS, preferred_element_type=F32)
            s = s * ATTN_SCALE + bias_ref[h]
            if tau == 0:
                s = s + pen_first
            m = jnp.max(s, axis=-1, keepdims=True)
            p = jnp.exp(s - m)
            l = jnp.sum(p, axis=-1, keepdims=True)
            oh = jnp.dot(p.astype(BF16), vb, preferred_element_type=F32) / l
            lh = jnp.broadcast_to(m + jnp.log(l), (BAND, LANES))
            o = oh if h == 0 else jnp.where(mine, oh, o)
            lse = lh if h == 0 else jnp.where(mine, lh, lse)
        o_ref[rows, :] = o
        lse_ref[rows, :] = lse

    def residue(r, carry):
        for tau in range(periods):
            tile(r, tau)
        return carry

    if dil > 1:
        lax.fori_loop(0, dil, residue, 0, unroll=max(1, min(dil, ATTN_TILES_IN_FLIGHT // periods)))
    else:
        residue(0, 0)


def _prompt_attention(proj, bias, g):
    b, l, _ = proj.shape
    dil = DIL_GROUPS[g][1]
    period = BAND * dil
    periods = max(1, ATTN_TILES_IN_FLIGHT // dil)
    rows = period * periods
    tiles = GROUP_W // LANES
    qb, kb, vb = (off // LANES + g * tiles for off in (OFF_Q, OFF_K, OFF_V))
    blk = (None, rows, LANES)
    cur = lambda cb: pl.BlockSpec(blk, lambda bb, i, t: (bb, i, cb + t))
    prev = lambda cb: pl.BlockSpec((None, period, LANES),
                                   lambda bb, i, t: (bb, jnp.maximum(i * periods - 1, 0), cb + t))
    out_spec = pl.BlockSpec(blk, lambda bb, i, t: (bb, i, t))
    return pl.pallas_call(
        functools.partial(_attn_kernel, dil=dil, periods=periods),
        grid=(b, l // rows, tiles),
        in_specs=[cur(qb), prev(kb), cur(kb), prev(vb), cur(vb),
                  pl.BlockSpec((HEADS_PER_TILE, BAND, 2 * BAND), lambda bb, i, t: (g * tiles + t, 0, 0))],
        out_specs=[out_spec, out_spec],
        out_shape=[jax.ShapeDtypeStruct((b, l, GROUP_W), F32)] * 2,
        compiler_params=_params(("arbitrary", "arbitrary", "arbitrary")),
        name=f"prompt_attn_g{g}",
    )(proj, proj, proj, proj, proj, bias)


def _expand4(arr, g, lane):
    rows = arr.shape[0]
    c = [jnp.broadcast_to(arr[:, 4 * g + e:4 * g + e + 1], (rows, SSD_GROUP_W)) for e in range(4)]
    return jnp.where(lane < SSD_P, c[0], jnp.where(lane < 2 * SSD_P, c[1],
                                                    jnp.where(lane < 3 * SSD_P, c[2], c[3])))


def _cumsum_rows(a):
    row = lax.broadcasted_iota(jnp.int32, a.shape, 0)
    s = 1
    while s < a.shape[0]:
        a = a + jnp.where(row >= s, pltpu.roll(a, s, axis=0), 0.0)
        s *= 2
    return a


def _ssd_kernel(xbc_ref, dt_ref, z_ref, cw_ref, cb_ref, dtb_ref, alog_ref, dskip_ref, ng_ref,
                y_ref, st_ref, cs_ref, xp_scr, xc_scr, stt_scr):
    q = BAND
    c = pl.program_id(1)
    last = c == pl.num_programs(1) - 1

    @pl.when(c == 0)
    def _():
        xp_scr[0:SUBLANES, :] = jnp.zeros((SUBLANES, CONV_DIM), F32)
        stt_scr[...] = jnp.zeros(stt_scr.shape, F32)

    xp_scr[SUBLANES:SUBLANES + q, :] = xbc_ref[...]
    cblk = 512
    nt = q // SUBLANES
    sub = lax.broadcasted_iota(jnp.int32, (nt, SUBLANES, cblk), 1)
    for j in range(CONV_DIM // cblk):
        cs = slice(j * cblk, (j + 1) * cblk)
        x3 = xp_scr[:, cs].reshape(nt + 1, SUBLANES, cblk)
        acc = cb_ref[:, cs] + x3[1:] * cw_ref[D_CONV - 1:D_CONV, cs]
        for k in range(1, D_CONV):
            rot = pltpu.roll(x3, k, axis=1)
            back = jnp.where(sub < k, rot[:-1], rot[1:])
            acc = acc + back * cw_ref[D_CONV - 1 - k:D_CONV - k, cs]
        xc_scr[:, cs] = _silu(acc).reshape(q, cblk)
    xp_scr[0:SUBLANES, :] = xp_scr[q:q + SUBLANES, :]

    dt = _softplus(dt_ref[:, 0:LANES] + dtb_ref[...])
    a = dt * (-jnp.exp(alog_ref[...]))
    acs = _cumsum_rows(a) * LOG2_E
    acs_t = acs.T
    dt_t = dt.T
    ea_last = jnp.exp2(acs[q - 1:q, :])
    w_end_t = dt_t * jnp.exp2(jnp.broadcast_to(acs_t[:, q - 1:q], (q, q)) - acs_t)

    row = lax.broadcasted_iota(jnp.int32, (q, q), 0)
    colq = lax.broadcasted_iota(jnp.int32, (q, q), 1)
    tril = row >= colq
    lane = lax.broadcasted_iota(jnp.int32, (q, SSD_GROUP_W), 1)
    lane1 = lax.broadcasted_iota(jnp.int32, (1, SSD_GROUP_W), 1)

    for g in range(SSD_GROUPS):
        gs = slice(g * SSD_GROUP_W, (g + 1) * SSD_GROUP_W)
        bg = xc_scr[:, D_INNER + g * D_STATE:D_INNER + (g + 1) * D_STATE]
        cg = xc_scr[:, D_INNER + (SSD_GROUPS + g) * D_STATE:D_INNER + (SSD_GROUPS + g + 1) * D_STATE]
        cbm = lax.dot_general(cg.astype(BF16), bg.astype(BF16), NT_DIMS, preferred_element_type=F32)
        bg_t = bg.T
        xg = xc_scr[:, gs]
        xgb = xg.astype(BF16)
        stg = stt_scr[:, gs]
        rhs = jnp.concatenate([xgb, stg.astype(BF16)], axis=0)
        y, snew = None, None
        for e in range(4):
            h = 4 * g + e
            col_h = jnp.broadcast_to(acs[:, h:h + 1], (q, q))
            row_h = lambda v: jnp.broadcast_to(v[h:h + 1, :], (q, q))
            intra = cbm * jnp.exp2(jnp.where(tril, col_h - row_h(acs_t), NEG)) * row_h(dt_t)
            inter = cg * jnp.exp2(col_h)
            r = jnp.dot(jnp.concatenate([intra, inter], axis=1).astype(BF16), rhs,
                        preferred_element_type=F32)
            sr = jnp.dot((bg_t * row_h(w_end_t)).astype(BF16), xgb, preferred_element_type=F32)
            mine = lane >= e * SSD_P
            y = r if e == 0 else jnp.where(mine, r, y)
            snew = sr if e == 0 else jnp.where(mine, sr, snew)
        y = y + dskip_ref[:, gs] * xg
        stt_scr[:, gs] = stg * _expand4(ea_last, g, lane1) + snew
        hg = y * _silu(z_ref[:, gs])
        y_ref[:, gs] = (_rms(hg) * ng_ref[:, gs]).astype(BF16)

    @pl.when(last)
    def _():
        cs_ref[...] = xp_scr[SUBLANES + q - (D_CONV - 1):SUBLANES + q, :]
        for k in range(D_INNER // LANES):
            st_ref[k * LANES:(k + 1) * LANES, :] = stt_scr[:, k * LANES:(k + 1) * LANES].T


def _prompt_ssd(proj, conv_w, conv_b, dt_bias, a_log, d_skip, norm_g, layer):
    b, l, _ = proj.shape
    q = BAND
    vec = lambda w: pl.BlockSpec((None, 1, w), lambda bb, c: (layer, 0, 0))
    return pl.pallas_call(
        _ssd_kernel,
        grid=(b, l // q),
        in_specs=[
            pl.BlockSpec((None, q, CONV_DIM), lambda bb, c: (bb, c, OFF_XBC // CONV_DIM)),
            pl.BlockSpec((None, q, DT_W), lambda bb, c: (bb, c, OFF_DT // DT_W)),
            pl.BlockSpec((None, q, D_INNER), lambda bb, c: (bb, c, OFF_Z // D_INNER)),
            pl.BlockSpec((None, D_CONV, CONV_DIM), lambda bb, c: (layer, 0, 0)),
            vec(CONV_DIM), vec(LANES), vec(LANES), vec(D_INNER), vec(D_INNER),
        ],
        out_specs=[
            pl.BlockSpec((None, q, D_INNER), lambda bb, c: (bb, c, 0)),
            pl.BlockSpec((None, D_INNER, D_STATE), lambda bb, c: (bb, 0, 0)),
            pl.BlockSpec((None, D_CONV - 1, CONV_DIM), lambda bb, c: (bb, 0, 0)),
        ],
        out_shape=[
            jax.ShapeDtypeStruct((b, l, D_INNER), BF16),
            jax.ShapeDtypeStruct((b, D_INNER, D_STATE), F32),
            jax.ShapeDtypeStruct((b, D_CONV - 1, CONV_DIM), F32),
        ],
        scratch_shapes=[
            pltpu.VMEM((q + SUBLANES, CONV_DIM), F32),
            pltpu.VMEM((q, CONV_DIM), F32),
            pltpu.VMEM((D_STATE, D_INNER), F32),
        ],
        compiler_params=_params(("arbitrary", "arbitrary")),
        name="prompt_ssd",
    )(proj, proj, proj, conv_w, conv_b, dt_bias, a_log, d_skip, norm_g)


def _out_proj_kernel(*refs, n_attn):
    a_refs = refs[:n_attn]
    ssd_ref, gate_ref, x_ref, g1_ref, wa_ref, ws_ref, wo_ref, o_ref = refs[n_attn:]
    if n_attn == 1:
        attn = a_refs[0][...]
    else:
        outs, lse = a_refs[0::2], [r[...] for r in a_refs[1::2]]
        m = functools.reduce(jnp.maximum, lse)
        e = [jnp.exp(v - m) for v in lse]
        attn = sum(ev * r[...] for ev, r in zip(e, outs)) / sum(e)
    pa = jnp.dot(attn.astype(BF16), wa_ref[...], preferred_element_type=F32)
    ps = jnp.dot(ssd_ref[...], ws_ref[...], preferred_element_type=F32)
    merged = _sigmoid(gate_ref[:, 0:D_MODEL]) * pa + _sigmoid(gate_ref[:, D_MODEL:2 * D_MODEL]) * ps
    o_ref[...] = x_ref[...] + g1_ref[...] * jnp.dot(
        merged.astype(BF16), wo_ref[...], preferred_element_type=F32)


def _out_proj(attn_list, ssd, proj, x, mod, w_o_attn, w_o_ssd, w_out, layer, tm):
    bx, lx, d = x.shape
    r = mod.shape[2]
    row = lambda w, cb=0: pl.BlockSpec((None, tm, w), lambda b, i: (b, i, cb))
    wgt = lambda k, n: pl.BlockSpec((None, k, n), lambda b, i: (layer, 0, 0))
    attn_w = attn_list[0].shape[-1]
    return pl.pallas_call(
        functools.partial(_out_proj_kernel, n_attn=len(attn_list)),
        grid=(bx, lx // tm),
        in_specs=[row(attn_w)] * len(attn_list) + [
            row(D_INNER), row(2 * D_MODEL, OFF_GATE // (2 * D_MODEL)), row(d),
            pl.BlockSpec((None, None, r, d), lambda b, i: (layer, b, 0, 2)),
            wgt(GROUP_W, d), wgt(D_INNER, d), wgt(d, d),
        ],
        out_specs=row(d),
        out_shape=jax.ShapeDtypeStruct((bx, lx, d), F32),
        compiler_params=_params(("arbitrary", "arbitrary")),
        name="out_proj",
    )(*attn_list, ssd, proj, x, mod, w_o_attn, w_o_ssd, w_out)


def _mlp_kernel(x_ref, mod_ref, g_ref, wu_ref, wd_ref, fg_ref, o_ref, h_scr, acc_scr, *, final):
    f = pl.program_id(2)

    @pl.when(f == 0)
    def _():
        y = _rms(x_ref[...]) * g_ref[...]
        h = y * (1.0 + mod_ref[:, D_MODEL:2 * D_MODEL]) + mod_ref[:, 0:D_MODEL]
        h_scr[...] = h.astype(BF16)
        acc_scr[...] = jnp.zeros(acc_scr.shape, F32)

    u = jnp.maximum(jnp.dot(h_scr[...], wu_ref[...], preferred_element_type=F32), 0.0)
    acc_scr[...] += jnp.dot((u * u).astype(BF16), wd_ref[...], preferred_element_type=F32)

    @pl.when(f == pl.num_programs(2) - 1)
    def _():
        x2 = x_ref[...] + mod_ref[:, 2 * D_MODEL:3 * D_MODEL] * acc_scr[...]
        if final:
            x2 = _rms(x2) * fg_ref[...]
        o_ref[...] = x2


def _mlp(x, mod, norm_g, w_up, w_down, final_g, layer, tm, final):
    bx, lx, d = x.shape
    r = mod.shape[2]
    tf = 1024
    return pl.pallas_call(
        functools.partial(_mlp_kernel, final=final),
        grid=(bx, lx // tm, D_FF // tf),
        in_specs=[
            pl.BlockSpec((None, tm, d), lambda b, i, f: (b, i, 0)),
            pl.BlockSpec((None, None, r, 3 * d), lambda b, i, f: (layer, b, 0, 1)),
            pl.BlockSpec((None, 1, d), lambda b, i, f: (layer, 0, 0)),
            pl.BlockSpec((None, d, tf), lambda b, i, f: (layer, 0, f)),
            pl.BlockSpec((None, tf, d), lambda b, i, f: (layer, f, 0)),
            pl.BlockSpec((1, d), lambda b, i, f: (0, 0)),
        ],
        out_specs=pl.BlockSpec((None, tm, d), lambda b, i, f: (b, i, 0)),
        out_shape=jax.ShapeDtypeStruct((bx, lx, d), F32),
        scratch_shapes=[pltpu.VMEM((tm, d), BF16), pltpu.VMEM((tm, d), F32)],
        compiler_params=_params(("arbitrary", "arbitrary", "arbitrary")),
        name="mlp",
    )(x, mod, norm_g, w_up, w_down, final_g)


def _dec_attn_kernel(*refs, n_alias):
    n = N_GROUPS
    proj_ref, c_refs, b_refs, bself_ref = refs[0], refs[1:1 + n], refs[1 + n:1 + 2 * n], refs[1 + 2 * n]
    o_ref, oc_refs = refs[2 + 2 * n + n_alias], refs[3 + 2 * n + n_alias:3 + 3 * n + n_alias]
    row = lax.broadcasted_iota(jnp.int32, (SUBLANES, GROUP_W), 0)
    lane = lax.broadcasted_iota(jnp.int32, (SUBLANES, GROUP_W), 1)
    hmask = jnp.logical_and(lane >= row * HEAD_DIM, lane < (row + 1) * HEAD_DIM)
    sel = jnp.logical_or(jnp.logical_and(row == 0, lane < LANES),
                         jnp.logical_and(row == 1, lane >= LANES)).astype(F32)
    last_lane = lax.broadcasted_iota(jnp.int32, (GROUP_W, LANES), 1) == LANES - 1
    parts = []
    for g in range(n):
        c_ref, oc_ref = c_refs[g], oc_refs[g]
        lb = c_ref.shape[-1]
        qv = proj_ref[:, OFF_Q + g * GROUP_W:OFF_Q + (g + 1) * GROUP_W]
        kn = proj_ref[:, OFF_K + g * GROUP_W:OFF_K + (g + 1) * GROUP_W]
        vn = proj_ref[:, OFF_V + g * GROUP_W:OFF_V + (g + 1) * GROUP_W]
        qbd = jnp.where(hmask, jnp.broadcast_to(qv, (SUBLANES, GROUP_W)), 0.0)
        kt = jnp.concatenate([c_ref[0, h] for h in range(HEADS_PER_GROUP)], axis=0)
        vt = jnp.concatenate([c_ref[1, h] for h in range(HEADS_PER_GROUP)], axis=0)
        s = jnp.dot(qbd.astype(BF16), kt.astype(BF16), preferred_element_type=F32)
        s = s * ATTN_SCALE + b_refs[g][...]
        s_self = jnp.sum(qbd * kn, axis=-1, keepdims=True) * ATTN_SCALE + bself_ref[g][:, 0:1]
        m = jnp.maximum(jnp.max(s, axis=-1, keepdims=True), s_self)
        p = jnp.exp(s - m)
        p_self = jnp.exp(s_self - m)
        l = jnp.sum(p, axis=-1, keepdims=True) + p_self
        acc = lax.dot_general(p.astype(BF16), vt.astype(BF16), NT_DIMS,
                              preferred_element_type=F32) + p_self * vn
        parts.append((m, l, acc))
        rows2 = jnp.where(row == 0, jnp.broadcast_to(kn, (SUBLANES, GROUP_W)),
                          jnp.where(row == 1, jnp.broadcast_to(vn, (SUBLANES, GROUP_W)), 0.0))
        cols = lax.dot_general(rows2, sel, TN_DIMS, precision=lax.Precision.HIGHEST,
                               preferred_element_type=F32)
        for kv, t in ((0, kt), (1, vt)):
            rolled = pltpu.roll(t, lb - 1, axis=1)
            tail = jnp.where(last_lane, cols[:, kv * LANES:(kv + 1) * LANES], rolled[:, lb - LANES:lb])
            new = tail if lb == LANES else jnp.concatenate([rolled[:, 0:lb - LANES], tail], axis=1)
            for h in range(HEADS_PER_GROUP):
                oc_ref[kv, h] = new[h * HEAD_DIM:(h + 1) * HEAD_DIM, :]
    m_all = functools.reduce(jnp.maximum, [p[0] for p in parts])
    num = sum(jnp.exp(m - m_all) * acc for m, _, acc in parts)
    den = sum(jnp.exp(m - m_all) * l for m, l, _ in parts)
    o_ref[...] = jnp.sum(jnp.where(hmask, num / den, 0.0), axis=0, keepdims=True)


def _decode_attention(proj_s, caches_t, bias_dec, bias_self, prev_out, layer):
    bs = proj_s.shape[1]
    n = N_GROUPS
    cspec = lambda c: pl.BlockSpec((None, None) + c.shape[2:], lambda b: (layer, b, 0, 0, 0, 0))
    in_specs = [pl.BlockSpec((None, 1, PROJ_W), lambda b: (b, 0, 0))]
    in_specs += [cspec(c) for c in caches_t]
    in_specs += [pl.BlockSpec(bd.shape, lambda b: (0, 0)) for bd in bias_dec]
    in_specs += [pl.BlockSpec(bias_self.shape, lambda b: (0, 0, 0))]
    args = [proj_s.reshape(bs, 1, PROJ_W), *caches_t, *bias_dec, bias_self]
    aliases = {}
    if prev_out is not None:
        in_specs += [pl.BlockSpec(memory_space=pl.ANY)] * n
        aliases = {len(args) + j: 1 + j for j in range(n)}
        args += list(prev_out)
    out = pl.pallas_call(
        functools.partial(_dec_attn_kernel, n_alias=len(aliases)),
        grid=(bs,),
        in_specs=in_specs,
        out_specs=[pl.BlockSpec((None, 1, GROUP_W), lambda b: (b, 0, 0))] + [cspec(c) for c in caches_t],
        out_shape=[jax.ShapeDtypeStruct((bs, 1, GROUP_W), F32)] + [
            jax.ShapeDtypeStruct(c.shape, c.dtype) for c in caches_t],
        input_output_aliases=aliases,
        compiler_params=_params(("arbitrary",)),
        name="decode_attn",
    )(*args)
    return out[0].reshape(1, bs, GROUP_W), out[1:]


def _dec_ssd_kernel(*refs, n_alias):
    proj_ref, cs_ref, st_ref, cw_ref, cb_ref, dtb_ref, alog_ref, dskip_ref, ng_ref = refs[:9]
    y_ref, nst_ref, ncs_ref, xc_scr, xdt_t_scr, da_t_scr, ct_scr, yt_scr = refs[9 + n_alias:]
    b = pl.program_id(0)
    bs = proj_ref.shape[0]
    nblk = D_INNER // LANES
    pad_rows = lambda v: jnp.concatenate([v, jnp.zeros((LANES - bs, v.shape[1]), v.dtype)], axis=0)

    @pl.when(b == 0)
    def _():
        xr = proj_ref[:, OFF_XBC:OFF_XBC + CONV_DIM]
        acc = cb_ref[...]
        for t in range(D_CONV - 1):
            acc = acc + cs_ref[t] * cw_ref[t:t + 1, :]
        acc = acc + xr * cw_ref[D_CONV - 1:D_CONV, :]
        xc = pad_rows(_silu(acc))
        xc_scr[...] = xc
        for t in range(D_CONV - 2):
            ncs_ref[t] = cs_ref[t + 1]
        ncs_ref[D_CONV - 2] = xr
        dt = _softplus(proj_ref[:, OFF_DT:OFF_DT + LANES] + dtb_ref[...])
        da = jnp.exp(dt * (-jnp.exp(alog_ref[...])))
        onehot = (lax.shift_right_logical(lax.broadcasted_iota(jnp.int32, (LANES, D_INNER), 1), 6)
                  == lax.broadcasted_iota(jnp.int32, (LANES, D_INNER), 0)).astype(F32)
        expand = lambda v: pad_rows(jnp.dot(v, onehot, precision=lax.Precision.HIGHEST,
                                            preferred_element_type=F32))
        xdt = xc[:, 0:D_INNER] * expand(dt)
        dae = expand(da)
        for k in range(nblk):
            ks = slice(k * LANES, (k + 1) * LANES)
            xdt_t_scr[ks, :] = xdt[:, ks].T.astype(BF16)
            da_t = dae[:, ks].T
            da_hi = da_t.astype(BF16)
            da_t_scr[ks, 0:LANES] = da_hi
            da_t_scr[ks, LANES:2 * LANES] = (da_t - da_hi.astype(F32)).astype(BF16)
        for g in range(SSD_GROUPS):
            ct_scr[g] = xc[:, D_INNER + (SSD_GROUPS + g) * D_STATE:D_INNER + (SSD_GROUPS + g + 1) * D_STATE].T
        yt_scr[...] = jnp.zeros(yt_scr.shape, F32)

    row = lax.broadcasted_iota(jnp.int32, (LANES, LANES), 0)
    lane = lax.broadcasted_iota(jnp.int32, (LANES, LANES), 1)
    pick = jnp.where(row == b, 1.0, 0.0).astype(BF16)
    decay = jnp.dot(da_t_scr[...], jnp.concatenate([pick, pick], axis=0), preferred_element_type=F32)
    for g in range(SSD_GROUPS):
        rs = slice(g * SSD_GROUP_W, (g + 1) * SSD_GROUP_W)
        bg = xc_scr[:, D_INNER + g * D_STATE:D_INNER + (g + 1) * D_STATE]
        b_sel = jnp.where(row == b, bg, 0.0).astype(BF16)
        hn = st_ref[rs, :] * decay[rs, :] + jnp.dot(xdt_t_scr[rs, :], b_sel, preferred_element_type=F32)
        nst_ref[rs, :] = hn
        c_sel = jnp.where(lane == b, ct_scr[g], 0.0).astype(BF16)
        yt_scr[rs, :] += jnp.dot(hn.astype(BF16), c_sel, preferred_element_type=F32)

    @pl.when(b == pl.num_programs(0) - 1)
    def _():
        for g in range(SSD_GROUPS):
            gs = slice(g * SSD_GROUP_W, (g + 1) * SSD_GROUP_W)
            yg = jnp.concatenate([yt_scr[k * LANES:(k + 1) * LANES, :].T[0:bs, :]
                                  for k in range(g * SSD_GROUP_W // LANES, (g + 1) * SSD_GROUP_W // LANES)],
                                 axis=1)
            y = yg + dskip_ref[:, gs] * xc_scr[0:bs, gs]
            hg = y * _silu(proj_ref[:, OFF_Z + g * SSD_GROUP_W:OFF_Z + (g + 1) * SSD_GROUP_W])
            y_ref[:, gs] = (_rms(hg) * ng_ref[:, gs]).astype(BF16)


def _decode_ssd(proj_s, conv_state_t, state, conv_w, conv_b, dt_bias, a_log, d_skip, norm_g, prev_state, layer):
    bs = proj_s.shape[1]
    vec = lambda w: pl.BlockSpec((None, 1, w), lambda b: (layer, 0, 0))
    st_spec = pl.BlockSpec((None, None, D_INNER, D_STATE), lambda b: (layer, b, 0, 0))
    args = [proj_s, conv_state_t, state, conv_w, conv_b, dt_bias, a_log, d_skip, norm_g]
    extra, aliases = [], {}
    if prev_state is not None:
        extra, aliases = [pl.BlockSpec(memory_space=pl.ANY)], {len(args): 1}
        args.append(prev_state)
    return pl.pallas_call(
        functools.partial(_dec_ssd_kernel, n_alias=len(aliases)),
        grid=(bs,),
        in_specs=[
            pl.BlockSpec((None, bs, PROJ_W), lambda b: (0, 0, 0)),
            pl.BlockSpec((None, D_CONV - 1, bs, CONV_DIM), lambda b: (layer, 0, 0, 0)),
            st_spec,
            pl.BlockSpec((None, D_CONV, CONV_DIM), lambda b: (layer, 0, 0)),
            vec(CONV_DIM), vec(LANES), vec(LANES), vec(D_INNER), vec(D_INNER),
        ] + extra,
        out_specs=[
            pl.BlockSpec((None, bs, D_INNER), lambda b: (0, 0, 0)),
            st_spec,
            pl.BlockSpec((D_CONV - 1, bs, CONV_DIM), lambda b: (0, 0, 0)),
        ],
        out_shape=[
            jax.ShapeDtypeStruct((1, bs, D_INNER), BF16),
            jax.ShapeDtypeStruct(state.shape, F32),
            jax.ShapeDtypeStruct((D_CONV - 1, bs, CONV_DIM), F32),
        ],
        input_output_aliases=aliases,
        scratch_shapes=[
            pltpu.VMEM((LANES, CONV_DIM), F32),
            pltpu.VMEM((D_INNER, LANES), BF16),
            pltpu.VMEM((D_INNER, 2 * LANES), BF16),
            pltpu.VMEM((SSD_GROUPS, D_STATE, LANES), F32),
            pltpu.VMEM((D_INNER, LANES), F32),
        ],
        compiler_params=_params(("arbitrary",)),
        name="decode_ssd",
    )(*args)


def _reorder_w_in(w_in):
    q0, z0, x0, d0, g0 = 0, 3 * QKV_W, 3 * QKV_W + D_INNER, 3 * QKV_W + D_INNER + CONV_DIM, \
        3 * QKV_W + D_INNER + CONV_DIM + SSD_HEADS
    pieces = [w_in[..., z0:x0], w_in[..., g0:g0 + 2 * D_MODEL], w_in[..., x0:d0], w_in[..., q0:z0],
              w_in[..., d0:g0], jnp.zeros(w_in.shape[:-1] + (DT_W - SSD_HEADS,), w_in.dtype)]
    return jnp.concatenate(pieces, axis=-1).astype(BF16)


def _pad_lanes(v, width):
    return jnp.pad(v, ((0, 0), (0, width - v.shape[-1])))[:, None, :]


def kernel(x_prompt, x_sample, cache_kv_g0, cache_kv_g1, cache_kv_g2, state_ssm, state_conv, c_prompt,
           c_sample, rel_bias, w_ada, b_ada, norm1_g, norm2_g, w_in, conv_w, conv_b, dt_bias, a_log, d_skip,
           ssd_norm_g, w_o_attn, w_o_ssd, w_out, w_up, w_down, final_g):
    depth = w_in.shape[0]
    b, l, d = x_prompt.shape
    bs = x_sample.shape[0]
    caches = (cache_kv_g0, cache_kv_g1, cache_kv_g2)
    assert d == D_MODEL and x_sample.shape[1] == 1 and l % (DIL_GROUPS[-1][1] * BAND) == 0
    assert all(win == BAND * dil for win, dil in DIL_GROUPS)
    assert all(c.shape[2] == win for c, (win, _) in zip(caches, DIL_GROUPS))
    assert bs % SUBLANES == 0 and bs <= LANES

    w_in_r = _reorder_w_in(w_in)
    w_oa, w_os, w_o, w_u, w_d = (w.astype(BF16) for w in (w_o_attn, w_o_ssd, w_out, w_up, w_down))
    n1, n2 = norm1_g[:, None, :], norm2_g[:, None, :]
    conv_b3, ng3 = conv_b[:, None, :], ssd_norm_g[:, None, :]
    dtb3, alog3 = _pad_lanes(dt_bias, LANES), _pad_lanes(a_log, LANES)
    dskip3 = jnp.repeat(d_skip, SSD_P, axis=-1)[:, None, :]
    fg = final_g[None, :]

    rows = b + bs
    rows_pad = -(-rows // SUBLANES) * SUBLANES
    c_all = jnp.pad(jnp.concatenate([c_prompt, c_sample], axis=0), ((0, rows_pad - rows), (0, 0)))
    mod = _ada_mod(c_all, w_ada, b_ada)
    mod_p = mod[:, :b, None, :]
    mod_s = mod[:, None, b:rows, :]

    bias = _bias_lookup(rel_bias, _bucket_tiles())
    head_rows = ((0, SUBLANES - HEADS_PER_GROUP), (0, 0))
    bias_rows = _bias_lookup(rel_bias, _bucket_rows())[:, 0, :]
    bias_dec = [jnp.pad(bias_rows[HEADS_PER_GROUP * g:HEADS_PER_GROUP * (g + 1), :win], head_rows)
                for g, (win, _) in enumerate(DIL_GROUPS)]
    self_bias = bias[:, 0, BAND].reshape(N_GROUPS, HEADS_PER_GROUP, 1)
    bias_self = jnp.pad(jnp.broadcast_to(self_bias, (N_GROUPS, HEADS_PER_GROUP, LANES)), ((0, 0),) + head_rows)

    xp = x_prompt
    xs = x_sample.reshape(1, bs, d)
    conv_state_t = jnp.swapaxes(state_conv, 1, 2)
    state_r = state_ssm.reshape(depth, bs, D_INNER, D_STATE)
    caches_t = [jnp.transpose(c, (0, 1, 3, 4, 5, 2)) for c in caches]
    tm_p = 1024 if l % 1024 == 0 else BAND

    kv_p = [[] for _ in DIL_GROUPS]
    ssm_p, conv_p, conv_s = [], [], []
    kv_s, st_s = None, None
    for layer in range(depth):
        final = layer == depth - 1
        proj = _in_proj(xp, mod_p, n1, w_in_r, layer, tm_p)
        attn = [a for g in range(N_GROUPS) for a in _prompt_attention(proj, bias, g)]
        ssd, st, cs = _prompt_ssd(proj, conv_w, conv_b3, dtb3, alog3, dskip3, ng3, layer)
        x1 = _out_proj(attn, ssd, proj, xp, mod_p, w_oa, w_os, w_o, layer, min(256, l))
        xp = _mlp(x1, mod_p, n2, w_u, w_d, fg, layer, tm_p, final)
        for g, (win, _) in enumerate(DIL_GROUPS):
            keep = min(win, l)
            kk = proj[:, l - keep:, OFF_K + g * GROUP_W:OFF_K + (g + 1) * GROUP_W]
            vv = proj[:, l - keep:, OFF_V + g * GROUP_W:OFF_V + (g + 1) * GROUP_W]
            kv_p[g].append(jnp.stack([kk, vv], axis=2).reshape(b, keep, 2, HEADS_PER_GROUP, HEAD_DIM))
        ssm_p.append(st.reshape(b, SSD_HEADS, SSD_P, D_STATE))
        conv_p.append(cs)
        proj_s = _in_proj(xs, mod_s, n1, w_in_r, layer, bs)
        attn_s, kv_s = _decode_attention(proj_s, caches_t, bias_dec, bias_self, kv_s, layer)
        ssd_s, st_s, cs_s = _decode_ssd(proj_s, conv_state_t, state_r, conv_w, conv_b3, dtb3, alog3,
                                        dskip3, ng3, st_s, layer)
        x1s = _out_proj([attn_s], ssd_s, proj_s, xs, mod_s, w_oa, w_os, w_o, layer, bs)
        xs = _mlp(x1s, mod_s, n2, w_u, w_d, fg, layer, bs, final)
        conv_s.append(jnp.swapaxes(cs_s, 0, 1))

    kv_s = [jnp.transpose(o, (0, 1, 5, 2, 3, 4)) for o in kv_s]
    return (xp, xs.reshape(bs, 1, d),
            jnp.stack(kv_p[0], axis=0), jnp.stack(kv_p[1], axis=0), jnp.stack(kv_p[2], axis=0),
            jnp.stack(ssm_p, axis=0), jnp.stack(conv_p, axis=0),
            kv_s[0], kv_s[1], kv_s[2],
            st_s.reshape(depth, bs, SSD_HEADS, SSD_P, D_STATE), jnp.stack(conv_s, axis=0))
```

```python
import functools
import math

import jax
import jax.numpy as jnp
import numpy as np
from jax import lax
from jax.experimental import pallas as pl
from jax.experimental.pallas import tpu as pltpu

F32 = jnp.float32
BF16 = jnp.bfloat16

D_MODEL = 1024
HEAD_DIM = 64
HEADS_PER_GROUP = 4
DIL_GROUPS = ((128, 1), (512, 4), (2048, 16))
N_GROUPS = len(DIL_GROUPS)
GROUP_W = HEADS_PER_GROUP * HEAD_DIM
QKV_W = N_GROUPS * GROUP_W
BAND = 128
NUM_BUCKETS = 32
MAX_DISTANCE = 2048
D_INNER = 2 * D_MODEL
SSD_HEADS = 32
SSD_P = 64
SSD_GROUPS = 8
SSD_GROUP_W = D_INNER // SSD_GROUPS
D_STATE = 128
D_CONV = 4
CONV_DIM = D_INNER + 2 * SSD_GROUPS * D_STATE
D_FF = 4 * D_MODEL
EPS = 1e-6
ATTN_SCALE = HEAD_DIM ** -0.5
NEG = -1e30
LOG2_E = 1.4426950408889634

SUBLANES = 8
LANES = 128
V7X_VMEM_LIMIT = 52 * 1024 * 1024

OFF_Z = 0
OFF_GATE = D_INNER
OFF_XBC = 2 * D_INNER
OFF_Q = OFF_XBC + CONV_DIM
OFF_K = OFF_Q + QKV_W
OFF_V = OFF_K + QKV_W
OFF_DT = OFF_V + QKV_W
DT_W = 2 * LANES
PROJ_W = OFF_DT + DT_W

NT_DIMS = (((1,), (1,)), ((), ()))
TN_DIMS = (((0,), (0,)), ((), ()))


def _params(semantics, vmem=V7X_VMEM_LIMIT):
    return pltpu.CompilerParams(dimension_semantics=semantics, vmem_limit_bytes=vmem)


def _sigmoid(x):
    return 0.5 * (jnp.tanh(0.5 * x) + 1.0)


def _silu(x):
    h = 0.5 * x
    return h + h * jnp.tanh(h)


def _softplus(x):
    return jnp.maximum(x, 0.0) + jnp.log(1.0 + jnp.exp(-jnp.abs(x)))


def _rms(x):
    return x * lax.rsqrt(jnp.mean(x * x, axis=-1, keepdims=True) + EPS)


def _ada_kernel(c_ref, w_ref, b_ref, o_ref):
    s = _silu(c_ref[...]).astype(BF16)
    o_ref[...] = jnp.dot(s, w_ref[...].astype(BF16), preferred_element_type=F32) + b_ref[...]


def _ada_mod(c_all, w_ada, b_ada):
    depth, d, n = w_ada.shape
    rows = c_all.shape[0]
    tn = 1536
    return pl.pallas_call(
        _ada_kernel,
        grid=(depth, n // tn),
        in_specs=[
            pl.BlockSpec((rows, d), lambda l, j: (0, 0)),
            pl.BlockSpec((None, d, tn), lambda l, j: (l, 0, j)),
            pl.BlockSpec((None, 1, tn), lambda l, j: (l, 0, j)),
        ],
        out_specs=pl.BlockSpec((None, rows, tn), lambda l, j: (l, 0, j)),
        out_shape=jax.ShapeDtypeStruct((depth, rows, n), F32),
        compiler_params=_params(("arbitrary", "arbitrary")),
        name="ada_mod",
    )(c_all, w_ada, b_ada.reshape(depth, 1, n))


def _in_proj_kernel(x_ref, mod_ref, g_ref, w_ref, o_ref, h_scr, *, tn):
    n = pl.program_id(2)

    @pl.when(n == 0)
    def _():
        y = _rms(x_ref[...]) * g_ref[...]
        h = y * (1.0 + mod_ref[:, D_MODEL:2 * D_MODEL]) + mod_ref[:, 0:D_MODEL]
        h_scr[...] = h.astype(BF16)

    w = w_ref[pl.ds(pl.multiple_of(n * tn, tn), tn), :]
    o_ref[...] = lax.dot_general(h_scr[...], w, NT_DIMS, preferred_element_type=F32)


def _in_proj(x, mod, norm_g, w_in_r, layer, tm):
    bx, lx, d = x.shape
    r = mod.shape[2]
    tn = 1536
    return pl.pallas_call(
        functools.partial(_in_proj_kernel, tn=tn),
        grid=(bx, lx // tm, PROJ_W // tn),
        in_specs=[
            pl.BlockSpec((None, tm, d), lambda b, i, n: (b, i, 0)),
            pl.BlockSpec((None, None, r, 2 * d), lambda b, i, n: (layer, b, 0, 0)),
            pl.BlockSpec((None, 1, d), lambda b, i, n: (layer, 0, 0)),
            pl.BlockSpec((None, PROJ_W, d), lambda b, i, n: (layer, 0, 0), pipeline_mode=pl.Buffered(1)),
        ],
        out_specs=pl.BlockSpec((None, tm, tn), lambda b, i, n: (b, i, n)),
        out_shape=jax.ShapeDtypeStruct((bx, lx, PROJ_W), F32),
        scratch_shapes=[pltpu.VMEM((tm, d), BF16)],
        compiler_params=_params(("arbitrary", "arbitrary", "arbitrary")),
        name="in_proj",
    )(x, mod, norm_g, w_in_r)


def _t5_bucket_np(dist):
    max_exact = NUM_BUCKETS // 2
    df = np.maximum(dist, 1).astype(np.float32)
    ratio = np.log(df / np.float32(max_exact)) / np.float32(math.log(MAX_DISTANCE / max_exact))
    large = max_exact + (ratio * np.float32(NUM_BUCKETS - max_exact)).astype(np.int32)
    large = np.minimum(large, NUM_BUCKETS - 1)
    return np.where(dist < max_exact, dist, large).astype(np.int32)


def _bucket_tiles():
    q = np.arange(BAND)[:, None]
    c = np.arange(2 * BAND)[None, :]
    j = q + BAND - c
    valid = (j >= 0) & (j <= BAND)
    tiles = []
    for _, dil in DIL_GROUPS:
        b = _t5_bucket_np(np.clip(j, 0, BAND) * dil)
        tiles.append(np.where(valid, b, -1))
    return np.stack(tiles).astype(np.int32)


def _bucket_rows():
    width = max(win for win, _ in DIL_GROUPS)
    i = np.arange(width)
    rows = []
    for win, dil in DIL_GROUPS:
        dist = win - i
        valid = (i < win) & (dist % dil == 0)
        rows.append(np.where(valid, _t5_bucket_np(np.clip(dist, 0, win)), -1))
    return np.broadcast_to(np.stack(rows)[:, None, :], (N_GROUPS, SUBLANES, width)).astype(np.int32)


def _bias_kernel(rb_ref, bk_ref, o_ref):
    hh = pl.program_id(0)
    bk = bk_ref[...]
    acc = jnp.full(bk.shape, NEG, F32)
    for b in range(NUM_BUCKETS):
        acc = jnp.where(bk == b, rb_ref[b, hh], acc)
    o_ref[...] = acc


def _bias_lookup(rel_bias, buckets):
    n_heads = rel_bias.shape[1]
    blk = (None,) + buckets.shape[1:]
    return pl.pallas_call(
        _bias_kernel,
        grid=(n_heads,),
        in_specs=[
            pl.BlockSpec(memory_space=pltpu.SMEM),
            pl.BlockSpec(blk, lambda h: (h // HEADS_PER_GROUP, 0, 0)),
        ],
        out_specs=pl.BlockSpec(blk, lambda h: (h, 0, 0)),
        out_shape=jax.ShapeDtypeStruct((n_heads,) + buckets.shape[1:], F32),
        compiler_params=_params(("arbitrary",)),
        name="bias_lookup",
    )(rel_bias, jnp.asarray(buckets))


HEADS_PER_TILE = LANES // HEAD_DIM


ATTN_TILES_IN_FLIGHT = 8


def _attn_kernel(q_ref, kp_ref, kc_ref, vp_ref, vc_ref, bias_ref, o_ref, lse_ref, *, dil, periods):
    period = BAND * dil
    first = pl.program_id(1) == 0
    col = lax.broadcasted_iota(jnp.int32, (BAND, 2 * BAND), 1)
    pen_first = jnp.where(col < BAND, jnp.where(first, NEG, 0.0), 0.0)
    lane = lax.broadcasted_iota(jnp.int32, (BAND, LANES), 1)

    def rows_of(r, tau):
        return pl.ds(r + tau * period, BAND, stride=dil) if dil > 1 else pl.ds(tau * period, BAND)

    def tile(r, tau):
        rows = rows_of(r, tau)
        q = q_ref[rows, :]
        if tau == 0:
            kprev, vprev = kp_ref[rows_of(r, 0), :], vp_ref[rows_of(r, 0), :]
        else:
            kprev, vprev = kc_ref[rows_of(r, tau - 1), :], vc_ref[rows_of(r, tau - 1), :]
        kb = jnp.concatenate([kprev, kc_ref[rows, :]], axis=0).astype(BF16)
        vb = jnp.concatenate([vprev, vc_ref[rows, :]], axis=0).astype(BF16)
        o, lse = None, None
        for h in range(HEADS_PER_TILE):
            mine = jnp.logical_and(lane >= h * HEAD_DIM, lane < (h + 1) * HEAD_DIM)
            qh = jnp.where(mine, q, 0.0).astype(BF16)
            s = lax.dot_general(qh, kb, NT_DIMS, preferred_element_type=F32)
            s = s * ATTN_SCALE + bias_ref[h]
            if tau == 0:
                s = s + pen_first
            m = jnp.max(s, axis=-1, keepdims=True)
            p = jnp.exp(s - m)
            l = jnp.sum(p, axis=-1, keepdims=True)
            oh = jnp.dot(p.astype(BF16), vb, preferred_element_type=F32) / l
            lh = jnp.broadcast_to(m + jnp.log(l), (BAND, LANES))
            o = oh if h == 0 else jnp.where(mine, oh, o)
            lse = lh if h == 0 else jnp.where(mine, lh, lse)
        o_ref[rows, :] = o
        lse_ref[rows, :] = lse

    def residue(r, carry):
        for tau in range(periods):
            tile(r, tau)
        return carry

    if dil > 1:
        lax.fori_loop(0, dil, residue, 0, unroll=max(1, min(dil, ATTN_TILES_IN_FLIGHT // periods)))
    else:
        residue(0, 0)


def _prompt_attention(proj, bias, g):
    b, l, _ = proj.shape
    dil = DIL_GROUPS[g][1]
    period = BAND * dil
    periods = max(1, ATTN_TILES_IN_FLIGHT // dil)
    rows = period * periods
    tiles = GROUP_W // LANES
    qb, kb, vb = (off // LANES + g * tiles for off in (OFF_Q, OFF_K, OFF_V))
    blk = (None, rows, LANES)
    cur = lambda cb: pl.BlockSpec(blk, lambda bb, i, t: (bb, i, cb + t))
    prev = lambda cb: pl.BlockSpec((None, period, LANES),
                                   lambda bb, i, t: (bb, jnp.maximum(i * periods - 1, 0), cb + t))
    out_spec = pl.BlockSpec(blk, lambda bb, i, t: (bb, i, t))
    return pl.pallas_call(
        functools.partial(_attn_kernel, dil=dil, periods=periods),
        grid=(b, l // rows, tiles),
        in_specs=[cur(qb), prev(kb), cur(kb), prev(vb), cur(vb),
                  pl.BlockSpec((HEADS_PER_TILE, BAND, 2 * BAND), lambda bb, i, t: (g * tiles + t, 0, 0))],
        out_specs=[out_spec, out_spec],
        out_shape=[jax.ShapeDtypeStruct((b, l, GROUP_W), F32)] * 2,
        compiler_params=_params(("arbitrary", "arbitrary", "arbitrary")),
        name=f"prompt_attn_g{g}",
    )(proj, proj, proj, proj, proj, bias)


def _expand4(arr, g, lane):
    rows = arr.shape[0]
    c = [jnp.broadcast_to(arr[:, 4 * g + e:4 * g + e + 1], (rows, SSD_GROUP_W)) for e in range(4)]
    return jnp.where(lane < SSD_P, c[0], jnp.where(lane < 2 * SSD_P, c[1],
                                                    jnp.where(lane < 3 * SSD_P, c[2], c[3])))


def _cumsum_rows(a):
    row = lax.broadcasted_iota(jnp.int32, a.shape, 0)
    s = 1
    while s < a.shape[0]:
        a = a + jnp.where(row >= s, pltpu.roll(a, s, axis=0), 0.0)
        s *= 2
    return a


def _ssd_kernel(xbc_ref, dt_ref, z_ref, cw_ref, cb_ref, dtb_ref, alog_ref, dskip_ref, ng_ref,
                y_ref, st_ref, cs_ref, xp_scr, xc_scr, stt_scr):
    q = BAND
    c = pl.program_id(1)
    last = c == pl.num_programs(1) - 1

    @pl.when(c == 0)
    def _():
        xp_scr[0:SUBLANES, :] = jnp.zeros((SUBLANES, CONV_DIM), F32)
        stt_scr[...] = jnp.zeros(stt_scr.shape, F32)

    xp_scr[SUBLANES:SUBLANES + q, :] = xbc_ref[...]
    cblk = 512
    nt = q // SUBLANES
    sub = lax.broadcasted_iota(jnp.int32, (nt, SUBLANES, cblk), 1)
    for j in range(CONV_DIM // cblk):
        cs = slice(j * cblk, (j + 1) * cblk)
        x3 = xp_scr[:, cs].reshape(nt + 1, SUBLANES, cblk)
        acc = cb_ref[:, cs] + x3[1:] * cw_ref[D_CONV - 1:D_CONV, cs]
        for k in range(1, D_CONV):
            rot = pltpu.roll(x3, k, axis=1)
            back = jnp.where(sub < k, rot[:-1], rot[1:])
            acc = acc + back * cw_ref[D_CONV - 1 - k:D_CONV - k, cs]
        xc_scr[:, cs] = _silu(acc).reshape(q, cblk)
    xp_scr[0:SUBLANES, :] = xp_scr[q:q + SUBLANES, :]

    dt = _softplus(dt_ref[:, 0:LANES] + dtb_ref[...])
    a = dt * (-jnp.exp(alog_ref[...]))
    acs = _cumsum_rows(a) * LOG2_E
    acs_t = acs.T
    dt_t = dt.T
    ea_last = jnp.exp2(acs[q - 1:q, :])
    w_end_t = dt_t * jnp.exp2(jnp.broadcast_to(acs_t[:, q - 1:q], (q, q)) - acs_t)
    src_t = acs_t - jnp.log2(dt_t)

    row = lax.broadcasted_iota(jnp.int32, (q, q), 0)
    colq = lax.broadcasted_iota(jnp.int32, (q, q), 1)
    tril = row >= colq
    lane = lax.broadcasted_iota(jnp.int32, (q, SSD_GROUP_W), 1)
    lane1 = lax.broadcasted_iota(jnp.int32, (1, SSD_GROUP_W), 1)

    for g in range(SSD_GROUPS):
        gs = slice(g * SSD_GROUP_W, (g + 1) * SSD_GROUP_W)
        bg = xc_scr[:, D_INNER + g * D_STATE:D_INNER + (g + 1) * D_STATE]
        cg = xc_scr[:, D_INNER + (SSD_GROUPS + g) * D_STATE:D_INNER + (SSD_GROUPS + g + 1) * D_STATE]
        cbm = lax.dot_general(cg.astype(BF16), bg.astype(BF16), NT_DIMS, preferred_element_type=F32)
        bg_t = bg.T
        xg = xc_scr[:, gs]
        xgb = xg.astype(BF16)
        stg = stt_scr[:, gs]
        rhs = jnp.concatenate([xgb, stg.astype(BF16)], axis=0)
        y, snew = None, None
        for e in range(4):
            h = 4 * g + e
            col_h = jnp.broadcast_to(acs[:, h:h + 1], (q, q))
            row_h = lambda v: jnp.broadcast_to(v[h:h + 1, :], (q, q))
            intra = cbm * jnp.exp2(jnp.where(tril, col_h - row_h(src_t), NEG))
            inter = cg * jnp.exp2(col_h)
            r = jnp.dot(jnp.concatenate([intra, inter], axis=1).astype(BF16), rhs,
                        preferred_element_type=F32)
            sr = jnp.dot((bg_t * row_h(w_end_t)).astype(BF16), xgb, preferred_element_type=F32)
            mine = lane >= e * SSD_P
            y = r if e == 0 else jnp.where(mine, r, y)
            snew = sr if e == 0 else jnp.where(mine, sr, snew)
        y = y + dskip_ref[:, gs] * xg
        stt_scr[:, gs] = stg * _expand4(ea_last, g, lane1) + snew
        hg = y * _silu(z_ref[:, gs])
        y_ref[:, gs] = (_rms(hg) * ng_ref[:, gs]).astype(BF16)

    @pl.when(last)
    def _():
        cs_ref[...] = xp_scr[SUBLANES + q - (D_CONV - 1):SUBLANES + q, :]
        for k in range(D_INNER // LANES):
            st_ref[k * LANES:(k + 1) * LANES, :] = stt_scr[:, k * LANES:(k + 1) * LANES].T


def _prompt_ssd(proj, conv_w, conv_b, dt_bias, a_log, d_skip, norm_g, layer):
    b, l, _ = proj.shape
    q = BAND
    vec = lambda w: pl.BlockSpec((None, 1, w), lambda bb, c: (layer, 0, 0))
    return pl.pallas_call(
        _ssd_kernel,
        grid=(b, l // q),
        in_specs=[
            pl.BlockSpec((None, q, CONV_DIM), lambda bb, c: (bb, c, OFF_XBC // CONV_DIM)),
            pl.BlockSpec((None, q, DT_W), lambda bb, c: (bb, c, OFF_DT // DT_W)),
            pl.BlockSpec((None, q, D_INNER), lambda bb, c: (bb, c, OFF_Z // D_INNER)),
            pl.BlockSpec((None, D_CONV, CONV_DIM), lambda bb, c: (layer, 0, 0)),
            vec(CONV_DIM), vec(LANES), vec(LANES), vec(D_INNER), vec(D_INNER),
        ],
        out_specs=[
            pl.BlockSpec((None, q, D_INNER), lambda bb, c: (bb, c, 0)),
            pl.BlockSpec((None, D_INNER, D_STATE), lambda bb, c: (bb, 0, 0)),
            pl.BlockSpec((None, D_CONV - 1, CONV_DIM), lambda bb, c: (bb, 0, 0)),
        ],
        out_shape=[
            jax.ShapeDtypeStruct((b, l, D_INNER), BF16),
            jax.ShapeDtypeStruct((b, D_INNER, D_STATE), F32),
            jax.ShapeDtypeStruct((b, D_CONV - 1, CONV_DIM), F32),
        ],
        scratch_shapes=[
            pltpu.VMEM((q + SUBLANES, CONV_DIM), F32),
            pltpu.VMEM((q, CONV_DIM), F32),
            pltpu.VMEM((D_STATE, D_INNER), F32),
        ],
        compiler_params=_params(("arbitrary", "arbitrary")),
        name="prompt_ssd",
    )(proj, proj, proj, conv_w, conv_b, dt_bias, a_log, d_skip, norm_g)


def _out_proj_kernel(*refs, n_attn):
    a_refs = refs[:n_attn]
    ssd_ref, gate_ref, x_ref, g1_ref, wa_ref, ws_ref, wo_ref, o_ref = refs[n_attn:]
    if n_attn == 1:
        attn = a_refs[0][...]
    else:
        outs, lse = a_refs[0::2], [r[...] for r in a_refs[1::2]]
        m = functools.reduce(jnp.maximum, lse)
        e = [jnp.exp(v - m) for v in lse]
        attn = sum(ev * r[...] for ev, r in zip(e, outs)) / sum(e)
    pa = jnp.dot(attn.astype(BF16), wa_ref[...], preferred_element_type=F32)
    ps = jnp.dot(ssd_ref[...], ws_ref[...], preferred_element_type=F32)
    merged = _sigmoid(gate_ref[:, 0:D_MODEL]) * pa + _sigmoid(gate_ref[:, D_MODEL:2 * D_MODEL]) * ps
    o_ref[...] = x_ref[...] + g1_ref[...] * jnp.dot(
        merged.astype(BF16), wo_ref[...], preferred_element_type=F32)


def _out_proj(attn_list, ssd, proj, x, mod, w_o_attn, w_o_ssd, w_out, layer, tm):
    bx, lx, d = x.shape
    r = mod.shape[2]
    row = lambda w, cb=0: pl.BlockSpec((None, tm, w), lambda b, i: (b, i, cb))
    wgt = lambda k, n: pl.BlockSpec((None, k, n), lambda b, i: (layer, 0, 0), pipeline_mode=pl.Buffered(1))
    attn_w = attn_list[0].shape[-1]
    return pl.pallas_call(
        functools.partial(_out_proj_kernel, n_attn=len(attn_list)),
        grid=(bx, lx // tm),
        in_specs=[row(attn_w)] * len(attn_list) + [
            row(D_INNER), row(2 * D_MODEL, OFF_GATE // (2 * D_MODEL)), row(d),
            pl.BlockSpec((None, None, r, d), lambda b, i: (layer, b, 0, 2)),
            wgt(GROUP_W, d), wgt(D_INNER, d), wgt(d, d),
        ],
        out_specs=row(d),
        out_shape=jax.ShapeDtypeStruct((bx, lx, d), F32),
        compiler_params=_params(("arbitrary", "arbitrary")),
        name="out_proj",
    )(*attn_list, ssd, proj, x, mod, w_o_attn, w_o_ssd, w_out)


MLP_FF_TILE = 1024


def _mlp_kernel(x_ref, mod_ref, g_ref, wu_ref, wd_ref, fg_ref, o_ref, *, final):
    x = x_ref[...]
    h = (_rms(x) * g_ref[...]) * (1.0 + mod_ref[:, D_MODEL:2 * D_MODEL]) + mod_ref[:, 0:D_MODEL]
    hb = h.astype(BF16)
    acc = None
    for f in range(D_FF // MLP_FF_TILE):
        fs = slice(f * MLP_FF_TILE, (f + 1) * MLP_FF_TILE)
        u = jnp.maximum(jnp.dot(hb, wu_ref[:, fs], preferred_element_type=F32), 0.0)
        part = jnp.dot((u * u).astype(BF16), wd_ref[fs, :], preferred_element_type=F32)
        acc = part if f == 0 else acc + part
    x2 = x + mod_ref[:, 2 * D_MODEL:3 * D_MODEL] * acc
    if final:
        x2 = _rms(x2) * fg_ref[...]
    o_ref[...] = x2


def _mlp(x, mod, norm_g, w_up, w_down, final_g, layer, tm, final):
    bx, lx, d = x.shape
    r = mod.shape[2]
    resident = pl.Buffered(1)
    return pl.pallas_call(
        functools.partial(_mlp_kernel, final=final),
        grid=(bx, lx // tm),
        in_specs=[
            pl.BlockSpec((None, tm, d), lambda b, i: (b, i, 0)),
            pl.BlockSpec((None, None, r, 3 * d), lambda b, i: (layer, b, 0, 1)),
            pl.BlockSpec((None, 1, d), lambda b, i: (layer, 0, 0)),
            pl.BlockSpec((None, d, D_FF), lambda b, i: (layer, 0, 0), pipeline_mode=resident),
            pl.BlockSpec((None, D_FF, d), lambda b, i: (layer, 0, 0), pipeline_mode=resident),
            pl.BlockSpec((1, d), lambda b, i: (0, 0)),
        ],
        out_specs=pl.BlockSpec((None, tm, d), lambda b, i: (b, i, 0)),
        out_shape=jax.ShapeDtypeStruct((bx, lx, d), F32),
        compiler_params=_params(("arbitrary", "arbitrary")),
        name="mlp",
    )(x, mod, norm_g, w_up, w_down, final_g)


def _dec_attn_kernel(*refs, n_alias):
    n = N_GROUPS
    proj_ref, c_refs, b_refs, bself_ref = refs[0], refs[1:1 + n], refs[1 + n:1 + 2 * n], refs[1 + 2 * n]
    o_ref, oc_refs = refs[2 + 2 * n + n_alias], refs[3 + 2 * n + n_alias:3 + 3 * n + n_alias]
    row = lax.broadcasted_iota(jnp.int32, (SUBLANES, GROUP_W), 0)
    lane = lax.broadcasted_iota(jnp.int32, (SUBLANES, GROUP_W), 1)
    hmask = jnp.logical_and(lane >= row * HEAD_DIM, lane < (row + 1) * HEAD_DIM)
    sel = jnp.logical_or(jnp.logical_and(row == 0, lane < LANES),
                         jnp.logical_and(row == 1, lane >= LANES)).astype(F32)
    last_lane = lax.broadcasted_iota(jnp.int32, (GROUP_W, LANES), 1) == LANES - 1
    parts = []
    for g in range(n):
        c_ref, oc_ref = c_refs[g], oc_refs[g]
        lb = c_ref.shape[-1]
        qv = proj_ref[:, OFF_Q + g * GROUP_W:OFF_Q + (g + 1) * GROUP_W]
        kn = proj_ref[:, OFF_K + g * GROUP_W:OFF_K + (g + 1) * GROUP_W]
        vn = proj_ref[:, OFF_V + g * GROUP_W:OFF_V + (g + 1) * GROUP_W]
        qbd = jnp.where(hmask, jnp.broadcast_to(qv, (SUBLANES, GROUP_W)), 0.0)
        kt = jnp.concatenate([c_ref[0, h] for h in range(HEADS_PER_GROUP)], axis=0)
        vt = jnp.concatenate([c_ref[1, h] for h in range(HEADS_PER_GROUP)], axis=0)
        s = jnp.dot(qbd.astype(BF16), kt.astype(BF16), preferred_element_type=F32)
        s = s * ATTN_SCALE + b_refs[g][...]
        s_self = jnp.sum(qbd * kn, axis=-1, keepdims=True) * ATTN_SCALE + bself_ref[g][:, 0:1]
        m = jnp.maximum(jnp.max(s, axis=-1, keepdims=True), s_self)
        p = jnp.exp(s - m)
        p_self = jnp.exp(s_self - m)
        l = jnp.sum(p, axis=-1, keepdims=True) + p_self
        acc = lax.dot_general(p.astype(BF16), vt.astype(BF16), NT_DIMS,
                              preferred_element_type=F32) + p_self * vn
        parts.append((m, l, acc))
        rows2 = jnp.where(row == 0, jnp.broadcast_to(kn, (SUBLANES, GROUP_W)),
                          jnp.where(row == 1, jnp.broadcast_to(vn, (SUBLANES, GROUP_W)), 0.0))
        cols = lax.dot_general(rows2, sel, TN_DIMS, precision=lax.Precision.HIGHEST,
                               preferred_element_type=F32)
        for kv, t in ((0, kt), (1, vt)):
            rolled = pltpu.roll(t, lb - 1, axis=1)
            tail = jnp.where(last_lane, cols[:, kv * LANES:(kv + 1) * LANES], rolled[:, lb - LANES:lb])
            new = tail if lb == LANES else jnp.concatenate([rolled[:, 0:lb - LANES], tail], axis=1)
            for h in range(HEADS_PER_GROUP):
                oc_ref[kv, h] = new[h * HEAD_DIM:(h + 1) * HEAD_DIM, :]
    m_all = functools.reduce(jnp.maximum, [p[0] for p in parts])
    num = sum(jnp.exp(m - m_all) * acc for m, _, acc in parts)
    den = sum(jnp.exp(m - m_all) * l for m, l, _ in parts)
    o_ref[...] = jnp.sum(jnp.where(hmask, num / den, 0.0), axis=0, keepdims=True)


def _decode_attention(proj_s, caches_t, bias_dec, bias_self, prev_out, layer):
    bs = proj_s.shape[1]
    n = N_GROUPS
    cspec = lambda c: pl.BlockSpec((None, None) + c.shape[2:], lambda b: (layer, b, 0, 0, 0, 0))
    in_specs = [pl.BlockSpec((None, 1, PROJ_W), lambda b: (b, 0, 0))]
    in_specs += [cspec(c) for c in caches_t]
    in_specs += [pl.BlockSpec(bd.shape, lambda b: (0, 0)) for bd in bias_dec]
    in_specs += [pl.BlockSpec(bias_self.shape, lambda b: (0, 0, 0))]
    args = [proj_s.reshape(bs, 1, PROJ_W), *caches_t, *bias_dec, bias_self]
    aliases = {}
    if prev_out is not None:
        in_specs += [pl.BlockSpec(memory_space=pl.ANY)] * n
        aliases = {len(args) + j: 1 + j for j in range(n)}
        args += list(prev_out)
    out = pl.pallas_call(
        functools.partial(_dec_attn_kernel, n_alias=len(aliases)),
        grid=(bs,),
        in_specs=in_specs,
        out_specs=[pl.BlockSpec((None, 1, GROUP_W), lambda b: (b, 0, 0))] + [cspec(c) for c in caches_t],
        out_shape=[jax.ShapeDtypeStruct((bs, 1, GROUP_W), F32)] + [
            jax.ShapeDtypeStruct(c.shape, c.dtype) for c in caches_t],
        input_output_aliases=aliases,
        compiler_params=_params(("arbitrary",)),
        name="decode_attn",
    )(*args)
    return out[0].reshape(1, bs, GROUP_W), out[1:]


def _dec_ssd_kernel(*refs, n_alias):
    proj_ref, cs_ref, st_ref, cw_ref, cb_ref, dtb_ref, alog_ref, dskip_ref, ng_ref = refs[:9]
    y_ref, nst_ref, ncs_ref, xc_scr, xdt_t_scr, da_t_scr, ct_scr, yt_scr = refs[9 + n_alias:]
    b = pl.program_id(0)
    bs = proj_ref.shape[0]
    nblk = D_INNER // LANES
    pad_rows = lambda v: jnp.concatenate([v, jnp.zeros((LANES - bs, v.shape[1]), v.dtype)], axis=0)

    @pl.when(b == 0)
    def _():
        xr = proj_ref[:, OFF_XBC:OFF_XBC + CONV_DIM]
        acc = cb_ref[...]
        for t in range(D_CONV - 1):
            acc = acc + cs_ref[t] * cw_ref[t:t + 1, :]
        acc = acc + xr * cw_ref[D_CONV - 1:D_CONV, :]
        xc = pad_rows(_silu(acc))
        xc_scr[...] = xc
        for t in range(D_CONV - 2):
            ncs_ref[t] = cs_ref[t + 1]
        ncs_ref[D_CONV - 2] = xr
        dt = _softplus(proj_ref[:, OFF_DT:OFF_DT + LANES] + dtb_ref[...])
        da = jnp.exp(dt * (-jnp.exp(alog_ref[...])))
        onehot = (lax.shift_right_logical(lax.broadcasted_iota(jnp.int32, (LANES, D_INNER), 1), 6)
                  == lax.broadcasted_iota(jnp.int32, (LANES, D_INNER), 0)).astype(F32)
        expand = lambda v: pad_rows(jnp.dot(v, onehot, precision=lax.Precision.HIGHEST,
                                            preferred_element_type=F32))
        xdt = xc[:, 0:D_INNER] * expand(dt)
        dae = expand(da)
        for k in range(nblk):
            ks = slice(k * LANES, (k + 1) * LANES)
            xdt_t_scr[ks, :] = xdt[:, ks].T.astype(BF16)
            da_t = dae[:, ks].T
            da_hi = da_t.astype(BF16)
            da_t_scr[ks, 0:LANES] = da_hi
            da_t_scr[ks, LANES:2 * LANES] = (da_t - da_hi.astype(F32)).astype(BF16)
        for g in range(SSD_GROUPS):
            ct_scr[g] = xc[:, D_INNER + (SSD_GROUPS + g) * D_STATE:D_INNER + (SSD_GROUPS + g + 1) * D_STATE].T
        yt_scr[...] = jnp.zeros(yt_scr.shape, F32)

    row = lax.broadcasted_iota(jnp.int32, (LANES, LANES), 0)
    lane = lax.broadcasted_iota(jnp.int32, (LANES, LANES), 1)
    pick = jnp.where(row == b, 1.0, 0.0).astype(BF16)
    decay = jnp.dot(da_t_scr[...], jnp.concatenate([pick, pick], axis=0), preferred_element_type=F32)
    for g in range(SSD_GROUPS):
        rs = slice(g * SSD_GROUP_W, (g + 1) * SSD_GROUP_W)
        bg = xc_scr[:, D_INNER + g * D_STATE:D_INNER + (g + 1) * D_STATE]
        b_sel = jnp.where(row == b, bg, 0.0).astype(BF16)
        hn = st_ref[rs, :] * decay[rs, :] + jnp.dot(xdt_t_scr[rs, :], b_sel, preferred_element_type=F32)
        nst_ref[rs, :] = hn
        c_sel = jnp.where(lane == b, ct_scr[g], 0.0).astype(BF16)
        yt_scr[rs, :] += jnp.dot(hn.astype(BF16), c_sel, preferred_element_type=F32)

    @pl.when(b == pl.num_programs(0) - 1)
    def _():
        for g in range(SSD_GROUPS):
            gs = slice(g * SSD_GROUP_W, (g + 1) * SSD_GROUP_W)
            yg = jnp.concatenate([yt_scr[k * LANES:(k + 1) * LANES, :].T[0:bs, :]
                                  for k in range(g * SSD_GROUP_W // LANES, (g + 1) * SSD_GROUP_W // LANES)],
                                 axis=1)
            y = yg + dskip_ref[:, gs] * xc_scr[0:bs, gs]
            hg = y * _silu(proj_ref[:, OFF_Z + g * SSD_GROUP_W:OFF_Z + (g + 1) * SSD_GROUP_W])
            y_ref[:, gs] = (_rms(hg) * ng_ref[:, gs]).astype(BF16)


def _decode_ssd(proj_s, conv_state_t, state, conv_w, conv_b, dt_bias, a_log, d_skip, norm_g, prev_state, layer):
    bs = proj_s.shape[1]
    vec = lambda w: pl.BlockSpec((None, 1, w), lambda b: (layer, 0, 0))
    st_spec = pl.BlockSpec((None, None, D_INNER, D_STATE), lambda b: (layer, b, 0, 0))
    args = [proj_s, conv_state_t, state, conv_w, conv_b, dt_bias, a_log, d_skip, norm_g]
    extra, aliases = [], {}
    if prev_state is not None:
        extra, aliases = [pl.BlockSpec(memory_space=pl.ANY)], {len(args): 1}
        args.append(prev_state)
    return pl.pallas_call(
        functools.partial(_dec_ssd_kernel, n_alias=len(aliases)),
        grid=(bs,),
        in_specs=[
            pl.BlockSpec((None, bs, PROJ_W), lambda b: (0, 0, 0)),
            pl.BlockSpec((None, D_CONV - 1, bs, CONV_DIM), lambda b: (layer, 0, 0, 0)),
            st_spec,
            pl.BlockSpec((None, D_CONV, CONV_DIM), lambda b: (layer, 0, 0)),
            vec(CONV_DIM), vec(LANES), vec(LANES), vec(D_INNER), vec(D_INNER),
        ] + extra,
        out_specs=[
            pl.BlockSpec((None, bs, D_INNER), lambda b: (0, 0, 0)),
            st_spec,
            pl.BlockSpec((D_CONV - 1, bs, CONV_DIM), lambda b: (0, 0, 0)),
        ],
        out_shape=[
            jax.ShapeDtypeStruct((1, bs, D_INNER), BF16),
            jax.ShapeDtypeStruct(state.shape, F32),
            jax.ShapeDtypeStruct((D_CONV - 1, bs, CONV_DIM), F32),
        ],
        input_output_aliases=aliases,
        scratch_shapes=[
            pltpu.VMEM((LANES, CONV_DIM), F32),
            pltpu.VMEM((D_INNER, LANES), BF16),
            pltpu.VMEM((D_INNER, 2 * LANES), BF16),
            pltpu.VMEM((SSD_GROUPS, D_STATE, LANES), F32),
            pltpu.VMEM((D_INNER, LANES), F32),
        ],
        compiler_params=_params(("arbitrary",)),
        name="decode_ssd",
    )(*args)


def _reorder_w_in(w_in):
    q0, z0, x0, d0, g0 = 0, 3 * QKV_W, 3 * QKV_W + D_INNER, 3 * QKV_W + D_INNER + CONV_DIM, \
        3 * QKV_W + D_INNER + CONV_DIM + SSD_HEADS
    wt = jnp.swapaxes(w_in, 1, 2)
    pieces = [wt[:, z0:x0], wt[:, g0:g0 + 2 * D_MODEL], wt[:, x0:d0], wt[:, q0:z0], wt[:, d0:g0],
              jnp.zeros((wt.shape[0], DT_W - SSD_HEADS, wt.shape[2]), wt.dtype)]
    return jnp.concatenate(pieces, axis=1).astype(BF16)


def _pad_lanes(v, width):
    return jnp.pad(v, ((0, 0), (0, width - v.shape[-1])))[:, None, :]


def kernel(x_prompt, x_sample, cache_kv_g0, cache_kv_g1, cache_kv_g2, state_ssm, state_conv, c_prompt,
           c_sample, rel_bias, w_ada, b_ada, norm1_g, norm2_g, w_in, conv_w, conv_b, dt_bias, a_log, d_skip,
           ssd_norm_g, w_o_attn, w_o_ssd, w_out, w_up, w_down, final_g):
    depth = w_in.shape[0]
    b, l, d = x_prompt.shape
    bs = x_sample.shape[0]
    caches = (cache_kv_g0, cache_kv_g1, cache_kv_g2)
    assert d == D_MODEL and x_sample.shape[1] == 1 and l % (DIL_GROUPS[-1][1] * BAND) == 0
    assert all(win == BAND * dil for win, dil in DIL_GROUPS)
    assert all(c.shape[2] == win for c, (win, _) in zip(caches, DIL_GROUPS))
    assert bs % SUBLANES == 0 and bs <= LANES

    w_in_r = _reorder_w_in(w_in)
    w_oa, w_os, w_o, w_u, w_d = (w.astype(BF16) for w in (w_o_attn, w_o_ssd, w_out, w_up, w_down))
    n1, n2 = norm1_g[:, None, :], norm2_g[:, None, :]
    conv_b3, ng3 = conv_b[:, None, :], ssd_norm_g[:, None, :]
    dtb3, alog3 = _pad_lanes(dt_bias, LANES), _pad_lanes(a_log, LANES)
    dskip3 = jnp.repeat(d_skip, SSD_P, axis=-1)[:, None, :]
    fg = final_g[None, :]

    rows = b + bs
    rows_pad = -(-rows // SUBLANES) * SUBLANES
    c_all = jnp.pad(jnp.concatenate([c_prompt, c_sample], axis=0), ((0, rows_pad - rows), (0, 0)))
    mod = _ada_mod(c_all, w_ada, b_ada)
    mod_p = mod[:, :b, None, :]
    mod_s = mod[:, None, b:rows, :]

    bias = _bias_lookup(rel_bias, _bucket_tiles())
    head_rows = ((0, SUBLANES - HEADS_PER_GROUP), (0, 0))
    bias_rows = _bias_lookup(rel_bias, _bucket_rows())[:, 0, :]
    bias_dec = [jnp.pad(bias_rows[HEADS_PER_GROUP * g:HEADS_PER_GROUP * (g + 1), :win], head_rows)
                for g, (win, _) in enumerate(DIL_GROUPS)]
    self_bias = bias[:, 0, BAND].reshape(N_GROUPS, HEADS_PER_GROUP, 1)
    bias_self = jnp.pad(jnp.broadcast_to(self_bias, (N_GROUPS, HEADS_PER_GROUP, LANES)), ((0, 0),) + head_rows)

    xp = x_prompt
    xs = x_sample.reshape(1, bs, d)
    conv_state_t = jnp.swapaxes(state_conv, 1, 2)
    state_r = state_ssm.reshape(depth, bs, D_INNER, D_STATE)
    caches_t = [jnp.transpose(c, (0, 1, 3, 4, 5, 2)) for c in caches]
    tm_p = 1024 if l % 1024 == 0 else BAND

    kv_p = [[] for _ in DIL_GROUPS]
    ssm_p, conv_p, conv_s = [], [], []
    kv_s, st_s = None, None
    for layer in range(depth):
        final = layer == depth - 1
        proj = _in_proj(xp, mod_p, n1, w_in_r, layer, tm_p)
        attn = [a for g in range(N_GROUPS) for a in _prompt_attention(proj, bias, g)]
        ssd, st, cs = _prompt_ssd(proj, conv_w, conv_b3, dtb3, alog3, dskip3, ng3, layer)
        x1 = _out_proj(attn, ssd, proj, xp, mod_p, w_oa, w_os, w_o, layer, min(512, l))
        xp = _mlp(x1, mod_p, n2, w_u, w_d, fg, layer, tm_p, final)
        for g, (win, _) in enumerate(DIL_GROUPS):
            keep = min(win, l)
            kk = proj[:, l - keep:, OFF_K + g * GROUP_W:OFF_K + (g + 1) * GROUP_W]
            vv = proj[:, l - keep:, OFF_V + g * GROUP_W:OFF_V + (g + 1) * GROUP_W]
            kv_p[g].append(jnp.stack([kk, vv], axis=2).reshape(b, keep, 2, HEADS_PER_GROUP, HEAD_DIM))
        ssm_p.append(st.reshape(b, SSD_HEADS, SSD_P, D_STATE))
        conv_p.append(cs)
        proj_s = _in_proj(xs, mod_s, n1, w_in_r, layer, bs)
        attn_s, kv_s = _decode_attention(proj_s, caches_t, bias_dec, bias_self, kv_s, layer)
        ssd_s, st_s, cs_s = _decode_ssd(proj_s, conv_state_t, state_r, conv_w, conv_b3, dtb3, alog3,
                                        dskip3, ng3, st_s, layer)
        x1s = _out_proj([attn_s], ssd_s, proj_s, xs, mod_s, w_oa, w_os, w_o, layer, bs)
        xs = _mlp(x1s, mod_s, n2, w_u, w_d, fg, layer, bs, final)
        conv_s.append(jnp.swapaxes(cs_s, 0, 1))

    kv_s = [jnp.transpose(o, (0, 1, 5, 2, 3, 4)) for o in kv_s]
    return (xp, xs.reshape(bs, 1, d),
            jnp.stack(kv_p[0], axis=0), jnp.stack(kv_p[1], axis=0), jnp.stack(kv_p[2], axis=0),
            jnp.stack(ssm_p, axis=0), jnp.stack(conv_p, axis=0),
            kv_s[0], kv_s[1], kv_s[2],
            st_s.reshape(depth, bs, SSD_HEADS, SSD_P, D_STATE), jnp.stack(conv_s, axis=0))
```

```python
import functools
import math

import jax
import jax.numpy as jnp
import numpy as np
from jax import lax
from jax.experimental import pallas as pl
from jax.experimental.pallas import tpu as pltpu

F32 = jnp.float32
BF16 = jnp.bfloat16

D_MODEL = 1024
HEAD_DIM = 64
HEADS_PER_GROUP = 4
DIL_GROUPS = ((128, 1), (512, 4), (2048, 16))
N_GROUPS = len(DIL_GROUPS)
GROUP_W = HEADS_PER_GROUP * HEAD_DIM
QKV_W = N_GROUPS * GROUP_W
BAND = 128
NUM_BUCKETS = 32
MAX_DISTANCE = 2048
D_INNER = 2 * D_MODEL
SSD_HEADS = 32
SSD_P = 64
SSD_GROUPS = 8
SSD_GROUP_W = D_INNER // SSD_GROUPS
D_STATE = 128
D_CONV = 4
CONV_DIM = D_INNER + 2 * SSD_GROUPS * D_STATE
D_FF = 4 * D_MODEL
EPS = 1e-6
ATTN_SCALE = HEAD_DIM ** -0.5
NEG = -1e30
LOG2_E = 1.4426950408889634

SUBLANES = 8
LANES = 128
V7X_VMEM_LIMIT = 52 * 1024 * 1024

OFF_Z = 0
OFF_GATE = D_INNER
OFF_XBC = 2 * D_INNER
OFF_Q = OFF_XBC + CONV_DIM
OFF_K = OFF_Q + QKV_W
OFF_V = OFF_K + QKV_W
OFF_DT = OFF_V + QKV_W
DT_W = 2 * LANES
PROJ_W = OFF_DT + DT_W

NT_DIMS = (((1,), (1,)), ((), ()))
TN_DIMS = (((0,), (0,)), ((), ()))


def _params(semantics, vmem=V7X_VMEM_LIMIT):
    return pltpu.CompilerParams(dimension_semantics=semantics, vmem_limit_bytes=vmem)


def _sigmoid(x):
    return 0.5 * (jnp.tanh(0.5 * x) + 1.0)


def _silu(x):
    h = 0.5 * x
    return h + h * jnp.tanh(h)


def _softplus(x):
    return jnp.maximum(x, 0.0) + jnp.log(1.0 + jnp.exp(-jnp.abs(x)))


def _rms(x):
    return x * lax.rsqrt(jnp.mean(x * x, axis=-1, keepdims=True) + EPS)


def _ada_kernel(c_ref, w_ref, b_ref, o_ref):
    s = _silu(c_ref[...]).astype(BF16)
    o_ref[...] = jnp.dot(s, w_ref[...].astype(BF16), preferred_element_type=F32) + b_ref[...]


def _ada_mod(c_all, w_ada, b_ada):
    depth, d, n = w_ada.shape
    rows = c_all.shape[0]
    tn = 1536
    return pl.pallas_call(
        _ada_kernel,
        grid=(depth, n // tn),
        in_specs=[
            pl.BlockSpec((rows, d), lambda l, j: (0, 0)),
            pl.BlockSpec((None, d, tn), lambda l, j: (l, 0, j)),
            pl.BlockSpec((None, 1, tn), lambda l, j: (l, 0, j)),
        ],
        out_specs=pl.BlockSpec((None, rows, tn), lambda l, j: (l, 0, j)),
        out_shape=jax.ShapeDtypeStruct((depth, rows, n), F32),
        compiler_params=_params(("arbitrary", "arbitrary")),
        name="ada_mod",
    )(c_all, w_ada, b_ada.reshape(depth, 1, n))


IN_PROJ_TN = 1792


def _in_proj_kernel(x_ref, mod_ref, g_ref, w_ref, o_ref, h_scr, *, tn):
    n = pl.program_id(2)

    @pl.when(n == 0)
    def _():
        y = _rms(x_ref[...]) * g_ref[...]
        h = y * (1.0 + mod_ref[:, D_MODEL:2 * D_MODEL]) + mod_ref[:, 0:D_MODEL]
        h_scr[...] = h.astype(BF16)

    w = w_ref[pl.ds(pl.multiple_of(n * tn, tn), tn), :]
    o_ref[...] = lax.dot_general(h_scr[...], w, NT_DIMS, preferred_element_type=F32)


def _in_proj(x, mod, norm_g, w_in_r, layer, tm):
    bx, lx, d = x.shape
    r = mod.shape[2]
    tn = IN_PROJ_TN
    return pl.pallas_call(
        functools.partial(_in_proj_kernel, tn=tn),
        grid=(bx, lx // tm, PROJ_W // tn),
        in_specs=[
            pl.BlockSpec((None, tm, d), lambda b, i, n: (b, i, 0)),
            pl.BlockSpec((None, None, r, 2 * d), lambda b, i, n: (layer, b, 0, 0)),
            pl.BlockSpec((None, 1, d), lambda b, i, n: (layer, 0, 0)),
            pl.BlockSpec((None, PROJ_W, d), lambda b, i, n: (layer, 0, 0), pipeline_mode=pl.Buffered(1)),
        ],
        out_specs=pl.BlockSpec((None, tm, tn), lambda b, i, n: (b, i, n)),
        out_shape=jax.ShapeDtypeStruct((bx, lx, PROJ_W), F32),
        scratch_shapes=[pltpu.VMEM((tm, d), BF16)],
        compiler_params=_params(("arbitrary", "arbitrary", "arbitrary")),
        name="in_proj",
    )(x, mod, norm_g, w_in_r)


def _t5_bucket_np(dist):
    max_exact = NUM_BUCKETS // 2
    df = np.maximum(dist, 1).astype(np.float32)
    ratio = np.log(df / np.float32(max_exact)) / np.float32(math.log(MAX_DISTANCE / max_exact))
    large = max_exact + (ratio * np.float32(NUM_BUCKETS - max_exact)).astype(np.int32)
    large = np.minimum(large, NUM_BUCKETS - 1)
    return np.where(dist < max_exact, dist, large).astype(np.int32)


def _bucket_tiles():
    q = np.arange(BAND)[:, None]
    c = np.arange(2 * BAND)[None, :]
    j = q + BAND - c
    valid = (j >= 0) & (j <= BAND)
    tiles = []
    for _, dil in DIL_GROUPS:
        b = _t5_bucket_np(np.clip(j, 0, BAND) * dil)
        tiles.append(np.where(valid, b, -1))
    return np.stack(tiles).astype(np.int32)


def _bucket_rows():
    width = max(win for win, _ in DIL_GROUPS)
    i = np.arange(width)
    rows = []
    for win, dil in DIL_GROUPS:
        dist = win - i
        valid = (i < win) & (dist % dil == 0)
        rows.append(np.where(valid, _t5_bucket_np(np.clip(dist, 0, win)), -1))
    return np.broadcast_to(np.stack(rows)[:, None, :], (N_GROUPS, SUBLANES, width)).astype(np.int32)


def _bias_kernel(rb_ref, bk_ref, o_ref):
    hh = pl.program_id(0)
    bk = bk_ref[...]
    acc = jnp.full(bk.shape, NEG, F32)
    for b in range(NUM_BUCKETS):
        acc = jnp.where(bk == b, rb_ref[b, hh], acc)
    o_ref[...] = acc


def _bias_lookup(rel_bias, buckets):
    n_heads = rel_bias.shape[1]
    blk = (None,) + buckets.shape[1:]
    return pl.pallas_call(
        _bias_kernel,
        grid=(n_heads,),
        in_specs=[
            pl.BlockSpec(memory_space=pltpu.SMEM),
            pl.BlockSpec(blk, lambda h: (h // HEADS_PER_GROUP, 0, 0)),
        ],
        out_specs=pl.BlockSpec(blk, lambda h: (h, 0, 0)),
        out_shape=jax.ShapeDtypeStruct((n_heads,) + buckets.shape[1:], F32),
        compiler_params=_params(("arbitrary",)),
        name="bias_lookup",
    )(rel_bias, jnp.asarray(buckets))


HEADS_PER_TILE = LANES // HEAD_DIM
ATTN_ROWS = BAND * max(dil for _, dil in DIL_GROUPS)
ATTN_TILES_IN_FLIGHT = 8
ATTN_MERGE_ROWS = 256


def _attn_kernel(*refs):
    n = N_GROUPS
    bias_ref, o_ref, og_scr, lse_scr = refs[5 * n:]
    first = pl.program_id(1) == 0
    head0 = pl.program_id(2) * HEADS_PER_TILE
    col = lax.broadcasted_iota(jnp.int32, (BAND, 2 * BAND), 1)
    pen_first = jnp.where(col < BAND, jnp.where(first, NEG, 0.0), 0.0)
    lane = lax.broadcasted_iota(jnp.int32, (BAND, LANES), 1)

    for g, (_, dil) in enumerate(DIL_GROUPS):
        q_ref, kp_ref, kc_ref, vp_ref, vc_ref = refs[5 * g:5 * g + 5]
        period = BAND * dil
        periods = ATTN_ROWS // period

        def rows_of(start, dil=dil):
            if dil > 1:
                return pl.ds(start, BAND, stride=dil)
            return pl.ds(pl.multiple_of(start, BAND), BAND)

        def tile(rows, kprev, vprev, leading, g=g, q_ref=q_ref, kc_ref=kc_ref, vc_ref=vc_ref):
            q = q_ref[rows, :]
            kb = jnp.concatenate([kprev, kc_ref[rows, :]], axis=0).astype(BF16)
            vb = jnp.concatenate([vprev, vc_ref[rows, :]], axis=0).astype(BF16)
            o, lse = None, None
            for h in range(HEADS_PER_TILE):
                mine = jnp.logical_and(lane >= h * HEAD_DIM, lane < (h + 1) * HEAD_DIM)
                qh = jnp.where(mine, q, 0.0).astype(BF16)
                s = lax.dot_general(qh, kb, NT_DIMS, preferred_element_type=F32)
                s = s * ATTN_SCALE + bias_ref[HEADS_PER_GROUP * g + head0 + h]
                if leading:
                    s = s + pen_first
                m = jnp.max(s, axis=-1, keepdims=True)
                p = jnp.exp(s - m)
                l = jnp.sum(p, axis=-1, keepdims=True)
                oh = jnp.dot(p.astype(BF16), vb, preferred_element_type=F32) / l
                lh = jnp.broadcast_to(m + jnp.log(l), (BAND, LANES))
                o = oh if h == 0 else jnp.where(mine, oh, o)
                lse = lh if h == 0 else jnp.where(mine, lh, lse)
            og_scr[g, rows, :] = o
            lse_scr[g, rows, :] = lse

        def lead(r, carry, rows_of=rows_of, tile=tile, kp_ref=kp_ref, vp_ref=vp_ref):
            rows = rows_of(r)
            tile(rows, kp_ref[rows, :], vp_ref[rows, :], True)
            return carry

        def body(j, carry, rows_of=rows_of, tile=tile, kc_ref=kc_ref, vc_ref=vc_ref, dil=dil, period=period):
            r, tau = j % dil, 1 + j // dil
            above = rows_of(r + (tau - 1) * period)
            tile(rows_of(r + tau * period), kc_ref[above, :], vc_ref[above, :], False)
            return carry

        lax.fori_loop(0, dil, lead, 0, unroll=min(dil, ATTN_TILES_IN_FLIGHT))
        n_body = dil * (periods - 1)
        if n_body:
            unroll = max(u for u in range(1, ATTN_TILES_IN_FLIGHT + 1) if n_body % u == 0)
            lax.fori_loop(0, n_body, body, 0, unroll=unroll)

    def merge(c, carry):
        rs = pl.ds(pl.multiple_of(c * ATTN_MERGE_ROWS, ATTN_MERGE_ROWS), ATTN_MERGE_ROWS)
        lse = [lse_scr[g, rs, :] for g in range(n)]
        m = functools.reduce(jnp.maximum, lse)
        e = [jnp.exp(v - m) for v in lse]
        o_ref[rs, :] = sum(e[g] * og_scr[g, rs, :] for g in range(n)) / sum(e)
        return carry

    lax.fori_loop(0, ATTN_ROWS // ATTN_MERGE_ROWS, merge, 0)


def _prompt_attention(proj, bias):
    b, l, _ = proj.shape
    tiles = GROUP_W // LANES
    in_specs = []
    for g, (_, dil) in enumerate(DIL_GROUPS):
        period = BAND * dil
        qb, kb, vb = (off // LANES + g * tiles for off in (OFF_Q, OFF_K, OFF_V))
        cur = lambda cb: pl.BlockSpec((None, ATTN_ROWS, LANES), lambda bb, i, t, cb=cb: (bb, i, cb + t))
        prev = lambda cb, per=ATTN_ROWS // period, period=period: pl.BlockSpec(
            (None, period, LANES), lambda bb, i, t: (bb, jnp.maximum(i * per - 1, 0), cb + t))
        in_specs += [cur(qb), prev(kb), cur(kb), prev(vb), cur(vb)]
    in_specs.append(pl.BlockSpec(bias.shape, lambda bb, i, t: (0, 0, 0), pipeline_mode=pl.Buffered(1)))
    return pl.pallas_call(
        _attn_kernel,
        grid=(b, l // ATTN_ROWS, tiles),
        in_specs=in_specs,
        out_specs=pl.BlockSpec((None, ATTN_ROWS, LANES), lambda bb, i, t: (bb, i, t)),
        out_shape=jax.ShapeDtypeStruct((b, l, GROUP_W), F32),
        scratch_shapes=[pltpu.VMEM((N_GROUPS, ATTN_ROWS, LANES), F32)] * 2,
        compiler_params=_params(("arbitrary", "arbitrary", "arbitrary")),
        name="prompt_attn",
    )(*([proj] * (5 * N_GROUPS)), bias)


def _expand4(arr, g, lane):
    rows = arr.shape[0]
    c = [jnp.broadcast_to(arr[:, 4 * g + e:4 * g + e + 1], (rows, SSD_GROUP_W)) for e in range(4)]
    return jnp.where(lane < SSD_P, c[0], jnp.where(lane < 2 * SSD_P, c[1],
                                                    jnp.where(lane < 3 * SSD_P, c[2], c[3])))


def _cumsum_rows(a):
    row = lax.broadcasted_iota(jnp.int32, a.shape, 0)
    s = 1
    while s < a.shape[0]:
        a = a + jnp.where(row >= s, pltpu.roll(a, s, axis=0), 0.0)
        s *= 2
    return a


def _ssd_kernel(xbc_ref, dt_ref, z_ref, cw_ref, cb_ref, dtb_ref, alog_ref, dskip_ref, ng_ref,
                y_ref, st_ref, cs_ref, xp_scr, xc_scr, stt_scr):
    q = BAND
    c = pl.program_id(1)
    last = c == pl.num_programs(1) - 1

    @pl.when(c == 0)
    def _():
        xp_scr[0:SUBLANES, :] = jnp.zeros((SUBLANES, CONV_DIM), F32)
        stt_scr[...] = jnp.zeros(stt_scr.shape, F32)

    xp_scr[SUBLANES:SUBLANES + q, :] = xbc_ref[...]
    cblk = 512
    nt = q // SUBLANES
    sub = lax.broadcasted_iota(jnp.int32, (nt, SUBLANES, cblk), 1)
    for j in range(CONV_DIM // cblk):
        cs = slice(j * cblk, (j + 1) * cblk)
        x3 = xp_scr[:, cs].reshape(nt + 1, SUBLANES, cblk)
        acc = cb_ref[:, cs] + x3[1:] * cw_ref[D_CONV - 1:D_CONV, cs]
        for k in range(1, D_CONV):
            rot = pltpu.roll(x3, k, axis=1)
            back = jnp.where(sub < k, rot[:-1], rot[1:])
            acc = acc + back * cw_ref[D_CONV - 1 - k:D_CONV - k, cs]
        xc_scr[:, cs] = _silu(acc).reshape(q, cblk)
    xp_scr[0:SUBLANES, :] = xp_scr[q:q + SUBLANES, :]

    dt = _softplus(dt_ref[:, 0:LANES] + dtb_ref[...])
    a = dt * (-jnp.exp(alog_ref[...]))
    acs = _cumsum_rows(a) * LOG2_E
    acs_t = acs.T
    dt_t = dt.T
    ea_last = jnp.exp2(acs[q - 1:q, :])
    w_end_t = dt_t * jnp.exp2(jnp.broadcast_to(acs_t[:, q - 1:q], (q, q)) - acs_t)
    src_t = acs_t - jnp.log2(dt_t)

    row = lax.broadcasted_iota(jnp.int32, (q, q), 0)
    colq = lax.broadcasted_iota(jnp.int32, (q, q), 1)
    tril = row >= colq
    lane = lax.broadcasted_iota(jnp.int32, (q, SSD_GROUP_W), 1)
    lane1 = lax.broadcasted_iota(jnp.int32, (1, SSD_GROUP_W), 1)

    for g in range(SSD_GROUPS):
        gs = slice(g * SSD_GROUP_W, (g + 1) * SSD_GROUP_W)
        bg = xc_scr[:, D_INNER + g * D_STATE:D_INNER + (g + 1) * D_STATE]
        cg = xc_scr[:, D_INNER + (SSD_GROUPS + g) * D_STATE:D_INNER + (SSD_GROUPS + g + 1) * D_STATE]
        cbm = lax.dot_general(cg.astype(BF16), bg.astype(BF16), NT_DIMS, preferred_element_type=F32)
        bg_t = bg.T
        xg = xc_scr[:, gs]
        xgb = xg.astype(BF16)
        stg = stt_scr[:, gs]
        rhs = jnp.concatenate([xgb, stg.astype(BF16)], axis=0)
        y, snew = None, None
        for e in range(4):
            h = 4 * g + e
            col_h = jnp.broadcast_to(acs[:, h:h + 1], (q, q))
            row_h = lambda v: jnp.broadcast_to(v[h:h + 1, :], (q, q))
            intra = cbm * jnp.exp2(jnp.where(tril, col_h - row_h(src_t), NEG))
            inter = cg * jnp.exp2(col_h)
            r = jnp.dot(jnp.concatenate([intra, inter], axis=1).astype(BF16), rhs,
                        preferred_element_type=F32)
            sr = jnp.dot((bg_t * row_h(w_end_t)).astype(BF16), xgb, preferred_element_type=F32)
            mine = lane >= e * SSD_P
            y = r if e == 0 else jnp.where(mine, r, y)
            snew = sr if e == 0 else jnp.where(mine, sr, snew)
        y = y + dskip_ref[:, gs] * xg
        stt_scr[:, gs] = stg * _expand4(ea_last, g, lane1) + snew
        hg = y * _silu(z_ref[:, gs])
        y_ref[:, gs] = (_rms(hg) * ng_ref[:, gs]).astype(BF16)

    @pl.when(last)
    def _():
        cs_ref[...] = xp_scr[SUBLANES + q - (D_CONV - 1):SUBLANES + q, :]
        for k in range(D_INNER // LANES):
            st_ref[k * LANES:(k + 1) * LANES, :] = stt_scr[:, k * LANES:(k + 1) * LANES].T


def _prompt_ssd(proj, conv_w, conv_b, dt_bias, a_log, d_skip, norm_g, layer):
    b, l, _ = proj.shape
    q = BAND
    vec = lambda w: pl.BlockSpec((None, 1, w), lambda bb, c: (layer, 0, 0))
    return pl.pallas_call(
        _ssd_kernel,
        grid=(b, l // q),
        in_specs=[
            pl.BlockSpec((None, q, CONV_DIM), lambda bb, c: (bb, c, OFF_XBC // CONV_DIM)),
            pl.BlockSpec((None, q, DT_W), lambda bb, c: (bb, c, OFF_DT // DT_W)),
            pl.BlockSpec((None, q, D_INNER), lambda bb, c: (bb, c, OFF_Z // D_INNER)),
            pl.BlockSpec((None, D_CONV, CONV_DIM), lambda bb, c: (layer, 0, 0)),
            vec(CONV_DIM), vec(LANES), vec(LANES), vec(D_INNER), vec(D_INNER),
        ],
        out_specs=[
            pl.BlockSpec((None, q, D_INNER), lambda bb, c: (bb, c, 0)),
            pl.BlockSpec((None, D_INNER, D_STATE), lambda bb, c: (bb, 0, 0)),
            pl.BlockSpec((None, D_CONV - 1, CONV_DIM), lambda bb, c: (bb, 0, 0)),
        ],
        out_shape=[
            jax.ShapeDtypeStruct((b, l, D_INNER), BF16),
            jax.ShapeDtypeStruct((b, D_INNER, D_STATE), F32),
            jax.ShapeDtypeStruct((b, D_CONV - 1, CONV_DIM), F32),
        ],
        scratch_shapes=[
            pltpu.VMEM((q + SUBLANES, CONV_DIM), F32),
            pltpu.VMEM((q, CONV_DIM), F32),
            pltpu.VMEM((D_STATE, D_INNER), F32),
        ],
        compiler_params=_params(("arbitrary", "arbitrary")),
        name="prompt_ssd",
    )(proj, proj, proj, conv_w, conv_b, dt_bias, a_log, d_skip, norm_g)


def _out_proj_kernel(attn_ref, ssd_ref, gate_ref, x_ref, g1_ref, wa_ref, ws_ref, wo_ref, o_ref):
    pa = jnp.dot(attn_ref[...].astype(BF16), wa_ref[...], preferred_element_type=F32)
    ps = jnp.dot(ssd_ref[...], ws_ref[...], preferred_element_type=F32)
    merged = _sigmoid(gate_ref[:, 0:D_MODEL]) * pa + _sigmoid(gate_ref[:, D_MODEL:2 * D_MODEL]) * ps
    o_ref[...] = x_ref[...] + g1_ref[...] * jnp.dot(
        merged.astype(BF16), wo_ref[...], preferred_element_type=F32)


def _out_proj(attn, ssd, proj, x, mod, w_o_attn, w_o_ssd, w_out, layer, tm):
    bx, lx, d = x.shape
    r = mod.shape[2]
    row = lambda w, cb=0: pl.BlockSpec((None, tm, w), lambda b, i: (b, i, cb))
    wgt = lambda k, n: pl.BlockSpec((None, k, n), lambda b, i: (layer, 0, 0), pipeline_mode=pl.Buffered(1))
    return pl.pallas_call(
        _out_proj_kernel,
        grid=(bx, lx // tm),
        in_specs=[
            row(GROUP_W), row(D_INNER), row(2 * D_MODEL, OFF_GATE // (2 * D_MODEL)), row(d),
            pl.BlockSpec((None, None, r, d), lambda b, i: (layer, b, 0, 2)),
            wgt(GROUP_W, d), wgt(D_INNER, d), wgt(d, d),
        ],
        out_specs=row(d),
        out_shape=jax.ShapeDtypeStruct((bx, lx, d), F32),
        compiler_params=_params(("arbitrary", "arbitrary")),
        name="out_proj",
    )(attn, ssd, proj, x, mod, w_o_attn, w_o_ssd, w_out)


MLP_FF_TILE = 1024


def _mlp_kernel(x_ref, mod_ref, g_ref, wu_ref, wd_ref, fg_ref, o_ref, *, final):
    x = x_ref[...]
    h = (_rms(x) * g_ref[...]) * (1.0 + mod_ref[:, D_MODEL:2 * D_MODEL]) + mod_ref[:, 0:D_MODEL]
    hb = h.astype(BF16)
    acc = None
    for f in range(D_FF // MLP_FF_TILE):
        fs = slice(f * MLP_FF_TILE, (f + 1) * MLP_FF_TILE)
        u = jnp.maximum(jnp.dot(hb, wu_ref[:, fs], preferred_element_type=F32), 0.0)
        part = jnp.dot((u * u).astype(BF16), wd_ref[fs, :], preferred_element_type=F32)
        acc = part if f == 0 else acc + part
    x2 = x + mod_ref[:, 2 * D_MODEL:3 * D_MODEL] * acc
    if final:
        x2 = _rms(x2) * fg_ref[...]
    o_ref[...] = x2


def _mlp(x, mod, norm_g, w_up, w_down, final_g, layer, tm, final):
    bx, lx, d = x.shape
    r = mod.shape[2]
    resident = pl.Buffered(1)
    return pl.pallas_call(
        functools.partial(_mlp_kernel, final=final),
        grid=(bx, lx // tm),
        in_specs=[
            pl.BlockSpec((None, tm, d), lambda b, i: (b, i, 0)),
            pl.BlockSpec((None, None, r, 3 * d), lambda b, i: (layer, b, 0, 1)),
            pl.BlockSpec((None, 1, d), lambda b, i: (layer, 0, 0)),
            pl.BlockSpec((None, d, D_FF), lambda b, i: (layer, 0, 0), pipeline_mode=resident),
            pl.BlockSpec((None, D_FF, d), lambda b, i: (layer, 0, 0), pipeline_mode=resident),
            pl.BlockSpec((1, d), lambda b, i: (0, 0)),
        ],
        out_specs=pl.BlockSpec((None, tm, d), lambda b, i: (b, i, 0)),
        out_shape=jax.ShapeDtypeStruct((bx, lx, d), F32),
        compiler_params=_params(("arbitrary", "arbitrary")),
        name="mlp",
    )(x, mod, norm_g, w_up, w_down, final_g)


def _dec_attn_kernel(*refs, n_alias):
    n = N_GROUPS
    proj_ref, c_refs, b_refs, bself_ref = refs[0], refs[1:1 + n], refs[1 + n:1 + 2 * n], refs[1 + 2 * n]
    o_ref, oc_refs = refs[2 + 2 * n + n_alias], refs[3 + 2 * n + n_alias:3 + 3 * n + n_alias]
    row = lax.broadcasted_iota(jnp.int32, (SUBLANES, GROUP_W), 0)
    lane = lax.broadcasted_iota(jnp.int32, (SUBLANES, GROUP_W), 1)
    hmask = jnp.logical_and(lane >= row * HEAD_DIM, lane < (row + 1) * HEAD_DIM)
    sel = jnp.logical_or(jnp.logical_and(row == 0, lane < LANES),
                         jnp.logical_and(row == 1, lane >= LANES)).astype(F32)
    last_lane = lax.broadcasted_iota(jnp.int32, (GROUP_W, LANES), 1) == LANES - 1
    parts = []
    for g in range(n):
        c_ref, oc_ref = c_refs[g], oc_refs[g]
        lb = c_ref.shape[-1]
        qv = proj_ref[:, OFF_Q + g * GROUP_W:OFF_Q + (g + 1) * GROUP_W]
        kn = proj_ref[:, OFF_K + g * GROUP_W:OFF_K + (g + 1) * GROUP_W]
        vn = proj_ref[:, OFF_V + g * GROUP_W:OFF_V + (g + 1) * GROUP_W]
        qbd = jnp.where(hmask, jnp.broadcast_to(qv, (SUBLANES, GROUP_W)), 0.0)
        kt = jnp.concatenate([c_ref[0, h] for h in range(HEADS_PER_GROUP)], axis=0)
        vt = jnp.concatenate([c_ref[1, h] for h in range(HEADS_PER_GROUP)], axis=0)
        s = jnp.dot(qbd.astype(BF16), kt.astype(BF16), preferred_element_type=F32)
        s = s * ATTN_SCALE + b_refs[g][...]
        s_self = jnp.sum(qbd * kn, axis=-1, keepdims=True) * ATTN_SCALE + bself_ref[g][:, 0:1]
        m = jnp.maximum(jnp.max(s, axis=-1, keepdims=True), s_self)
        p = jnp.exp(s - m)
        p_self = jnp.exp(s_self - m)
        l = jnp.sum(p, axis=-1, keepdims=True) + p_self
        acc = lax.dot_general(p.astype(BF16), vt.astype(BF16), NT_DIMS,
                              preferred_element_type=F32) + p_self * vn
        parts.append((m, l, acc))
        rows2 = jnp.where(row == 0, jnp.broadcast_to(kn, (SUBLANES, GROUP_W)),
                          jnp.where(row == 1, jnp.broadcast_to(vn, (SUBLANES, GROUP_W)), 0.0))
        cols = lax.dot_general(rows2, sel, TN_DIMS, precision=lax.Precision.HIGHEST,
                               preferred_element_type=F32)
        for kv, t in ((0, kt), (1, vt)):
            rolled = pltpu.roll(t, lb - 1, axis=1)
            tail = jnp.where(last_lane, cols[:, kv * LANES:(kv + 1) * LANES], rolled[:, lb - LANES:lb])
            new = tail if lb == LANES else jnp.concatenate([rolled[:, 0:lb - LANES], tail], axis=1)
            for h in range(HEADS_PER_GROUP):
                oc_ref[kv, h] = new[h * HEAD_DIM:(h + 1) * HEAD_DIM, :]
    m_all = functools.reduce(jnp.maximum, [p[0] for p in parts])
    num = sum(jnp.exp(m - m_all) * acc for m, _, acc in parts)
    den = sum(jnp.exp(m - m_all) * l for m, l, _ in parts)
    o_ref[...] = jnp.sum(jnp.where(hmask, num / den, 0.0), axis=0, keepdims=True)


def _decode_attention(proj_s, caches_t, bias_dec, bias_self, prev_out, layer):
    bs = proj_s.shape[1]
    n = N_GROUPS
    cspec = lambda c: pl.BlockSpec((None, None) + c.shape[2:], lambda b: (layer, b, 0, 0, 0, 0))
    in_specs = [pl.BlockSpec((None, 1, PROJ_W), lambda b: (b, 0, 0))]
    in_specs += [cspec(c) for c in caches_t]
    in_specs += [pl.BlockSpec(bd.shape, lambda b: (0, 0)) for bd in bias_dec]
    in_specs += [pl.BlockSpec(bias_self.shape, lambda b: (0, 0, 0))]
    args = [proj_s.reshape(bs, 1, PROJ_W), *caches_t, *bias_dec, bias_self]
    aliases = {}
    if prev_out is not None:
        in_specs += [pl.BlockSpec(memory_space=pl.ANY)] * n
        aliases = {len(args) + j: 1 + j for j in range(n)}
        args += list(prev_out)
    out = pl.pallas_call(
        functools.partial(_dec_attn_kernel, n_alias=len(aliases)),
        grid=(bs,),
        in_specs=in_specs,
        out_specs=[pl.BlockSpec((None, 1, GROUP_W), lambda b: (b, 0, 0))] + [cspec(c) for c in caches_t],
        out_shape=[jax.ShapeDtypeStruct((bs, 1, GROUP_W), F32)] + [
            jax.ShapeDtypeStruct(c.shape, c.dtype) for c in caches_t],
        input_output_aliases=aliases,
        compiler_params=_params(("arbitrary",)),
        name="decode_attn",
    )(*args)
    return out[0].reshape(1, bs, GROUP_W), out[1:]


def _dec_ssd_kernel(*refs, n_alias):
    proj_ref, cs_ref, st_ref, cw_ref, cb_ref, dtb_ref, alog_ref, dskip_ref, ng_ref = refs[:9]
    y_ref, nst_ref, ncs_ref, xc_scr, xdt_t_scr, da_t_scr, ct_scr, yt_scr = refs[9 + n_alias:]
    b = pl.program_id(0)
    bs = proj_ref.shape[0]
    nblk = D_INNER // LANES
    pad_rows = lambda v: jnp.concatenate([v, jnp.zeros((LANES - bs, v.shape[1]), v.dtype)], axis=0)

    @pl.when(b == 0)
    def _():
        xr = proj_ref[:, OFF_XBC:OFF_XBC + CONV_DIM]
        acc = cb_ref[...]
        for t in range(D_CONV - 1):
            acc = acc + cs_ref[t] * cw_ref[t:t + 1, :]
        acc = acc + xr * cw_ref[D_CONV - 1:D_CONV, :]
        xc = pad_rows(_silu(acc))
        xc_scr[...] = xc
        for t in range(D_CONV - 2):
            ncs_ref[t] = cs_ref[t + 1]
        ncs_ref[D_CONV - 2] = xr
        dt = _softplus(proj_ref[:, OFF_DT:OFF_DT + LANES] + dtb_ref[...])
        da = jnp.exp(dt * (-jnp.exp(alog_ref[...])))
        onehot = (lax.shift_right_logical(lax.broadcasted_iota(jnp.int32, (LANES, D_INNER), 1), 6)
                  == lax.broadcasted_iota(jnp.int32, (LANES, D_INNER), 0)).astype(F32)
        expand = lambda v: pad_rows(jnp.dot(v, onehot, precision=lax.Precision.HIGHEST,
                                            preferred_element_type=F32))
        xdt = xc[:, 0:D_INNER] * expand(dt)
        dae = expand(da)
        for k in range(nblk):
            ks = slice(k * LANES, (k + 1) * LANES)
            xdt_t_scr[ks, :] = xdt[:, ks].T.astype(BF16)
            da_t = dae[:, ks].T
            da_hi = da_t.astype(BF16)
            da_t_scr[ks, 0:LANES] = da_hi
            da_t_scr[ks, LANES:2 * LANES] = (da_t - da_hi.astype(F32)).astype(BF16)
        for g in range(SSD_GROUPS):
            ct_scr[g] = xc[:, D_INNER + (SSD_GROUPS + g) * D_STATE:D_INNER + (SSD_GROUPS + g + 1) * D_STATE].T
        yt_scr[...] = jnp.zeros(yt_scr.shape, F32)

    row = lax.broadcasted_iota(jnp.int32, (LANES, LANES), 0)
    lane = lax.broadcasted_iota(jnp.int32, (LANES, LANES), 1)
    pick = jnp.where(row == b, 1.0, 0.0).astype(BF16)
    decay = jnp.dot(da_t_scr[...], jnp.concatenate([pick, pick], axis=0), preferred_element_type=F32)
    for g in range(SSD_GROUPS):
        rs = slice(g * SSD_GROUP_W, (g + 1) * SSD_GROUP_W)
        bg = xc_scr[:, D_INNER + g * D_STATE:D_INNER + (g + 1) * D_STATE]
        b_sel = jnp.where(row == b, bg, 0.0).astype(BF16)
        hn = st_ref[rs, :] * decay[rs, :] + jnp.dot(xdt_t_scr[rs, :], b_sel, preferred_element_type=F32)
        nst_ref[rs, :] = hn
        c_sel = jnp.where(lane == b, ct_scr[g], 0.0).astype(BF16)
        yt_scr[rs, :] += jnp.dot(hn.astype(BF16), c_sel, preferred_element_type=F32)

    @pl.when(b == pl.num_programs(0) - 1)
    def _():
        for g in range(SSD_GROUPS):
            gs = slice(g * SSD_GROUP_W, (g + 1) * SSD_GROUP_W)
            yg = jnp.concatenate([yt_scr[k * LANES:(k + 1) * LANES, :].T[0:bs, :]
                                  for k in range(g * SSD_GROUP_W // LANES, (g + 1) * SSD_GROUP_W // LANES)],
                                 axis=1)
            y = yg + dskip_ref[:, gs] * xc_scr[0:bs, gs]
            hg = y * _silu(proj_ref[:, OFF_Z + g * SSD_GROUP_W:OFF_Z + (g + 1) * SSD_GROUP_W])
            y_ref[:, gs] = (_rms(hg) * ng_ref[:, gs]).astype(BF16)


def _decode_ssd(proj_s, conv_state_t, state, conv_w, conv_b, dt_bias, a_log, d_skip, norm_g, prev_state, layer):
    bs = proj_s.shape[1]
    vec = lambda w: pl.BlockSpec((None, 1, w), lambda b: (layer, 0, 0))
    st_spec = pl.BlockSpec((None, None, D_INNER, D_STATE), lambda b: (layer, b, 0, 0))
    args = [proj_s, conv_state_t, state, conv_w, conv_b, dt_bias, a_log, d_skip, norm_g]
    extra, aliases = [], {}
    if prev_state is not None:
        extra, aliases = [pl.BlockSpec(memory_space=pl.ANY)], {len(args): 1}
        args.append(prev_state)
    return pl.pallas_call(
        functools.partial(_dec_ssd_kernel, n_alias=len(aliases)),
        grid=(bs,),
        in_specs=[
            pl.BlockSpec((None, bs, PROJ_W), lambda b: (0, 0, 0)),
            pl.BlockSpec((None, D_CONV - 1, bs, CONV_DIM), lambda b: (layer, 0, 0, 0)),
            st_spec,
            pl.BlockSpec((None, D_CONV, CONV_DIM), lambda b: (layer, 0, 0)),
            vec(CONV_DIM), vec(LANES), vec(LANES), vec(D_INNER), vec(D_INNER),
        ] + extra,
        out_specs=[
            pl.BlockSpec((None, bs, D_INNER), lambda b: (0, 0, 0)),
            st_spec,
            pl.BlockSpec((D_CONV - 1, bs, CONV_DIM), lambda b: (0, 0, 0)),
        ],
        out_shape=[
            jax.ShapeDtypeStruct((1, bs, D_INNER), BF16),
            jax.ShapeDtypeStruct(state.shape, F32),
            jax.ShapeDtypeStruct((D_CONV - 1, bs, CONV_DIM), F32),
        ],
        input_output_aliases=aliases,
        scratch_shapes=[
            pltpu.VMEM((LANES, CONV_DIM), F32),
            pltpu.VMEM((D_INNER, LANES), BF16),
            pltpu.VMEM((D_INNER, 2 * LANES), BF16),
            pltpu.VMEM((SSD_GROUPS, D_STATE, LANES), F32),
            pltpu.VMEM((D_INNER, LANES), F32),
        ],
        compiler_params=_params(("arbitrary",)),
        name="decode_ssd",
    )(*args)


def _reorder_w_in(w_in):
    q0, z0, x0, d0, g0 = 0, 3 * QKV_W, 3 * QKV_W + D_INNER, 3 * QKV_W + D_INNER + CONV_DIM, \
        3 * QKV_W + D_INNER + CONV_DIM + SSD_HEADS
    wt = jnp.swapaxes(w_in, 1, 2)
    pieces = [wt[:, z0:x0], wt[:, g0:g0 + 2 * D_MODEL], wt[:, x0:d0], wt[:, q0:z0], wt[:, d0:g0],
              jnp.zeros((wt.shape[0], DT_W - SSD_HEADS, wt.shape[2]), wt.dtype)]
    return jnp.concatenate(pieces, axis=1).astype(BF16)


def _pad_lanes(v, width):
    return jnp.pad(v, ((0, 0), (0, width - v.shape[-1])))[:, None, :]


def kernel(x_prompt, x_sample, cache_kv_g0, cache_kv_g1, cache_kv_g2, state_ssm, state_conv, c_prompt,
           c_sample, rel_bias, w_ada, b_ada, norm1_g, norm2_g, w_in, conv_w, conv_b, dt_bias, a_log, d_skip,
           ssd_norm_g, w_o_attn, w_o_ssd, w_out, w_up, w_down, final_g):
    depth = w_in.shape[0]
    b, l, d = x_prompt.shape
    bs = x_sample.shape[0]
    caches = (cache_kv_g0, cache_kv_g1, cache_kv_g2)
    assert d == D_MODEL and x_sample.shape[1] == 1 and l % ATTN_ROWS == 0 and PROJ_W % IN_PROJ_TN == 0
    assert all(win == BAND * dil for win, dil in DIL_GROUPS)
    assert all(c.shape[2] == win for c, (win, _) in zip(caches, DIL_GROUPS))
    assert bs % SUBLANES == 0 and bs <= LANES

    w_in_r = _reorder_w_in(w_in)
    w_oa, w_os, w_o, w_u, w_d = (w.astype(BF16) for w in (w_o_attn, w_o_ssd, w_out, w_up, w_down))
    n1, n2 = norm1_g[:, None, :], norm2_g[:, None, :]
    conv_b3, ng3 = conv_b[:, None, :], ssd_norm_g[:, None, :]
    dtb3, alog3 = _pad_lanes(dt_bias, LANES), _pad_lanes(a_log, LANES)
    dskip3 = jnp.repeat(d_skip, SSD_P, axis=-1)[:, None, :]
    fg = final_g[None, :]

    rows = b + bs
    rows_pad = -(-rows // SUBLANES) * SUBLANES
    c_all = jnp.pad(jnp.concatenate([c_prompt, c_sample], axis=0), ((0, rows_pad - rows), (0, 0)))
    mod = _ada_mod(c_all, w_ada, b_ada)
    mod_p = mod[:, :b, None, :]
    mod_s = mod[:, None, b:rows, :]

    bias = _bias_lookup(rel_bias, _bucket_tiles())
    head_rows = ((0, SUBLANES - HEADS_PER_GROUP), (0, 0))
    bias_rows = _bias_lookup(rel_bias, _bucket_rows())[:, 0, :]
    bias_dec = [jnp.pad(bias_rows[HEADS_PER_GROUP * g:HEADS_PER_GROUP * (g + 1), :win], head_rows)
                for g, (win, _) in enumerate(DIL_GROUPS)]
    self_bias = bias[:, 0, BAND].reshape(N_GROUPS, HEADS_PER_GROUP, 1)
    bias_self = jnp.pad(jnp.broadcast_to(self_bias, (N_GROUPS, HEADS_PER_GROUP, LANES)), ((0, 0),) + head_rows)

    xp = x_prompt
    xs = x_sample.reshape(1, bs, d)
    conv_state_t = jnp.swapaxes(state_conv, 1, 2)
    state_r = state_ssm.reshape(depth, bs, D_INNER, D_STATE)
    caches_t = [jnp.transpose(c, (0, 1, 3, 4, 5, 2)) for c in caches]
    tm_p = 1024 if l % 1024 == 0 else BAND

    kv_p = [[] for _ in DIL_GROUPS]
    ssm_p, conv_p, conv_s = [], [], []
    kv_s, st_s = None, None
    for layer in range(depth):
        final = layer == depth - 1
        proj = _in_proj(xp, mod_p, n1, w_in_r, layer, tm_p)
        attn = _prompt_attention(proj, bias)
        ssd, st, cs = _prompt_ssd(proj, conv_w, conv_b3, dtb3, alog3, dskip3, ng3, layer)
        x1 = _out_proj(attn, ssd, proj, xp, mod_p, w_oa, w_os, w_o, layer, min(512, l))
        xp = _mlp(x1, mod_p, n2, w_u, w_d, fg, layer, tm_p, final)
        for g, (win, _) in enumerate(DIL_GROUPS):
            keep = min(win, l)
            kk = proj[:, l - keep:, OFF_K + g * GROUP_W:OFF_K + (g + 1) * GROUP_W]
            vv = proj[:, l - keep:, OFF_V + g * GROUP_W:OFF_V + (g + 1) * GROUP_W]
            kv_p[g].append(jnp.stack([kk, vv], axis=2).reshape(b, keep, 2, HEADS_PER_GROUP, HEAD_DIM))
        ssm_p.append(st.reshape(b, SSD_HEADS, SSD_P, D_STATE))
        conv_p.append(cs)
        proj_s = _in_proj(xs, mod_s, n1, w_in_r, layer, bs)
        attn_s, kv_s = _decode_attention(proj_s, caches_t, bias_dec, bias_self, kv_s, layer)
        ssd_s, st_s, cs_s = _decode_ssd(proj_s, conv_state_t, state_r, conv_w, conv_b3, dtb3, alog3,
                                        dskip3, ng3, st_s, layer)
        x1s = _out_proj(attn_s, ssd_s, proj_s, xs, mod_s, w_oa, w_os, w_o, layer, bs)
        xs = _mlp(x1s, mod_s, n2, w_u, w_d, fg, layer, bs, final)
        conv_s.append(jnp.swapaxes(cs_s, 0, 1))

    kv_s = [jnp.transpose(o, (0, 1, 5, 2, 3, 4)) for o in kv_s]
    return (xp, xs.reshape(bs, 1, d),
            jnp.stack(kv_p[0], axis=0), jnp.stack(kv_p[1], axis=0), jnp.stack(kv_p[2], axis=0),
            jnp.stack(ssm_p, axis=0), jnp.stack(conv_p, axis=0),
            kv_s[0], kv_s[1], kv_s[2],
            st_s.reshape(depth, bs, SSD_HEADS, SSD_P, D_STATE), jnp.stack(conv_s, axis=0))
```

```python
import functools
import math

import jax
import jax.numpy as jnp
import numpy as np
from jax import lax
from jax.experimental import pallas as pl
from jax.experimental.pallas import tpu as pltpu

F32 = jnp.float32
BF16 = jnp.bfloat16

D_MODEL = 1024
HEAD_DIM = 64
HEADS_PER_GROUP = 4
DIL_GROUPS = ((128, 1), (512, 4), (2048, 16))
N_GROUPS = len(DIL_GROUPS)
GROUP_W = HEADS_PER_GROUP * HEAD_DIM
QKV_W = N_GROUPS * GROUP_W
BAND = 128
NUM_BUCKETS = 32
MAX_DISTANCE = 2048
D_INNER = 2 * D_MODEL
SSD_HEADS = 32
SSD_P = 64
SSD_GROUPS = 8
SSD_GROUP_W = D_INNER // SSD_GROUPS
D_STATE = 128
D_CONV = 4
CONV_DIM = D_INNER + 2 * SSD_GROUPS * D_STATE
D_FF = 4 * D_MODEL
EPS = 1e-6
ATTN_SCALE = HEAD_DIM ** -0.5
NEG = -1e30
LOG2_E = 1.4426950408889634

SUBLANES = 8
LANES = 128
V7X_VMEM_LIMIT = 52 * 1024 * 1024

OFF_Z = 0
OFF_GATE = D_INNER
OFF_XBC = 2 * D_INNER
OFF_Q = OFF_XBC + CONV_DIM
OFF_K = OFF_Q + QKV_W
OFF_V = OFF_K + QKV_W
OFF_DT = OFF_V + QKV_W
DT_W = 2 * LANES
PROJ_W = OFF_DT + DT_W

NT_DIMS = (((1,), (1,)), ((), ()))
TN_DIMS = (((0,), (0,)), ((), ()))


def _params(semantics, vmem=V7X_VMEM_LIMIT):
    return pltpu.CompilerParams(dimension_semantics=semantics, vmem_limit_bytes=vmem)


def _sigmoid(x):
    return 0.5 * (jnp.tanh(0.5 * x) + 1.0)


def _silu(x):
    h = 0.5 * x
    return h + h * jnp.tanh(h)


def _softplus(x):
    return jnp.maximum(x, 0.0) + jnp.log(1.0 + jnp.exp(-jnp.abs(x)))


def _rms(x):
    return x * lax.rsqrt(jnp.mean(x * x, axis=-1, keepdims=True) + EPS)


def _ada_kernel(c_ref, w_ref, b_ref, o_ref):
    s = _silu(c_ref[...]).astype(BF16)
    o_ref[...] = jnp.dot(s, w_ref[...].astype(BF16), preferred_element_type=F32) + b_ref[...]


def _ada_mod(c_all, w_ada, b_ada):
    depth, d, n = w_ada.shape
    rows = c_all.shape[0]
    tn = 1536
    return pl.pallas_call(
        _ada_kernel,
        grid=(depth, n // tn),
        in_specs=[
            pl.BlockSpec((rows, d), lambda l, j: (0, 0)),
            pl.BlockSpec((None, d, tn), lambda l, j: (l, 0, j)),
            pl.BlockSpec((None, 1, tn), lambda l, j: (l, 0, j)),
        ],
        out_specs=pl.BlockSpec((None, rows, tn), lambda l, j: (l, 0, j)),
        out_shape=jax.ShapeDtypeStruct((depth, rows, n), F32),
        compiler_params=_params(("arbitrary", "arbitrary")),
        name="ada_mod",
    )(c_all, w_ada, b_ada.reshape(depth, 1, n))


IN_PROJ_TN = 1792


def _in_proj_kernel(x_ref, mod_ref, g_ref, w_ref, o_ref, h_scr, *, tn):
    n = pl.program_id(2)

    @pl.when(n == 0)
    def _():
        y = _rms(x_ref[...]) * g_ref[...]
        h = y * (1.0 + mod_ref[:, D_MODEL:2 * D_MODEL]) + mod_ref[:, 0:D_MODEL]
        h_scr[...] = h.astype(BF16)

    w = w_ref[pl.ds(pl.multiple_of(n * tn, tn), tn), :]
    o_ref[...] = lax.dot_general(h_scr[...], w, NT_DIMS, preferred_element_type=F32)


def _in_proj(x, mod, norm_g, w_in_r, layer, tm):
    bx, lx, d = x.shape
    r = mod.shape[2]
    tn = IN_PROJ_TN
    return pl.pallas_call(
        functools.partial(_in_proj_kernel, tn=tn),
        grid=(bx, lx // tm, PROJ_W // tn),
        in_specs=[
            pl.BlockSpec((None, tm, d), lambda b, i, n: (b, i, 0)),
            pl.BlockSpec((None, None, r, 2 * d), lambda b, i, n: (layer, b, 0, 0)),
            pl.BlockSpec((None, 1, d), lambda b, i, n: (layer, 0, 0)),
            pl.BlockSpec((None, PROJ_W, d), lambda b, i, n: (layer, 0, 0), pipeline_mode=pl.Buffered(1)),
        ],
        out_specs=pl.BlockSpec((None, tm, tn), lambda b, i, n: (b, i, n)),
        out_shape=jax.ShapeDtypeStruct((bx, lx, PROJ_W), F32),
        scratch_shapes=[pltpu.VMEM((tm, d), BF16)],
        compiler_params=_params(("arbitrary", "arbitrary", "arbitrary")),
        name="in_proj",
    )(x, mod, norm_g, w_in_r)


def _t5_bucket_np(dist):
    max_exact = NUM_BUCKETS // 2
    df = np.maximum(dist, 1).astype(np.float32)
    ratio = np.log(df / np.float32(max_exact)) / np.float32(math.log(MAX_DISTANCE / max_exact))
    large = max_exact + (ratio * np.float32(NUM_BUCKETS - max_exact)).astype(np.int32)
    large = np.minimum(large, NUM_BUCKETS - 1)
    return np.where(dist < max_exact, dist, large).astype(np.int32)


def _bucket_tiles():
    q = np.arange(BAND)[:, None]
    c = np.arange(2 * BAND)[None, :]
    j = q + BAND - c
    valid = (j >= 0) & (j <= BAND)
    tiles = []
    for _, dil in DIL_GROUPS:
        b = _t5_bucket_np(np.clip(j, 0, BAND) * dil)
        tiles.append(np.where(valid, b, -1))
    return np.stack(tiles).astype(np.int32)


def _bucket_rows():
    width = max(win for win, _ in DIL_GROUPS)
    i = np.arange(width)
    rows = []
    for win, dil in DIL_GROUPS:
        dist = win - i
        valid = (i < win) & (dist % dil == 0)
        rows.append(np.where(valid, _t5_bucket_np(np.clip(dist, 0, win)), -1))
    return np.broadcast_to(np.stack(rows)[:, None, :], (N_GROUPS, SUBLANES, width)).astype(np.int32)


def _bias_kernel(rb_ref, bk_ref, o_ref):
    hh = pl.program_id(0)
    bk = bk_ref[...]
    acc = jnp.full(bk.shape, NEG, F32)
    for b in range(NUM_BUCKETS):
        acc = jnp.where(bk == b, rb_ref[b, hh], acc)
    o_ref[...] = acc


def _bias_lookup(rel_bias, buckets):
    n_heads = rel_bias.shape[1]
    blk = (None,) + buckets.shape[1:]
    return pl.pallas_call(
        _bias_kernel,
        grid=(n_heads,),
        in_specs=[
            pl.BlockSpec(memory_space=pltpu.SMEM),
            pl.BlockSpec(blk, lambda h: (h // HEADS_PER_GROUP, 0, 0)),
        ],
        out_specs=pl.BlockSpec(blk, lambda h: (h, 0, 0)),
        out_shape=jax.ShapeDtypeStruct((n_heads,) + buckets.shape[1:], F32),
        compiler_params=_params(("arbitrary",)),
        name="bias_lookup",
    )(rel_bias, jnp.asarray(buckets))


HEADS_PER_TILE = LANES // HEAD_DIM
ATTN_ROWS = BAND * max(dil for _, dil in DIL_GROUPS)
ATTN_TILES_IN_FLIGHT = 8
ATTN_MERGE_ROWS = 256


def _attn_kernel(*refs):
    n = N_GROUPS
    bias_ref, o_ref, og_scr, lse_scr = refs[5 * n:]
    first = pl.program_id(1) == 0
    head0 = pl.program_id(2) * HEADS_PER_TILE
    col = lax.broadcasted_iota(jnp.int32, (BAND, 2 * BAND), 1)
    pen_first = jnp.where(col < BAND, jnp.where(first, NEG, 0.0), 0.0)
    lane = lax.broadcasted_iota(jnp.int32, (BAND, LANES), 1)

    for g, (_, dil) in enumerate(DIL_GROUPS):
        q_ref, kp_ref, kc_ref, vp_ref, vc_ref = refs[5 * g:5 * g + 5]
        period = BAND * dil
        periods = ATTN_ROWS // period

        def rows_of(start, dil=dil):
            if dil > 1:
                return pl.ds(start, BAND, stride=dil)
            return pl.ds(pl.multiple_of(start, BAND), BAND)

        def tile(rows, kprev, vprev, leading, g=g, q_ref=q_ref, kc_ref=kc_ref, vc_ref=vc_ref):
            q = q_ref[rows, :]
            kb = jnp.concatenate([kprev, kc_ref[rows, :]], axis=0).astype(BF16)
            vb = jnp.concatenate([vprev, vc_ref[rows, :]], axis=0).astype(BF16)
            o, lse = None, None
            for h in range(HEADS_PER_TILE):
                mine = jnp.logical_and(lane >= h * HEAD_DIM, lane < (h + 1) * HEAD_DIM)
                qh = jnp.where(mine, q, 0.0).astype(BF16)
                s = lax.dot_general(qh, kb, NT_DIMS, preferred_element_type=F32)
                s = s * ATTN_SCALE + bias_ref[HEADS_PER_GROUP * g + head0 + h]
                if leading:
                    s = s + pen_first
                m = jnp.max(s, axis=-1, keepdims=True)
                p = jnp.exp(s - m)
                l = jnp.sum(p, axis=-1, keepdims=True)
                oh = jnp.dot(p.astype(BF16), vb, preferred_element_type=F32) / l
                lh = jnp.broadcast_to(m + jnp.log(l), (BAND, LANES))
                o = oh if h == 0 else jnp.where(mine, oh, o)
                lse = lh if h == 0 else jnp.where(mine, lh, lse)
            og_scr[g, rows, :] = o
            lse_scr[g, rows, :] = lse

        def lead(r, carry, rows_of=rows_of, tile=tile, kp_ref=kp_ref, vp_ref=vp_ref):
            rows = rows_of(r)
            tile(rows, kp_ref[rows, :], vp_ref[rows, :], True)
            return carry

        def body(j, carry, rows_of=rows_of, tile=tile, kc_ref=kc_ref, vc_ref=vc_ref, dil=dil, period=period):
            r, tau = j % dil, 1 + j // dil
            above = rows_of(r + (tau - 1) * period)
            tile(rows_of(r + tau * period), kc_ref[above, :], vc_ref[above, :], False)
            return carry

        lax.fori_loop(0, dil, lead, 0, unroll=min(dil, ATTN_TILES_IN_FLIGHT))
        n_body = dil * (periods - 1)
        if n_body:
            unroll = max(u for u in range(1, ATTN_TILES_IN_FLIGHT + 1) if n_body % u == 0)
            lax.fori_loop(0, n_body, body, 0, unroll=unroll)

    def merge(c, carry):
        rs = pl.ds(pl.multiple_of(c * ATTN_MERGE_ROWS, ATTN_MERGE_ROWS), ATTN_MERGE_ROWS)
        lse = [lse_scr[g, rs, :] for g in range(n)]
        m = functools.reduce(jnp.maximum, lse)
        e = [jnp.exp(v - m) for v in lse]
        o_ref[rs, :] = sum(e[g] * og_scr[g, rs, :] for g in range(n)) / sum(e)
        return carry

    lax.fori_loop(0, ATTN_ROWS // ATTN_MERGE_ROWS, merge, 0)


def _prompt_attention(proj, bias):
    b, l, _ = proj.shape
    tiles = GROUP_W // LANES
    in_specs = []
    for g, (_, dil) in enumerate(DIL_GROUPS):
        period = BAND * dil
        qb, kb, vb = (off // LANES + g * tiles for off in (OFF_Q, OFF_K, OFF_V))
        cur = lambda cb: pl.BlockSpec((None, ATTN_ROWS, LANES), lambda bb, i, t, cb=cb: (bb, i, cb + t))
        prev = lambda cb, per=ATTN_ROWS // period, period=period: pl.BlockSpec(
            (None, period, LANES), lambda bb, i, t: (bb, jnp.maximum(i * per - 1, 0), cb + t))
        in_specs += [cur(qb), prev(kb), cur(kb), prev(vb), cur(vb)]
    in_specs.append(pl.BlockSpec(bias.shape, lambda bb, i, t: (0, 0, 0), pipeline_mode=pl.Buffered(1)))
    return pl.pallas_call(
        _attn_kernel,
        grid=(b, l // ATTN_ROWS, tiles),
        in_specs=in_specs,
        out_specs=pl.BlockSpec((None, ATTN_ROWS, LANES), lambda bb, i, t: (bb, i, t)),
        out_shape=jax.ShapeDtypeStruct((b, l, GROUP_W), F32),
        scratch_shapes=[pltpu.VMEM((N_GROUPS, ATTN_ROWS, LANES), F32)] * 2,
        compiler_params=_params(("arbitrary", "arbitrary", "arbitrary")),
        name="prompt_attn",
    )(*([proj] * (5 * N_GROUPS)), bias)


def _expand4(arr, g, lane):
    rows = arr.shape[0]
    c = [jnp.broadcast_to(arr[:, 4 * g + e:4 * g + e + 1], (rows, SSD_GROUP_W)) for e in range(4)]
    return jnp.where(lane < SSD_P, c[0], jnp.where(lane < 2 * SSD_P, c[1],
                                                    jnp.where(lane < 3 * SSD_P, c[2], c[3])))


def _cumsum_rows(a):
    row = lax.broadcasted_iota(jnp.int32, a.shape, 0)
    s = 1
    while s < a.shape[0]:
        a = a + jnp.where(row >= s, pltpu.roll(a, s, axis=0), 0.0)
        s *= 2
    return a


def _ssd_kernel(xbc_ref, dt_ref, z_ref, cw_ref, cb_ref, dtb_ref, alog_ref, dskip_ref, ng_ref,
                y_ref, st_ref, cs_ref, xp_scr, xc_scr, stt_scr):
    q = BAND
    c = pl.program_id(1)
    last = c == pl.num_programs(1) - 1

    @pl.when(c == 0)
    def _():
        xp_scr[0:SUBLANES, :] = jnp.zeros((SUBLANES, CONV_DIM), F32)
        stt_scr[...] = jnp.zeros(stt_scr.shape, F32)

    xp_scr[SUBLANES:SUBLANES + q, :] = xbc_ref[...]
    cblk = 512
    nt = q // SUBLANES
    sub = lax.broadcasted_iota(jnp.int32, (nt, SUBLANES, cblk), 1)
    for j in range(CONV_DIM // cblk):
        cs = slice(j * cblk, (j + 1) * cblk)
        x3 = xp_scr[:, cs].reshape(nt + 1, SUBLANES, cblk)
        acc = cb_ref[:, cs] + x3[1:] * cw_ref[D_CONV - 1:D_CONV, cs]
        for k in range(1, D_CONV):
            rot = pltpu.roll(x3, k, axis=1)
            back = jnp.where(sub < k, rot[:-1], rot[1:])
            acc = acc + back * cw_ref[D_CONV - 1 - k:D_CONV - k, cs]
        xc_scr[:, cs] = _silu(acc).reshape(q, cblk)
    xp_scr[0:SUBLANES, :] = xp_scr[q:q + SUBLANES, :]

    dt = _softplus(dt_ref[:, 0:LANES] + dtb_ref[...])
    a = dt * (-jnp.exp(alog_ref[...]))
    acs = _cumsum_rows(a) * LOG2_E
    acs_t = acs.T
    dt_t = dt.T
    ea_last = jnp.exp2(acs[q - 1:q, :])
    w_end_t = dt_t * jnp.exp2(jnp.broadcast_to(acs_t[:, q - 1:q], (q, q)) - acs_t)
    src_t = acs_t - jnp.log2(dt_t)

    row = lax.broadcasted_iota(jnp.int32, (q, q), 0)
    colq = lax.broadcasted_iota(jnp.int32, (q, q), 1)
    tril = row >= colq
    lane = lax.broadcasted_iota(jnp.int32, (q, SSD_GROUP_W), 1)
    lane1 = lax.broadcasted_iota(jnp.int32, (1, SSD_GROUP_W), 1)

    for g in range(SSD_GROUPS):
        gs = slice(g * SSD_GROUP_W, (g + 1) * SSD_GROUP_W)
        bg = xc_scr[:, D_INNER + g * D_STATE:D_INNER + (g + 1) * D_STATE]
        cg = xc_scr[:, D_INNER + (SSD_GROUPS + g) * D_STATE:D_INNER + (SSD_GROUPS + g + 1) * D_STATE]
        cbm = lax.dot_general(cg.astype(BF16), bg.astype(BF16), NT_DIMS, preferred_element_type=F32)
        bg_t = bg.T
        xg = xc_scr[:, gs]
        xgb = xg.astype(BF16)
        stg = stt_scr[:, gs]
        rhs = jnp.concatenate([xgb, stg.astype(BF16)], axis=0)
        y, snew = None, None
        for e in range(4):
            h = 4 * g + e
            col_h = jnp.broadcast_to(acs[:, h:h + 1], (q, q))
            row_h = lambda v: jnp.broadcast_to(v[h:h + 1, :], (q, q))
            intra = cbm * jnp.exp2(jnp.where(tril, col_h - row_h(src_t), NEG))
            inter = cg * jnp.exp2(col_h)
            r = jnp.dot(jnp.concatenate([intra, inter], axis=1).astype(BF16), rhs,
                        preferred_element_type=F32)
            sr = jnp.dot((bg_t * row_h(w_end_t)).astype(BF16), xgb, preferred_element_type=F32)
            mine = lane >= e * SSD_P
            y = r if e == 0 else jnp.where(mine, r, y)
            snew = sr if e == 0 else jnp.where(mine, sr, snew)
        y = y + dskip_ref[:, gs] * xg
        stt_scr[:, gs] = stg * _expand4(ea_last, g, lane1) + snew
        hg = y * _silu(z_ref[:, gs])
        y_ref[:, gs] = (_rms(hg) * ng_ref[:, gs]).astype(BF16)

    @pl.when(last)
    def _():
        cs_ref[...] = xp_scr[SUBLANES + q - (D_CONV - 1):SUBLANES + q, :]
        for k in range(D_INNER // LANES):
            st_ref[k * LANES:(k + 1) * LANES, :] = stt_scr[:, k * LANES:(k + 1) * LANES].T


def _prompt_ssd(proj, conv_w, conv_b, dt_bias, a_log, d_skip, norm_g, layer):
    b, l, _ = proj.shape
    q = BAND
    vec = lambda w: pl.BlockSpec((None, 1, w), lambda bb, c: (layer, 0, 0))
    return pl.pallas_call(
        _ssd_kernel,
        grid=(b, l // q),
        in_specs=[
            pl.BlockSpec((None, q, CONV_DIM), lambda bb, c: (bb, c, OFF_XBC // CONV_DIM)),
            pl.BlockSpec((None, q, DT_W), lambda bb, c: (bb, c, OFF_DT // DT_W)),
            pl.BlockSpec((None, q, D_INNER), lambda bb, c: (bb, c, OFF_Z // D_INNER)),
            pl.BlockSpec((None, D_CONV, CONV_DIM), lambda bb, c: (layer, 0, 0)),
            vec(CONV_DIM), vec(LANES), vec(LANES), vec(D_INNER), vec(D_INNER),
        ],
        out_specs=[
            pl.BlockSpec((None, q, D_INNER), lambda bb, c: (bb, c, 0)),
            pl.BlockSpec((None, D_INNER, D_STATE), lambda bb, c: (bb, 0, 0)),
            pl.BlockSpec((None, D_CONV - 1, CONV_DIM), lambda bb, c: (bb, 0, 0)),
        ],
        out_shape=[
            jax.ShapeDtypeStruct((b, l, D_INNER), BF16),
            jax.ShapeDtypeStruct((b, D_INNER, D_STATE), F32),
            jax.ShapeDtypeStruct((b, D_CONV - 1, CONV_DIM), F32),
        ],
        scratch_shapes=[
            pltpu.VMEM((q + SUBLANES, CONV_DIM), F32),
            pltpu.VMEM((q, CONV_DIM), F32),
            pltpu.VMEM((D_STATE, D_INNER), F32),
        ],
        compiler_params=_params(("arbitrary", "arbitrary")),
        name="prompt_ssd",
    )(proj, proj, proj, conv_w, conv_b, dt_bias, a_log, d_skip, norm_g)


def _out_proj_kernel(attn_ref, ssd_ref, gate_ref, x_ref, g1_ref, wa_ref, ws_ref, wo_ref, o_ref):
    pa = jnp.dot(attn_ref[...].astype(BF16), wa_ref[...], preferred_element_type=F32)
    ps = jnp.dot(ssd_ref[...], ws_ref[...], preferred_element_type=F32)
    merged = _sigmoid(gate_ref[:, 0:D_MODEL]) * pa + _sigmoid(gate_ref[:, D_MODEL:2 * D_MODEL]) * ps
    o_ref[...] = x_ref[...] + g1_ref[...] * jnp.dot(
        merged.astype(BF16), wo_ref[...], preferred_element_type=F32)


def _out_proj(attn, ssd, proj, x, mod, w_o_attn, w_o_ssd, w_out, layer, tm):
    bx, lx, d = x.shape
    r = mod.shape[2]
    row = lambda w, cb=0: pl.BlockSpec((None, tm, w), lambda b, i: (b, i, cb))
    wgt = lambda k, n: pl.BlockSpec((None, k, n), lambda b, i: (layer, 0, 0), pipeline_mode=pl.Buffered(1))
    return pl.pallas_call(
        _out_proj_kernel,
        grid=(bx, lx // tm),
        in_specs=[
            row(GROUP_W), row(D_INNER), row(2 * D_MODEL, OFF_GATE // (2 * D_MODEL)), row(d),
            pl.BlockSpec((None, None, r, d), lambda b, i: (layer, b, 0, 2)),
            wgt(GROUP_W, d), wgt(D_INNER, d), wgt(d, d),
        ],
        out_specs=row(d),
        out_shape=jax.ShapeDtypeStruct((bx, lx, d), F32),
        compiler_params=_params(("arbitrary", "arbitrary")),
        name="out_proj",
    )(attn, ssd, proj, x, mod, w_o_attn, w_o_ssd, w_out)


MLP_FF_TILE = 1024


def _mlp_kernel(*refs, final, ride):
    x_ref, mod_ref, g_ref, wu_ref, wd_ref, fg_ref = refs[:6]
    o_ref = refs[-3] if ride else refs[-1]
    x = x_ref[...]
    h = (_rms(x) * g_ref[...]) * (1.0 + mod_ref[:, D_MODEL:2 * D_MODEL]) + mod_ref[:, 0:D_MODEL]
    hb = h.astype(BF16)
    acc = None
    for f in range(D_FF // MLP_FF_TILE):
        fs = slice(f * MLP_FF_TILE, (f + 1) * MLP_FF_TILE)
        u = jnp.maximum(jnp.dot(hb, wu_ref[:, fs], preferred_element_type=F32), 0.0)
        part = jnp.dot((u * u).astype(BF16), wd_ref[fs, :], preferred_element_type=F32)
        acc = part if f == 0 else acc + part
    x2 = x + mod_ref[:, 2 * D_MODEL:3 * D_MODEL] * acc
    if final:
        x2 = _rms(x2) * fg_ref[...]
    o_ref[...] = x2

    if ride:
        proj_ref, c_ref, b_ref, bself_ref = refs[6:10]
        oc_ref, part_ref = refs[-2:]
        g = RIDE_GROUP
        consts = _dec_consts()
        for j in range(proj_ref.shape[0]):
            m, l, out = _dec_group(g, proj_ref[j], c_ref.at[j], oc_ref.at[j], b_ref[...],
                                   bself_ref[g][:, 0:1], consts)
            part_ref[j, :, 0:GROUP_W] = out
            part_ref[j, :, GROUP_W:GROUP_W + LANES] = jnp.broadcast_to(m, (SUBLANES, LANES))
            part_ref[j, :, GROUP_W + LANES:PART_W] = jnp.broadcast_to(l, (SUBLANES, LANES))


def _mlp(x, mod, norm_g, w_up, w_down, final_g, layer, tm, final, ride=None):
    bx, lx, d = x.shape
    r = mod.shape[2]
    nt = lx // tm
    resident = pl.Buffered(1)
    in_specs = [
        pl.BlockSpec((None, tm, d), lambda b, i: (b, i, 0)),
        pl.BlockSpec((None, None, r, 3 * d), lambda b, i: (layer, b, 0, 1)),
        pl.BlockSpec((None, 1, d), lambda b, i: (layer, 0, 0)),
        pl.BlockSpec((None, d, D_FF), lambda b, i: (layer, 0, 0), pipeline_mode=resident),
        pl.BlockSpec((None, D_FF, d), lambda b, i: (layer, 0, 0), pipeline_mode=resident),
        pl.BlockSpec((1, d), lambda b, i: (0, 0)),
    ]
    out_specs = [pl.BlockSpec((None, tm, d), lambda b, i: (b, i, 0))]
    out_shape = [jax.ShapeDtypeStruct((bx, lx, d), F32)]
    args = [x, mod, norm_g, w_up, w_down, final_g]
    aliases = {}
    if ride is not None:
        proj_s, cache, bias, bias_self, prev = ride
        bs = proj_s.shape[1]
        k = bs // (bx * nt)
        assert k * bx * nt == bs
        cspec = pl.BlockSpec((None, k) + cache.shape[2:], lambda b, i: (layer, b * nt + i, 0, 0, 0, 0))
        in_specs += [pl.BlockSpec((k, 1, PROJ_W), lambda b, i: (b * nt + i, 0, 0)), cspec,
                     pl.BlockSpec(bias.shape, lambda b, i: (0, 0)),
                     pl.BlockSpec(bias_self.shape, lambda b, i: (0, 0, 0))]
        args += [proj_s.reshape(bs, 1, PROJ_W), cache, bias, bias_self]
        if prev is not None:
            in_specs.append(pl.BlockSpec(memory_space=pl.ANY))
            aliases = {len(args): 1}
            args.append(prev)
        out_specs += [cspec, pl.BlockSpec((k, SUBLANES, PART_W), lambda b, i: (b * nt + i, 0, 0))]
        out_shape += [jax.ShapeDtypeStruct(cache.shape, cache.dtype),
                      jax.ShapeDtypeStruct((bs, SUBLANES, PART_W), F32)]
    out = pl.pallas_call(
        functools.partial(_mlp_kernel, final=final, ride=ride is not None),
        grid=(bx, nt),
        in_specs=in_specs,
        out_specs=out_specs,
        out_shape=out_shape,
        input_output_aliases=aliases,
        compiler_params=_params(("arbitrary", "arbitrary")),
        name="mlp",
    )(*args)
    return out if ride is not None else out[0]


RIDE_GROUP = N_GROUPS - 1
PART_W = GROUP_W + 2 * LANES


def _dec_consts():
    row = lax.broadcasted_iota(jnp.int32, (SUBLANES, GROUP_W), 0)
    lane = lax.broadcasted_iota(jnp.int32, (SUBLANES, GROUP_W), 1)
    hmask = jnp.logical_and(lane >= row * HEAD_DIM, lane < (row + 1) * HEAD_DIM)
    sel = jnp.logical_or(jnp.logical_and(row == 0, lane < LANES),
                         jnp.logical_and(row == 1, lane >= LANES)).astype(F32)
    last_lane = lax.broadcasted_iota(jnp.int32, (GROUP_W, LANES), 1) == LANES - 1
    return row, hmask, sel, last_lane


def _dec_group(g, proj_row, c_ref, oc_ref, bias, bias_self, consts):
    row, hmask, sel, last_lane = consts
    lb = c_ref.shape[-1]
    qv = proj_row[:, OFF_Q + g * GROUP_W:OFF_Q + (g + 1) * GROUP_W]
    kn = proj_row[:, OFF_K + g * GROUP_W:OFF_K + (g + 1) * GROUP_W]
    vn = proj_row[:, OFF_V + g * GROUP_W:OFF_V + (g + 1) * GROUP_W]
    qbd = jnp.where(hmask, jnp.broadcast_to(qv, (SUBLANES, GROUP_W)), 0.0)
    kt = jnp.concatenate([c_ref[0, h] for h in range(HEADS_PER_GROUP)], axis=0)
    vt = jnp.concatenate([c_ref[1, h] for h in range(HEADS_PER_GROUP)], axis=0)
    s = jnp.dot(qbd.astype(BF16), kt.astype(BF16), preferred_element_type=F32)
    s = s * ATTN_SCALE + bias
    s_self = jnp.sum(qbd * kn, axis=-1, keepdims=True) * ATTN_SCALE + bias_self
    m = jnp.maximum(jnp.max(s, axis=-1, keepdims=True), s_self)
    p = jnp.exp(s - m)
    p_self = jnp.exp(s_self - m)
    l = jnp.sum(p, axis=-1, keepdims=True) + p_self
    acc = lax.dot_general(p.astype(BF16), vt.astype(BF16), NT_DIMS,
                          preferred_element_type=F32) + p_self * vn
    rows2 = jnp.where(row == 0, jnp.broadcast_to(kn, (SUBLANES, GROUP_W)),
                      jnp.where(row == 1, jnp.broadcast_to(vn, (SUBLANES, GROUP_W)), 0.0))
    cols = lax.dot_general(rows2, sel, TN_DIMS, precision=lax.Precision.HIGHEST,
                           preferred_element_type=F32)
    for kv, t in ((0, kt), (1, vt)):
        rolled = pltpu.roll(t, lb - 1, axis=1)
        tail = jnp.where(last_lane, cols[:, kv * LANES:(kv + 1) * LANES], rolled[:, lb - LANES:lb])
        new = tail if lb == LANES else jnp.concatenate([rolled[:, 0:lb - LANES], tail], axis=1)
        for h in range(HEADS_PER_GROUP):
            oc_ref[kv, h] = new[h * HEAD_DIM:(h + 1) * HEAD_DIM, :]
    return m, l, acc


def _dec_attn_kernel(*refs, groups, n_alias):
    n = len(groups)
    proj_ref, c_refs, b_refs = refs[0], refs[1:1 + n], refs[1 + n:1 + 2 * n]
    bself_ref, part_ref = refs[1 + 2 * n], refs[2 + 2 * n]
    o_ref, oc_refs = refs[3 + 2 * n + n_alias], refs[4 + 2 * n + n_alias:4 + 3 * n + n_alias]
    consts = _dec_consts()
    hmask = consts[1]
    parts = [_dec_group(g, proj_ref[...], c_refs[i], oc_refs[i], b_refs[i][...], bself_ref[g][:, 0:1], consts)
             for i, g in enumerate(groups)]
    parts.append((part_ref[:, GROUP_W:GROUP_W + 1], part_ref[:, GROUP_W + LANES:GROUP_W + LANES + 1],
                  part_ref[:, 0:GROUP_W]))
    m_all = functools.reduce(jnp.maximum, [p[0] for p in parts])
    num = sum(jnp.exp(m - m_all) * acc for m, _, acc in parts)
    den = sum(jnp.exp(m - m_all) * l for m, l, _ in parts)
    o_ref[...] = jnp.sum(jnp.where(hmask, num / den, 0.0), axis=0, keepdims=True)


def _decode_attention(proj_s, groups, caches_t, bias_dec, bias_self, part, prev_out, layer):
    bs = proj_s.shape[1]
    n = len(groups)
    cspec = lambda c: pl.BlockSpec((None, None) + c.shape[2:], lambda b: (layer, b, 0, 0, 0, 0))
    in_specs = [pl.BlockSpec((None, 1, PROJ_W), lambda b: (b, 0, 0))]
    in_specs += [cspec(c) for c in caches_t]
    in_specs += [pl.BlockSpec(bd.shape, lambda b: (0, 0)) for bd in bias_dec]
    in_specs += [pl.BlockSpec(bias_self.shape, lambda b: (0, 0, 0)),
                 pl.BlockSpec((None, SUBLANES, PART_W), lambda b: (b, 0, 0))]
    args = [proj_s.reshape(bs, 1, PROJ_W), *caches_t, *bias_dec, bias_self, part]
    aliases = {}
    if prev_out is not None:
        in_specs += [pl.BlockSpec(memory_space=pl.ANY)] * n
        aliases = {len(args) + j: 1 + j for j in range(n)}
        args += list(prev_out)
    out = pl.pallas_call(
        functools.partial(_dec_attn_kernel, groups=tuple(groups), n_alias=len(aliases)),
        grid=(bs,),
        in_specs=in_specs,
        out_specs=[pl.BlockSpec((None, 1, GROUP_W), lambda b: (b, 0, 0))] + [cspec(c) for c in caches_t],
        out_shape=[jax.ShapeDtypeStruct((bs, 1, GROUP_W), F32)] + [
            jax.ShapeDtypeStruct(c.shape, c.dtype) for c in caches_t],
        input_output_aliases=aliases,
        compiler_params=_params(("arbitrary",)),
        name="decode_attn",
    )(*args)
    return out[0].reshape(1, bs, GROUP_W), out[1:]


def _dec_ssd_kernel(*refs, n_alias):
    proj_ref, cs_ref, st_ref, cw_ref, cb_ref, dtb_ref, alog_ref, dskip_ref, ng_ref = refs[:9]
    y_ref, nst_ref, ncs_ref, xc_scr, xdt_t_scr, da_t_scr, ct_scr, yt_scr = refs[9 + n_alias:]
    b = pl.program_id(0)
    bs = proj_ref.shape[0]
    nblk = D_INNER // LANES
    pad_rows = lambda v: jnp.concatenate([v, jnp.zeros((LANES - bs, v.shape[1]), v.dtype)], axis=0)

    @pl.when(b == 0)
    def _():
        xr = proj_ref[:, OFF_XBC:OFF_XBC + CONV_DIM]
        acc = cb_ref[...]
        for t in range(D_CONV - 1):
            acc = acc + cs_ref[t] * cw_ref[t:t + 1, :]
        acc = acc + xr * cw_ref[D_CONV - 1:D_CONV, :]
        xc = pad_rows(_silu(acc))
        xc_scr[...] = xc
        for t in range(D_CONV - 2):
            ncs_ref[t] = cs_ref[t + 1]
        ncs_ref[D_CONV - 2] = xr
        dt = _softplus(proj_ref[:, OFF_DT:OFF_DT + LANES] + dtb_ref[...])
        da = jnp.exp(dt * (-jnp.exp(alog_ref[...])))
        onehot = (lax.shift_right_logical(lax.broadcasted_iota(jnp.int32, (LANES, D_INNER), 1), 6)
                  == lax.broadcasted_iota(jnp.int32, (LANES, D_INNER), 0)).astype(F32)
        expand = lambda v: pad_rows(jnp.dot(v, onehot, precision=lax.Precision.HIGHEST,
                                            preferred_element_type=F32))
        xdt = xc[:, 0:D_INNER] * expand(dt)
        dae = expand(da)
        for k in range(nblk):
            ks = slice(k * LANES, (k + 1) * LANES)
            xdt_t_scr[ks, :] = xdt[:, ks].T.astype(BF16)
            da_t = dae[:, ks].T
            da_hi = da_t.astype(BF16)
            da_t_scr[ks, 0:LANES] = da_hi
            da_t_scr[ks, LANES:2 * LANES] = (da_t - da_hi.astype(F32)).astype(BF16)
        for g in range(SSD_GROUPS):
            ct_scr[g] = xc[:, D_INNER + (SSD_GROUPS + g) * D_STATE:D_INNER + (SSD_GROUPS + g + 1) * D_STATE].T
        yt_scr[...] = jnp.zeros(yt_scr.shape, F32)

    row = lax.broadcasted_iota(jnp.int32, (LANES, LANES), 0)
    lane = lax.broadcasted_iota(jnp.int32, (LANES, LANES), 1)
    pick = jnp.where(row == b, 1.0, 0.0).astype(BF16)
    decay = jnp.dot(da_t_scr[...], jnp.concatenate([pick, pick], axis=0), preferred_element_type=F32)
    for g in range(SSD_GROUPS):
        rs = slice(g * SSD_GROUP_W, (g + 1) * SSD_GROUP_W)
        bg = xc_scr[:, D_INNER + g * D_STATE:D_INNER + (g + 1) * D_STATE]
        b_sel = jnp.where(row == b, bg, 0.0).astype(BF16)
        hn = st_ref[rs, :] * decay[rs, :] + jnp.dot(xdt_t_scr[rs, :], b_sel, preferred_element_type=F32)
        nst_ref[rs, :] = hn
        c_sel = jnp.where(lane == b, ct_scr[g], 0.0).astype(BF16)
        yt_scr[rs, :] += jnp.dot(hn.astype(BF16), c_sel, preferred_element_type=F32)

    @pl.when(b == pl.num_programs(0) - 1)
    def _():
        for g in range(SSD_GROUPS):
            gs = slice(g * SSD_GROUP_W, (g + 1) * SSD_GROUP_W)
            yg = jnp.concatenate([yt_scr[k * LANES:(k + 1) * LANES, :].T[0:bs, :]
                                  for k in range(g * SSD_GROUP_W // LANES, (g + 1) * SSD_GROUP_W // LANES)],
                                 axis=1)
            y = yg + dskip_ref[:, gs] * xc_scr[0:bs, gs]
            hg = y * _silu(proj_ref[:, OFF_Z + g * SSD_GROUP_W:OFF_Z + (g + 1) * SSD_GROUP_W])
            y_ref[:, gs] = (_rms(hg) * ng_ref[:, gs]).astype(BF16)


def _decode_ssd(proj_s, conv_state_t, state, conv_w, conv_b, dt_bias, a_log, d_skip, norm_g, prev_state, layer):
    bs = proj_s.shape[1]
    vec = lambda w: pl.BlockSpec((None, 1, w), lambda b: (layer, 0, 0))
    st_spec = pl.BlockSpec((None, None, D_INNER, D_STATE), lambda b: (layer, b, 0, 0))
    args = [proj_s, conv_state_t, state, conv_w, conv_b, dt_bias, a_log, d_skip, norm_g]
    extra, aliases = [], {}
    if prev_state is not None:
        extra, aliases = [pl.BlockSpec(memory_space=pl.ANY)], {len(args): 1}
        args.append(prev_state)
    return pl.pallas_call(
        functools.partial(_dec_ssd_kernel, n_alias=len(aliases)),
        grid=(bs,),
        in_specs=[
            pl.BlockSpec((None, bs, PROJ_W), lambda b: (0, 0, 0)),
            pl.BlockSpec((None, D_CONV - 1, bs, CONV_DIM), lambda b: (layer, 0, 0, 0)),
            st_spec,
            pl.BlockSpec((None, D_CONV, CONV_DIM), lambda b: (layer, 0, 0)),
            vec(CONV_DIM), vec(LANES), vec(LANES), vec(D_INNER), vec(D_INNER),
        ] + extra,
        out_specs=[
            pl.BlockSpec((None, bs, D_INNER), lambda b: (0, 0, 0)),
            st_spec,
            pl.BlockSpec((D_CONV - 1, bs, CONV_DIM), lambda b: (0, 0, 0)),
        ],
        out_shape=[
            jax.ShapeDtypeStruct((1, bs, D_INNER), BF16),
            jax.ShapeDtypeStruct(state.shape, F32),
            jax.ShapeDtypeStruct((D_CONV - 1, bs, CONV_DIM), F32),
        ],
        input_output_aliases=aliases,
        scratch_shapes=[
            pltpu.VMEM((LANES, CONV_DIM), F32),
            pltpu.VMEM((D_INNER, LANES), BF16),
            pltpu.VMEM((D_INNER, 2 * LANES), BF16),
            pltpu.VMEM((SSD_GROUPS, D_STATE, LANES), F32),
            pltpu.VMEM((D_INNER, LANES), F32),
        ],
        compiler_params=_params(("arbitrary",)),
        name="decode_ssd",
    )(*args)


def _reorder_w_in(w_in):
    q0, z0, x0, d0, g0 = 0, 3 * QKV_W, 3 * QKV_W + D_INNER, 3 * QKV_W + D_INNER + CONV_DIM, \
        3 * QKV_W + D_INNER + CONV_DIM + SSD_HEADS
    wt = jnp.swapaxes(w_in, 1, 2)
    pieces = [wt[:, z0:x0], wt[:, g0:g0 + 2 * D_MODEL], wt[:, x0:d0], wt[:, q0:z0], wt[:, d0:g0],
              jnp.zeros((wt.shape[0], DT_W - SSD_HEADS, wt.shape[2]), wt.dtype)]
    return jnp.concatenate(pieces, axis=1).astype(BF16)


def _pad_lanes(v, width):
    return jnp.pad(v, ((0, 0), (0, width - v.shape[-1])))[:, None, :]


def kernel(x_prompt, x_sample, cache_kv_g0, cache_kv_g1, cache_kv_g2, state_ssm, state_conv, c_prompt,
           c_sample, rel_bias, w_ada, b_ada, norm1_g, norm2_g, w_in, conv_w, conv_b, dt_bias, a_log, d_skip,
           ssd_norm_g, w_o_attn, w_o_ssd, w_out, w_up, w_down, final_g):
    depth = w_in.shape[0]
    b, l, d = x_prompt.shape
    bs = x_sample.shape[0]
    caches = (cache_kv_g0, cache_kv_g1, cache_kv_g2)
    assert d == D_MODEL and x_sample.shape[1] == 1 and l % ATTN_ROWS == 0 and PROJ_W % IN_PROJ_TN == 0
    assert all(win == BAND * dil for win, dil in DIL_GROUPS)
    assert all(c.shape[2] == win for c, (win, _) in zip(caches, DIL_GROUPS))
    assert bs % SUBLANES == 0 and bs <= LANES

    w_in_r = _reorder_w_in(w_in)
    w_oa, w_os, w_o, w_u, w_d = (w.astype(BF16) for w in (w_o_attn, w_o_ssd, w_out, w_up, w_down))
    n1, n2 = norm1_g[:, None, :], norm2_g[:, None, :]
    conv_b3, ng3 = conv_b[:, None, :], ssd_norm_g[:, None, :]
    dtb3, alog3 = _pad_lanes(dt_bias, LANES), _pad_lanes(a_log, LANES)
    dskip3 = jnp.repeat(d_skip, SSD_P, axis=-1)[:, None, :]
    fg = final_g[None, :]

    rows = b + bs
    rows_pad = -(-rows // SUBLANES) * SUBLANES
    c_all = jnp.pad(jnp.concatenate([c_prompt, c_sample], axis=0), ((0, rows_pad - rows), (0, 0)))
    mod = _ada_mod(c_all, w_ada, b_ada)
    mod_p = mod[:, :b, None, :]
    mod_s = mod[:, None, b:rows, :]

    bias = _bias_lookup(rel_bias, _bucket_tiles())
    head_rows = ((0, SUBLANES - HEADS_PER_GROUP), (0, 0))
    bias_rows = _bias_lookup(rel_bias, _bucket_rows())[:, 0, :]
    bias_dec = [jnp.pad(bias_rows[HEADS_PER_GROUP * g:HEADS_PER_GROUP * (g + 1), :win], head_rows)
                for g, (win, _) in enumerate(DIL_GROUPS)]
    self_bias = bias[:, 0, BAND].reshape(N_GROUPS, HEADS_PER_GROUP, 1)
    bias_self = jnp.pad(jnp.broadcast_to(self_bias, (N_GROUPS, HEADS_PER_GROUP, LANES)), ((0, 0),) + head_rows)

    xp = x_prompt
    xs = x_sample.reshape(1, bs, d)
    conv_state_t = jnp.swapaxes(state_conv, 1, 2)
    state_r = state_ssm.reshape(depth, bs, D_INNER, D_STATE)
    caches_t = [jnp.transpose(c, (0, 1, 3, 4, 5, 2)) for c in caches]
    tm_p = 1024 if l % 1024 == 0 else BAND

    kv_p = [[] for _ in DIL_GROUPS]
    ssm_p, conv_p, conv_s = [], [], []
    others = [g for g in range(N_GROUPS) if g != RIDE_GROUP]
    kv_ride, kv_rest, st_s = None, None, None
    for layer in range(depth):
        final = layer == depth - 1
        proj_s = _in_proj(xs, mod_s, n1, w_in_r, layer, bs)
        proj = _in_proj(xp, mod_p, n1, w_in_r, layer, tm_p)
        attn = _prompt_attention(proj, bias)
        ssd, st, cs = _prompt_ssd(proj, conv_w, conv_b3, dtb3, alog3, dskip3, ng3, layer)
        x1 = _out_proj(attn, ssd, proj, xp, mod_p, w_oa, w_os, w_o, layer, min(512, l))
        xp, kv_ride, part = _mlp(x1, mod_p, n2, w_u, w_d, fg, layer, min(512, l), final,
                                 ride=(proj_s, caches_t[RIDE_GROUP], bias_dec[RIDE_GROUP], bias_self, kv_ride))
        for g, (win, _) in enumerate(DIL_GROUPS):
            keep = min(win, l)
            kk = proj[:, l - keep:, OFF_K + g * GROUP_W:OFF_K + (g + 1) * GROUP_W]
            vv = proj[:, l - keep:, OFF_V + g * GROUP_W:OFF_V + (g + 1) * GROUP_W]
            kv_p[g].append(jnp.stack([kk, vv], axis=2).reshape(b, keep, 2, HEADS_PER_GROUP, HEAD_DIM))
        ssm_p.append(st.reshape(b, SSD_HEADS, SSD_P, D_STATE))
        conv_p.append(cs)
        attn_s, kv_rest = _decode_attention(proj_s, others, [caches_t[g] for g in others],
                                            [bias_dec[g] for g in others], bias_self, part, kv_rest, layer)
        ssd_s, st_s, cs_s = _decode_ssd(proj_s, conv_state_t, state_r, conv_w, conv_b3, dtb3, alog3,
                                        dskip3, ng3, st_s, layer)
        x1s = _out_proj(attn_s, ssd_s, proj_s, xs, mod_s, w_oa, w_os, w_o, layer, bs)
        xs = _mlp(x1s, mod_s, n2, w_u, w_d, fg, layer, bs, final)
        conv_s.append(jnp.swapaxes(cs_s, 0, 1))

    kv_s = dict(zip(others, kv_rest))
    kv_s[RIDE_GROUP] = kv_ride
    kv_s = [jnp.transpose(kv_s[g], (0, 1, 5, 2, 3, 4)) for g in range(N_GROUPS)]
    return (xp, xs.reshape(bs, 1, d),
            jnp.stack(kv_p[0], axis=0), jnp.stack(kv_p[1], axis=0), jnp.stack(kv_p[2], axis=0),
            jnp.stack(ssm_p, axis=0), jnp.stack(conv_p, axis=0),
            kv_s[0], kv_s[1], kv_s[2],
            st_s.reshape(depth, bs, SSD_HEADS, SSD_P, D_STATE), jnp.stack(conv_s, axis=0))
```

```python
import functools
import math

import jax
import jax.numpy as jnp
import numpy as np
from jax import lax
from jax.experimental import pallas as pl
from jax.experimental.pallas import tpu as pltpu

F32 = jnp.float32
BF16 = jnp.bfloat16

D_MODEL = 1024
HEAD_DIM = 64
HEADS_PER_GROUP = 4
DIL_GROUPS = ((128, 1), (512, 4), (2048, 16))
N_GROUPS = len(DIL_GROUPS)
GROUP_W = HEADS_PER_GROUP * HEAD_DIM
QKV_W = N_GROUPS * GROUP_W
BAND = 128
NUM_BUCKETS = 32
MAX_DISTANCE = 2048
D_INNER = 2 * D_MODEL
SSD_HEADS = 32
SSD_P = 64
SSD_GROUPS = 8
SSD_GROUP_W = D_INNER // SSD_GROUPS
D_STATE = 128
D_CONV = 4
CONV_DIM = D_INNER + 2 * SSD_GROUPS * D_STATE
D_FF = 4 * D_MODEL
EPS = 1e-6
ATTN_SCALE = HEAD_DIM ** -0.5
NEG = -1e30
LOG2_E = 1.4426950408889634

SUBLANES = 8
LANES = 128
V7X_VMEM_LIMIT = 52 * 1024 * 1024

OFF_Z = 0
OFF_GATE = D_INNER
OFF_XBC = 2 * D_INNER
OFF_Q = OFF_XBC + CONV_DIM
OFF_K = OFF_Q + QKV_W
OFF_V = OFF_K + QKV_W
OFF_DT = OFF_V + QKV_W
DT_W = 2 * LANES
PROJ_W = OFF_DT + DT_W

NT_DIMS = (((1,), (1,)), ((), ()))
TN_DIMS = (((0,), (0,)), ((), ()))


def _params(semantics, vmem=V7X_VMEM_LIMIT):
    return pltpu.CompilerParams(dimension_semantics=semantics, vmem_limit_bytes=vmem)


def _sigmoid(x):
    return 0.5 * (jnp.tanh(0.5 * x) + 1.0)


def _silu(x):
    h = 0.5 * x
    return h + h * jnp.tanh(h)


def _softplus(x):
    return jnp.maximum(x, 0.0) + jnp.log(1.0 + jnp.exp(-jnp.abs(x)))


def _rms(x):
    return x * lax.rsqrt(jnp.mean(x * x, axis=-1, keepdims=True) + EPS)


def _ada_kernel(c_ref, w_ref, b_ref, o_ref):
    s = _silu(c_ref[...]).astype(BF16)
    o_ref[...] = jnp.dot(s, w_ref[...].astype(BF16), preferred_element_type=F32) + b_ref[...]


def _ada_mod(c_all, w_ada, b_ada):
    depth, d, n = w_ada.shape
    rows = c_all.shape[0]
    tn = 1536
    return pl.pallas_call(
        _ada_kernel,
        grid=(depth, n // tn),
        in_specs=[
            pl.BlockSpec((rows, d), lambda l, j: (0, 0)),
            pl.BlockSpec((None, d, tn), lambda l, j: (l, 0, j)),
            pl.BlockSpec((None, 1, tn), lambda l, j: (l, 0, j)),
        ],
        out_specs=pl.BlockSpec((None, rows, tn), lambda l, j: (l, 0, j)),
        out_shape=jax.ShapeDtypeStruct((depth, rows, n), F32),
        compiler_params=_params(("arbitrary", "arbitrary")),
        name="ada_mod",
    )(c_all, w_ada, b_ada.reshape(depth, 1, n))


IN_PROJ_TN = 1792


def _in_proj_kernel(x_ref, mod_ref, g_ref, w_ref, o_ref, h_scr, *, tn):
    n = pl.program_id(2)

    @pl.when(n == 0)
    def _():
        y = _rms(x_ref[...]) * g_ref[...]
        h = y * (1.0 + mod_ref[:, D_MODEL:2 * D_MODEL]) + mod_ref[:, 0:D_MODEL]
        h_scr[...] = h.astype(BF16)

    w = w_ref[pl.ds(pl.multiple_of(n * tn, tn), tn), :]
    o_ref[...] = lax.dot_general(h_scr[...], w, NT_DIMS, preferred_element_type=F32)


def _in_proj(x, mod, norm_g, w_in_r, layer, tm):
    bx, lx, d = x.shape
    r = mod.shape[2]
    tn = IN_PROJ_TN
    return pl.pallas_call(
        functools.partial(_in_proj_kernel, tn=tn),
        grid=(bx, lx // tm, PROJ_W // tn),
        in_specs=[
            pl.BlockSpec((None, tm, d), lambda b, i, n: (b, i, 0)),
            pl.BlockSpec((None, None, r, 2 * d), lambda b, i, n: (layer, b, 0, 0)),
            pl.BlockSpec((None, 1, d), lambda b, i, n: (layer, 0, 0)),
            pl.BlockSpec((None, PROJ_W, d), lambda b, i, n: (layer, 0, 0), pipeline_mode=pl.Buffered(1)),
        ],
        out_specs=pl.BlockSpec((None, tm, tn), lambda b, i, n: (b, i, n)),
        out_shape=jax.ShapeDtypeStruct((bx, lx, PROJ_W), F32),
        scratch_shapes=[pltpu.VMEM((tm, d), BF16)],
        compiler_params=_params(("arbitrary", "arbitrary", "arbitrary")),
        name="in_proj",
    )(x, mod, norm_g, w_in_r)


def _t5_bucket_np(dist):
    max_exact = NUM_BUCKETS // 2
    df = np.maximum(dist, 1).astype(np.float32)
    ratio = np.log(df / np.float32(max_exact)) / np.float32(math.log(MAX_DISTANCE / max_exact))
    large = max_exact + (ratio * np.float32(NUM_BUCKETS - max_exact)).astype(np.int32)
    large = np.minimum(large, NUM_BUCKETS - 1)
    return np.where(dist < max_exact, dist, large).astype(np.int32)


def _bucket_tiles():
    q = np.arange(BAND)[:, None]
    c = np.arange(2 * BAND)[None, :]
    j = q + BAND - c
    valid = (j >= 0) & (j <= BAND)
    tiles = []
    for _, dil in DIL_GROUPS:
        b = _t5_bucket_np(np.clip(j, 0, BAND) * dil)
        tiles.append(np.where(valid, b, -1))
    return np.stack(tiles).astype(np.int32)


def _bucket_rows():
    width = max(win for win, _ in DIL_GROUPS)
    i = np.arange(width)
    rows = []
    for win, dil in DIL_GROUPS:
        dist = win - i
        valid = (i < win) & (dist % dil == 0)
        rows.append(np.where(valid, _t5_bucket_np(np.clip(dist, 0, win)), -1))
    return np.broadcast_to(np.stack(rows)[:, None, :], (N_GROUPS, SUBLANES, width)).astype(np.int32)


def _bias_kernel(rb_ref, bk_ref, o_ref):
    hh = pl.program_id(0)
    bk = bk_ref[...]
    acc = jnp.full(bk.shape, NEG, F32)
    for b in range(NUM_BUCKETS):
        acc = jnp.where(bk == b, rb_ref[b, hh], acc)
    o_ref[...] = acc


def _bias_lookup(rel_bias, buckets):
    n_heads = rel_bias.shape[1]
    blk = (None,) + buckets.shape[1:]
    return pl.pallas_call(
        _bias_kernel,
        grid=(n_heads,),
        in_specs=[
            pl.BlockSpec(memory_space=pltpu.SMEM),
            pl.BlockSpec(blk, lambda h: (h // HEADS_PER_GROUP, 0, 0)),
        ],
        out_specs=pl.BlockSpec(blk, lambda h: (h, 0, 0)),
        out_shape=jax.ShapeDtypeStruct((n_heads,) + buckets.shape[1:], F32),
        compiler_params=_params(("arbitrary",)),
        name="bias_lookup",
    )(rel_bias, jnp.asarray(buckets))


HEADS_PER_TILE = LANES // HEAD_DIM
ATTN_ROWS = BAND * max(dil for _, dil in DIL_GROUPS)
ATTN_TILES_IN_FLIGHT = 8
ATTN_MERGE_ROWS = 256


def _attn_kernel(*refs):
    n = N_GROUPS
    bias_ref, o_ref, og_scr, lse_scr = refs[5 * n:]
    first = pl.program_id(1) == 0
    head0 = pl.program_id(2) * HEADS_PER_TILE
    col = lax.broadcasted_iota(jnp.int32, (BAND, 2 * BAND), 1)
    pen_first = jnp.where(col < BAND, jnp.where(first, NEG, 0.0), 0.0)
    lane = lax.broadcasted_iota(jnp.int32, (BAND, LANES), 1)

    for g, (_, dil) in enumerate(DIL_GROUPS):
        q_ref, kp_ref, kc_ref, vp_ref, vc_ref = refs[5 * g:5 * g + 5]
        period = BAND * dil
        periods = ATTN_ROWS // period

        def rows_of(start, dil=dil):
            if dil > 1:
                return pl.ds(start, BAND, stride=dil)
            return pl.ds(pl.multiple_of(start, BAND), BAND)

        def tile(rows, kprev, vprev, leading, g=g, q_ref=q_ref, kc_ref=kc_ref, vc_ref=vc_ref):
            q = q_ref[rows, :]
            kb = jnp.concatenate([kprev, kc_ref[rows, :]], axis=0).astype(BF16)
            vb = jnp.concatenate([vprev, vc_ref[rows, :]], axis=0).astype(BF16)
            o, lse = None, None
            for h in range(HEADS_PER_TILE):
                mine = jnp.logical_and(lane >= h * HEAD_DIM, lane < (h + 1) * HEAD_DIM)
                qh = jnp.where(mine, q, 0.0).astype(BF16)
                s = lax.dot_general(qh, kb, NT_DIMS, preferred_element_type=F32)
                s = s * ATTN_SCALE + bias_ref[HEADS_PER_GROUP * g + head0 + h]
                if leading:
                    s = s + pen_first
                m = jnp.max(s, axis=-1, keepdims=True)
                p = jnp.exp(s - m)
                l = jnp.sum(p, axis=-1, keepdims=True)
                oh = jnp.dot(p.astype(BF16), vb, preferred_element_type=F32) / l
                lh = jnp.broadcast_to(m + jnp.log(l), (BAND, LANES))
                o = oh if h == 0 else jnp.where(mine, oh, o)
                lse = lh if h == 0 else jnp.where(mine, lh, lse)
            og_scr[g, rows, :] = o
            lse_scr[g, rows, :] = lse

        def lead(r, carry, rows_of=rows_of, tile=tile, kp_ref=kp_ref, vp_ref=vp_ref):
            rows = rows_of(r)
            tile(rows, kp_ref[rows, :], vp_ref[rows, :], True)
            return carry

        def body(j, carry, rows_of=rows_of, tile=tile, kc_ref=kc_ref, vc_ref=vc_ref, dil=dil, period=period):
            r, tau = j % dil, 1 + j // dil
            above = rows_of(r + (tau - 1) * period)
            tile(rows_of(r + tau * period), kc_ref[above, :], vc_ref[above, :], False)
            return carry

        lax.fori_loop(0, dil, lead, 0, unroll=min(dil, ATTN_TILES_IN_FLIGHT))
        n_body = dil * (periods - 1)
        if n_body:
            unroll = max(u for u in range(1, ATTN_TILES_IN_FLIGHT + 1) if n_body % u == 0)
            lax.fori_loop(0, n_body, body, 0, unroll=unroll)

    def merge(c, carry):
        rs = pl.ds(pl.multiple_of(c * ATTN_MERGE_ROWS, ATTN_MERGE_ROWS), ATTN_MERGE_ROWS)
        lse = [lse_scr[g, rs, :] for g in range(n)]
        m = functools.reduce(jnp.maximum, lse)
        e = [jnp.exp(v - m) for v in lse]
        o_ref[rs, :] = sum(e[g] * og_scr[g, rs, :] for g in range(n)) / sum(e)
        return carry

    lax.fori_loop(0, ATTN_ROWS // ATTN_MERGE_ROWS, merge, 0)


def _prompt_attention(proj, bias):
    b, l, _ = proj.shape
    tiles = GROUP_W // LANES
    in_specs = []
    for g, (_, dil) in enumerate(DIL_GROUPS):
        period = BAND * dil
        qb, kb, vb = (off // LANES + g * tiles for off in (OFF_Q, OFF_K, OFF_V))
        cur = lambda cb: pl.BlockSpec((None, ATTN_ROWS, LANES), lambda bb, i, t, cb=cb: (bb, i, cb + t))
        prev = lambda cb, per=ATTN_ROWS // period, period=period: pl.BlockSpec(
            (None, period, LANES), lambda bb, i, t: (bb, jnp.maximum(i * per - 1, 0), cb + t))
        in_specs += [cur(qb), prev(kb), cur(kb), prev(vb), cur(vb)]
    in_specs.append(pl.BlockSpec(bias.shape, lambda bb, i, t: (0, 0, 0), pipeline_mode=pl.Buffered(1)))
    return pl.pallas_call(
        _attn_kernel,
        grid=(b, l // ATTN_ROWS, tiles),
        in_specs=in_specs,
        out_specs=pl.BlockSpec((None, ATTN_ROWS, LANES), lambda bb, i, t: (bb, i, t)),
        out_shape=jax.ShapeDtypeStruct((b, l, GROUP_W), F32),
        scratch_shapes=[pltpu.VMEM((N_GROUPS, ATTN_ROWS, LANES), F32)] * 2,
        compiler_params=_params(("arbitrary", "arbitrary", "arbitrary")),
        name="prompt_attn",
    )(*([proj] * (5 * N_GROUPS)), bias)


def _expand4(arr, g, lane):
    rows = arr.shape[0]
    c = [jnp.broadcast_to(arr[:, 4 * g + e:4 * g + e + 1], (rows, SSD_GROUP_W)) for e in range(4)]
    return jnp.where(lane < SSD_P, c[0], jnp.where(lane < 2 * SSD_P, c[1],
                                                    jnp.where(lane < 3 * SSD_P, c[2], c[3])))


def _cumsum_rows(a):
    row = lax.broadcasted_iota(jnp.int32, a.shape, 0)
    s = 1
    while s < a.shape[0]:
        a = a + jnp.where(row >= s, pltpu.roll(a, s, axis=0), 0.0)
        s *= 2
    return a


def _ssd_kernel(xbc_ref, dt_ref, z_ref, cw_ref, cb_ref, dtb_ref, alog_ref, dskip_ref, ng_ref,
                y_ref, st_ref, cs_ref, xp_scr, xc_scr, stt_scr):
    q = BAND
    rows_step = xbc_ref.shape[0]
    c = pl.program_id(1)
    last = c == pl.num_programs(1) - 1

    @pl.when(c == 0)
    def _():
        xp_scr[0:SUBLANES, :] = jnp.zeros((SUBLANES, CONV_DIM), F32)
        stt_scr[...] = jnp.zeros(stt_scr.shape, F32)

    xp_scr[SUBLANES:SUBLANES + rows_step, :] = xbc_ref[...]
    cblk = 512
    nt = rows_step // SUBLANES
    sub = lax.broadcasted_iota(jnp.int32, (nt, SUBLANES, cblk), 1)
    for j in range(CONV_DIM // cblk):
        cs = slice(j * cblk, (j + 1) * cblk)
        x3 = xp_scr[:, cs].reshape(nt + 1, SUBLANES, cblk)
        acc = cb_ref[:, cs] + x3[1:] * cw_ref[D_CONV - 1:D_CONV, cs]
        for k in range(1, D_CONV):
            rot = pltpu.roll(x3, k, axis=1)
            back = jnp.where(sub < k, rot[:-1], rot[1:])
            acc = acc + back * cw_ref[D_CONV - 1 - k:D_CONV - k, cs]
        xc_scr[:, cs] = _silu(acc).reshape(rows_step, cblk)
    xp_scr[0:SUBLANES, :] = xp_scr[rows_step:rows_step + SUBLANES, :]

    for ch in range(rows_step // q):
        _ssd_chunk(slice(ch * q, (ch + 1) * q), xc_scr, dt_ref, z_ref, dtb_ref, alog_ref, dskip_ref, ng_ref,
                   y_ref, stt_scr)

    @pl.when(last)
    def _():
        cs_ref[...] = xp_scr[SUBLANES + rows_step - (D_CONV - 1):SUBLANES + rows_step, :]
        for k in range(D_INNER // LANES):
            st_ref[k * LANES:(k + 1) * LANES, :] = stt_scr[:, k * LANES:(k + 1) * LANES].T


def _ssd_chunk(rows, xc_scr, dt_ref, z_ref, dtb_ref, alog_ref, dskip_ref, ng_ref, y_ref, stt_scr):
    q = BAND
    dt = _softplus(dt_ref[rows, 0:LANES] + dtb_ref[...])
    a = dt * (-jnp.exp(alog_ref[...]))
    acs = _cumsum_rows(a) * LOG2_E
    acs_t = acs.T
    dt_t = dt.T
    ea_last = jnp.exp2(acs[q - 1:q, :])
    w_end_t = dt_t * jnp.exp2(jnp.broadcast_to(acs_t[:, q - 1:q], (q, q)) - acs_t)
    src_t = acs_t - jnp.log2(dt_t)

    row = lax.broadcasted_iota(jnp.int32, (q, q), 0)
    colq = lax.broadcasted_iota(jnp.int32, (q, q), 1)
    tril = row >= colq
    lane = lax.broadcasted_iota(jnp.int32, (q, SSD_GROUP_W), 1)
    lane1 = lax.broadcasted_iota(jnp.int32, (1, SSD_GROUP_W), 1)

    for g in range(SSD_GROUPS):
        gs = slice(g * SSD_GROUP_W, (g + 1) * SSD_GROUP_W)
        bg = xc_scr[rows, D_INNER + g * D_STATE:D_INNER + (g + 1) * D_STATE]
        cg = xc_scr[rows, D_INNER + (SSD_GROUPS + g) * D_STATE:D_INNER + (SSD_GROUPS + g + 1) * D_STATE]
        cbm = lax.dot_general(cg.astype(BF16), bg.astype(BF16), NT_DIMS, preferred_element_type=F32)
        bg_t = bg.T
        xg = xc_scr[rows, gs]
        xgb = xg.astype(BF16)
        stg = stt_scr[:, gs]
        rhs = jnp.concatenate([xgb, stg.astype(BF16)], axis=0)
        y, snew = None, None
        for e in range(4):
            h = 4 * g + e
            col_h = jnp.broadcast_to(acs[:, h:h + 1], (q, q))
            row_h = lambda v: jnp.broadcast_to(v[h:h + 1, :], (q, q))
            intra = cbm * jnp.exp2(jnp.where(tril, col_h - row_h(src_t), NEG))
            inter = cg * jnp.exp2(col_h)
            r = jnp.dot(jnp.concatenate([intra, inter], axis=1).astype(BF16), rhs,
                        preferred_element_type=F32)
            sr = jnp.dot((bg_t * row_h(w_end_t)).astype(BF16), xgb, preferred_element_type=F32)
            mine = lane >= e * SSD_P
            y = r if e == 0 else jnp.where(mine, r, y)
            snew = sr if e == 0 else jnp.where(mine, sr, snew)
        y = y + dskip_ref[:, gs] * xg
        stt_scr[:, gs] = stg * _expand4(ea_last, g, lane1) + snew
        hg = y * _silu(z_ref[rows, gs])
        y_ref[rows, gs] = (_rms(hg) * ng_ref[:, gs]).astype(BF16)


SSD_CHUNKS_PER_STEP = 4


def _prompt_ssd(proj, conv_w, conv_b, dt_bias, a_log, d_skip, norm_g, layer):
    b, l, _ = proj.shape
    q = BAND * SSD_CHUNKS_PER_STEP
    vec = lambda w: pl.BlockSpec((None, 1, w), lambda bb, c: (layer, 0, 0))
    return pl.pallas_call(
        _ssd_kernel,
        grid=(b, l // q),
        in_specs=[
            pl.BlockSpec((None, q, CONV_DIM), lambda bb, c: (bb, c, OFF_XBC // CONV_DIM)),
            pl.BlockSpec((None, q, DT_W), lambda bb, c: (bb, c, OFF_DT // DT_W)),
            pl.BlockSpec((None, q, D_INNER), lambda bb, c: (bb, c, OFF_Z // D_INNER)),
            pl.BlockSpec((None, D_CONV, CONV_DIM), lambda bb, c: (layer, 0, 0)),
            vec(CONV_DIM), vec(LANES), vec(LANES), vec(D_INNER), vec(D_INNER),
        ],
        out_specs=[
            pl.BlockSpec((None, q, D_INNER), lambda bb, c: (bb, c, 0)),
            pl.BlockSpec((None, D_INNER, D_STATE), lambda bb, c: (bb, 0, 0)),
            pl.BlockSpec((None, D_CONV - 1, CONV_DIM), lambda bb, c: (bb, 0, 0)),
        ],
        out_shape=[
            jax.ShapeDtypeStruct((b, l, D_INNER), BF16),
            jax.ShapeDtypeStruct((b, D_INNER, D_STATE), F32),
            jax.ShapeDtypeStruct((b, D_CONV - 1, CONV_DIM), F32),
        ],
        scratch_shapes=[
            pltpu.VMEM((q + SUBLANES, CONV_DIM), F32),
            pltpu.VMEM((q, CONV_DIM), F32),
            pltpu.VMEM((D_STATE, D_INNER), F32),
        ],
        compiler_params=_params(("arbitrary", "arbitrary")),
        name="prompt_ssd",
    )(proj, proj, proj, conv_w, conv_b, dt_bias, a_log, d_skip, norm_g)


def _out_proj_kernel(attn_ref, ssd_ref, gate_ref, x_ref, g1_ref, wa_ref, ws_ref, wo_ref, o_ref):
    pa = jnp.dot(attn_ref[...].astype(BF16), wa_ref[...], preferred_element_type=F32)
    ps = jnp.dot(ssd_ref[...], ws_ref[...], preferred_element_type=F32)
    merged = _sigmoid(gate_ref[:, 0:D_MODEL]) * pa + _sigmoid(gate_ref[:, D_MODEL:2 * D_MODEL]) * ps
    o_ref[...] = x_ref[...] + g1_ref[...] * jnp.dot(
        merged.astype(BF16), wo_ref[...], preferred_element_type=F32)


def _out_proj(attn, ssd, proj, x, mod, w_o_attn, w_o_ssd, w_out, layer, tm):
    bx, lx, d = x.shape
    r = mod.shape[2]
    row = lambda w, cb=0: pl.BlockSpec((None, tm, w), lambda b, i: (b, i, cb))
    wgt = lambda k, n: pl.BlockSpec((None, k, n), lambda b, i: (layer, 0, 0), pipeline_mode=pl.Buffered(1))
    return pl.pallas_call(
        _out_proj_kernel,
        grid=(bx, lx // tm),
        in_specs=[
            row(GROUP_W), row(D_INNER), row(2 * D_MODEL, OFF_GATE // (2 * D_MODEL)), row(d),
            pl.BlockSpec((None, None, r, d), lambda b, i: (layer, b, 0, 2)),
            wgt(GROUP_W, d), wgt(D_INNER, d), wgt(d, d),
        ],
        out_specs=row(d),
        out_shape=jax.ShapeDtypeStruct((bx, lx, d), F32),
        compiler_params=_params(("arbitrary", "arbitrary")),
        name="out_proj",
    )(attn, ssd, proj, x, mod, w_o_attn, w_o_ssd, w_out)


MLP_FF_TILE = 1024


def _mlp_kernel(*refs, final, ride):
    x_ref, mod_ref, g_ref, wu_ref, wd_ref, fg_ref = refs[:6]
    o_ref = refs[-3] if ride else refs[-1]
    x = x_ref[...]
    h = (_rms(x) * g_ref[...]) * (1.0 + mod_ref[:, D_MODEL:2 * D_MODEL]) + mod_ref[:, 0:D_MODEL]
    hb = h.astype(BF16)
    acc = None
    for f in range(D_FF // MLP_FF_TILE):
        fs = slice(f * MLP_FF_TILE, (f + 1) * MLP_FF_TILE)
        u = jnp.maximum(jnp.dot(hb, wu_ref[:, fs], preferred_element_type=F32), 0.0)
        part = jnp.dot((u * u).astype(BF16), wd_ref[fs, :], preferred_element_type=F32)
        acc = part if f == 0 else acc + part
    x2 = x + mod_ref[:, 2 * D_MODEL:3 * D_MODEL] * acc
    if final:
        x2 = _rms(x2) * fg_ref[...]
    o_ref[...] = x2

    if ride:
        proj_ref, c_ref, b_ref, bself_ref = refs[6:10]
        oc_ref, part_ref = refs[-2:]
        g = RIDE_GROUP
        consts = _dec_consts()
        for j in range(proj_ref.shape[0]):
            m, l, out = _dec_group(g, proj_ref[j], c_ref.at[j], oc_ref.at[j], b_ref[...],
                                   bself_ref[g][:, 0:1], consts)
            part_ref[j, :, 0:GROUP_W] = out
            part_ref[j, :, GROUP_W:GROUP_W + LANES] = jnp.broadcast_to(m, (SUBLANES, LANES))
            part_ref[j, :, GROUP_W + LANES:PART_W] = jnp.broadcast_to(l, (SUBLANES, LANES))


def _mlp(x, mod, norm_g, w_up, w_down, final_g, layer, tm, final, ride=None):
    bx, lx, d = x.shape
    r = mod.shape[2]
    nt = lx // tm
    resident = pl.Buffered(1)
    in_specs = [
        pl.BlockSpec((None, tm, d), lambda b, i: (b, i, 0)),
        pl.BlockSpec((None, None, r, 3 * d), lambda b, i: (layer, b, 0, 1)),
        pl.BlockSpec((None, 1, d), lambda b, i: (layer, 0, 0)),
        pl.BlockSpec((None, d, D_FF), lambda b, i: (layer, 0, 0), pipeline_mode=resident),
        pl.BlockSpec((None, D_FF, d), lambda b, i: (layer, 0, 0), pipeline_mode=resident),
        pl.BlockSpec((1, d), lambda b, i: (0, 0)),
    ]
    out_specs = [pl.BlockSpec((None, tm, d), lambda b, i: (b, i, 0))]
    out_shape = [jax.ShapeDtypeStruct((bx, lx, d), F32)]
    args = [x, mod, norm_g, w_up, w_down, final_g]
    aliases = {}
    if ride is not None:
        proj_s, cache, bias, bias_self, prev = ride
        bs = proj_s.shape[1]
        k = bs // (bx * nt)
        assert k * bx * nt == bs
        cspec = pl.BlockSpec((None, k) + cache.shape[2:], lambda b, i: (layer, b * nt + i, 0, 0, 0, 0))
        in_specs += [pl.BlockSpec((k, 1, PROJ_W), lambda b, i: (b * nt + i, 0, 0)), cspec,
                     pl.BlockSpec(bias.shape, lambda b, i: (0, 0)),
                     pl.BlockSpec(bias_self.shape, lambda b, i: (0, 0, 0))]
        args += [proj_s.reshape(bs, 1, PROJ_W), cache, bias, bias_self]
        if prev is not None:
            in_specs.append(pl.BlockSpec(memory_space=pl.ANY))
            aliases = {len(args): 1}
            args.append(prev)
        out_specs += [cspec, pl.BlockSpec((k, SUBLANES, PART_W), lambda b, i: (b * nt + i, 0, 0))]
        out_shape += [jax.ShapeDtypeStruct(cache.shape, cache.dtype),
                      jax.ShapeDtypeStruct((bs, SUBLANES, PART_W), F32)]
    out = pl.pallas_call(
        functools.partial(_mlp_kernel, final=final, ride=ride is not None),
        grid=(bx, nt),
        in_specs=in_specs,
        out_specs=out_specs,
        out_shape=out_shape,
        input_output_aliases=aliases,
        compiler_params=_params(("arbitrary", "arbitrary")),
        name="mlp",
    )(*args)
    return out if ride is not None else out[0]


RIDE_GROUP = N_GROUPS - 1
PART_W = GROUP_W + 2 * LANES


def _dec_consts():
    row = lax.broadcasted_iota(jnp.int32, (SUBLANES, GROUP_W), 0)
    lane = lax.broadcasted_iota(jnp.int32, (SUBLANES, GROUP_W), 1)
    hmask = jnp.logical_and(lane >= row * HEAD_DIM, lane < (row + 1) * HEAD_DIM)
    sel = jnp.logical_or(jnp.logical_and(row == 0, lane < LANES),
                         jnp.logical_and(row == 1, lane >= LANES)).astype(F32)
    last_lane = lax.broadcasted_iota(jnp.int32, (GROUP_W, LANES), 1) == LANES - 1
    return row, hmask, sel, last_lane


def _dec_group(g, proj_row, c_ref, oc_ref, bias, bias_self, consts):
    row, hmask, sel, last_lane = consts
    lb = c_ref.shape[-1]
    qv = proj_row[:, OFF_Q + g * GROUP_W:OFF_Q + (g + 1) * GROUP_W]
    kn = proj_row[:, OFF_K + g * GROUP_W:OFF_K + (g + 1) * GROUP_W]
    vn = proj_row[:, OFF_V + g * GROUP_W:OFF_V + (g + 1) * GROUP_W]
    qbd = jnp.where(hmask, jnp.broadcast_to(qv, (SUBLANES, GROUP_W)), 0.0)
    kt = jnp.concatenate([c_ref[0, h] for h in range(HEADS_PER_GROUP)], axis=0)
    vt = jnp.concatenate([c_ref[1, h] for h in range(HEADS_PER_GROUP)], axis=0)
    s = jnp.dot(qbd.astype(BF16), kt.astype(BF16), preferred_element_type=F32)
    s = s * ATTN_SCALE + bias
    s_self = jnp.sum(qbd * kn, axis=-1, keepdims=True) * ATTN_SCALE + bias_self
    m = jnp.maximum(jnp.max(s, axis=-1, keepdims=True), s_self)
    p = jnp.exp(s - m)
    p_self = jnp.exp(s_self - m)
    l = jnp.sum(p, axis=-1, keepdims=True) + p_self
    acc = lax.dot_general(p.astype(BF16), vt.astype(BF16), NT_DIMS,
                          preferred_element_type=F32) + p_self * vn
    rows2 = jnp.where(row == 0, jnp.broadcast_to(kn, (SUBLANES, GROUP_W)),
                      jnp.where(row == 1, jnp.broadcast_to(vn, (SUBLANES, GROUP_W)), 0.0))
    cols = lax.dot_general(rows2, sel, TN_DIMS, precision=lax.Precision.HIGHEST,
                           preferred_element_type=F32)
    for kv, t in ((0, kt), (1, vt)):
        rolled = pltpu.roll(t, lb - 1, axis=1)
        tail = jnp.where(last_lane, cols[:, kv * LANES:(kv + 1) * LANES], rolled[:, lb - LANES:lb])
        new = tail if lb == LANES else jnp.concatenate([rolled[:, 0:lb - LANES], tail], axis=1)
        for h in range(HEADS_PER_GROUP):
            oc_ref[kv, h] = new[h * HEAD_DIM:(h + 1) * HEAD_DIM, :]
    return m, l, acc


def _dec_attn_kernel(*refs, groups, n_alias):
    n = len(groups)
    proj_ref, c_refs, b_refs = refs[0], refs[1:1 + n], refs[1 + n:1 + 2 * n]
    bself_ref, part_ref = refs[1 + 2 * n], refs[2 + 2 * n]
    o_ref, oc_refs = refs[3 + 2 * n + n_alias], refs[4 + 2 * n + n_alias:4 + 3 * n + n_alias]
    consts = _dec_consts()
    hmask = consts[1]
    parts = [_dec_group(g, proj_ref[...], c_refs[i], oc_refs[i], b_refs[i][...], bself_ref[g][:, 0:1], consts)
             for i, g in enumerate(groups)]
    parts.append((part_ref[:, GROUP_W:GROUP_W + 1], part_ref[:, GROUP_W + LANES:GROUP_W + LANES + 1],
                  part_ref[:, 0:GROUP_W]))
    m_all = functools.reduce(jnp.maximum, [p[0] for p in parts])
    num = sum(jnp.exp(m - m_all) * acc for m, _, acc in parts)
    den = sum(jnp.exp(m - m_all) * l for m, l, _ in parts)
    o_ref[...] = jnp.sum(jnp.where(hmask, num / den, 0.0), axis=0, keepdims=True)


def _decode_attention(proj_s, groups, caches_t, bias_dec, bias_self, part, prev_out, layer):
    bs = proj_s.shape[1]
    n = len(groups)
    cspec = lambda c: pl.BlockSpec((None, None) + c.shape[2:], lambda b: (layer, b, 0, 0, 0, 0))
    in_specs = [pl.BlockSpec((None, 1, PROJ_W), lambda b: (b, 0, 0))]
    in_specs += [cspec(c) for c in caches_t]
    in_specs += [pl.BlockSpec(bd.shape, lambda b: (0, 0)) for bd in bias_dec]
    in_specs += [pl.BlockSpec(bias_self.shape, lambda b: (0, 0, 0)),
                 pl.BlockSpec((None, SUBLANES, PART_W), lambda b: (b, 0, 0))]
    args = [proj_s.reshape(bs, 1, PROJ_W), *caches_t, *bias_dec, bias_self, part]
    aliases = {}
    if prev_out is not None:
        in_specs += [pl.BlockSpec(memory_space=pl.ANY)] * n
        aliases = {len(args) + j: 1 + j for j in range(n)}
        args += list(prev_out)
    out = pl.pallas_call(
        functools.partial(_dec_attn_kernel, groups=tuple(groups), n_alias=len(aliases)),
        grid=(bs,),
        in_specs=in_specs,
        out_specs=[pl.BlockSpec((None, 1, GROUP_W), lambda b: (b, 0, 0))] + [cspec(c) for c in caches_t],
        out_shape=[jax.ShapeDtypeStruct((bs, 1, GROUP_W), F32)] + [
            jax.ShapeDtypeStruct(c.shape, c.dtype) for c in caches_t],
        input_output_aliases=aliases,
        compiler_params=_params(("arbitrary",)),
        name="decode_attn",
    )(*args)
    return out[0].reshape(1, bs, GROUP_W), out[1:]


def _dec_ssd_kernel(*refs, n_alias):
    proj_ref, cs_ref, st_ref, cw_ref, cb_ref, dtb_ref, alog_ref, dskip_ref, ng_ref = refs[:9]
    y_ref, nst_ref, ncs_ref, xc_scr, xdt_t_scr, da_t_scr, ct_scr, yt_scr = refs[9 + n_alias:]
    b = pl.program_id(0)
    bs = proj_ref.shape[0]
    nblk = D_INNER // LANES
    pad_rows = lambda v: jnp.concatenate([v, jnp.zeros((LANES - bs, v.shape[1]), v.dtype)], axis=0)

    @pl.when(b == 0)
    def _():
        xr = proj_ref[:, OFF_XBC:OFF_XBC + CONV_DIM]
        acc = cb_ref[...]
        for t in range(D_CONV - 1):
            acc = acc + cs_ref[t] * cw_ref[t:t + 1, :]
        acc = acc + xr * cw_ref[D_CONV - 1:D_CONV, :]
        xc = pad_rows(_silu(acc))
        xc_scr[...] = xc
        for t in range(D_CONV - 2):
            ncs_ref[t] = cs_ref[t + 1]
        ncs_ref[D_CONV - 2] = xr
        dt = _softplus(proj_ref[:, OFF_DT:OFF_DT + LANES] + dtb_ref[...])
        da = jnp.exp(dt * (-jnp.exp(alog_ref[...])))
        onehot = (lax.shift_right_logical(lax.broadcasted_iota(jnp.int32, (LANES, D_INNER), 1), 6)
                  == lax.broadcasted_iota(jnp.int32, (LANES, D_INNER), 0)).astype(F32)
        expand = lambda v: pad_rows(jnp.dot(v, onehot, precision=lax.Precision.HIGHEST,
                                            preferred_element_type=F32))
        xdt = xc[:, 0:D_INNER] * expand(dt)
        dae = expand(da)
        for k in range(nblk):
            ks = slice(k * LANES, (k + 1) * LANES)
            xdt_t_scr[ks, :] = xdt[:, ks].T.astype(BF16)
            da_t = dae[:, ks].T
            da_hi = da_t.astype(BF16)
            da_t_scr[ks, 0:LANES] = da_hi
            da_t_scr[ks, LANES:2 * LANES] = (da_t - da_hi.astype(F32)).astype(BF16)
        for g in range(SSD_GROUPS):
            ct_scr[g] = xc[:, D_INNER + (SSD_GROUPS + g) * D_STATE:D_INNER + (SSD_GROUPS + g + 1) * D_STATE].T
        yt_scr[...] = jnp.zeros(yt_scr.shape, F32)

    row = lax.broadcasted_iota(jnp.int32, (LANES, LANES), 0)
    lane = lax.broadcasted_iota(jnp.int32, (LANES, LANES), 1)
    pick = jnp.where(row == b, 1.0, 0.0).astype(BF16)
    decay = jnp.dot(da_t_scr[...], jnp.concatenate([pick, pick], axis=0), preferred_element_type=F32)
    for g in range(SSD_GROUPS):
        rs = slice(g * SSD_GROUP_W, (g + 1) * SSD_GROUP_W)
        bg = xc_scr[:, D_INNER + g * D_STATE:D_INNER + (g + 1) * D_STATE]
        b_sel = jnp.where(row == b, bg, 0.0).astype(BF16)
        hn = st_ref[rs, :] * decay[rs, :] + jnp.dot(xdt_t_scr[rs, :], b_sel, preferred_element_type=F32)
        nst_ref[rs, :] = hn
        c_sel = jnp.where(lane == b, ct_scr[g], 0.0).astype(BF16)
        yt_scr[rs, :] += jnp.dot(hn.astype(BF16), c_sel, preferred_element_type=F32)

    @pl.when(b == pl.num_programs(0) - 1)
    def _():
        for g in range(SSD_GROUPS):
            gs = slice(g * SSD_GROUP_W, (g + 1) * SSD_GROUP_W)
            yg = jnp.concatenate([yt_scr[k * LANES:(k + 1) * LANES, :].T[0:bs, :]
                                  for k in range(g * SSD_GROUP_W // LANES, (g + 1) * SSD_GROUP_W // LANES)],
                                 axis=1)
            y = yg + dskip_ref[:, gs] * xc_scr[0:bs, gs]
            hg = y * _silu(proj_ref[:, OFF_Z + g * SSD_GROUP_W:OFF_Z + (g + 1) * SSD_GROUP_W])
            y_ref[:, gs] = (_rms(hg) * ng_ref[:, gs]).astype(BF16)


def _decode_ssd(proj_s, conv_state_t, state, conv_w, conv_b, dt_bias, a_log, d_skip, norm_g, prev_state, layer):
    bs = proj_s.shape[1]
    vec = lambda w: pl.BlockSpec((None, 1, w), lambda b: (layer, 0, 0))
    st_spec = pl.BlockSpec((None, None, D_INNER, D_STATE), lambda b: (layer, b, 0, 0))
    args = [proj_s, conv_state_t, state, conv_w, conv_b, dt_bias, a_log, d_skip, norm_g]
    extra, aliases = [], {}
    if prev_state is not None:
        extra, aliases = [pl.BlockSpec(memory_space=pl.ANY)], {len(args): 1}
        args.append(prev_state)
    return pl.pallas_call(
        functools.partial(_dec_ssd_kernel, n_alias=len(aliases)),
        grid=(bs,),
        in_specs=[
            pl.BlockSpec((None, bs, PROJ_W), lambda b: (0, 0, 0)),
            pl.BlockSpec((None, D_CONV - 1, bs, CONV_DIM), lambda b: (layer, 0, 0, 0)),
            st_spec,
            pl.BlockSpec((None, D_CONV, CONV_DIM), lambda b: (layer, 0, 0)),
            vec(CONV_DIM), vec(LANES), vec(LANES), vec(D_INNER), vec(D_INNER),
        ] + extra,
        out_specs=[
            pl.BlockSpec((None, bs, D_INNER), lambda b: (0, 0, 0)),
            st_spec,
            pl.BlockSpec((D_CONV - 1, bs, CONV_DIM), lambda b: (0, 0, 0)),
        ],
        out_shape=[
            jax.ShapeDtypeStruct((1, bs, D_INNER), BF16),
            jax.ShapeDtypeStruct(state.shape, F32),
            jax.ShapeDtypeStruct((D_CONV - 1, bs, CONV_DIM), F32),
        ],
        input_output_aliases=aliases,
        scratch_shapes=[
            pltpu.VMEM((LANES, CONV_DIM), F32),
            pltpu.VMEM((D_INNER, LANES), BF16),
            pltpu.VMEM((D_INNER, 2 * LANES), BF16),
            pltpu.VMEM((SSD_GROUPS, D_STATE, LANES), F32),
            pltpu.VMEM((D_INNER, LANES), F32),
        ],
        compiler_params=_params(("arbitrary",)),
        name="decode_ssd",
    )(*args)


def _reorder_w_in(w_in):
    q0, z0, x0, d0, g0 = 0, 3 * QKV_W, 3 * QKV_W + D_INNER, 3 * QKV_W + D_INNER + CONV_DIM, \
        3 * QKV_W + D_INNER + CONV_DIM + SSD_HEADS
    wt = jnp.swapaxes(w_in, 1, 2)
    pieces = [wt[:, z0:x0], wt[:, g0:g0 + 2 * D_MODEL], wt[:, x0:d0], wt[:, q0:z0], wt[:, d0:g0],
              jnp.zeros((wt.shape[0], DT_W - SSD_HEADS, wt.shape[2]), wt.dtype)]
    return jnp.concatenate(pieces, axis=1).astype(BF16)


def _pad_lanes(v, width):
    return jnp.pad(v, ((0, 0), (0, width - v.shape[-1])))[:, None, :]


def kernel(x_prompt, x_sample, cache_kv_g0, cache_kv_g1, cache_kv_g2, state_ssm, state_conv, c_prompt,
           c_sample, rel_bias, w_ada, b_ada, norm1_g, norm2_g, w_in, conv_w, conv_b, dt_bias, a_log, d_skip,
           ssd_norm_g, w_o_attn, w_o_ssd, w_out, w_up, w_down, final_g):
    depth = w_in.shape[0]
    b, l, d = x_prompt.shape
    bs = x_sample.shape[0]
    caches = (cache_kv_g0, cache_kv_g1, cache_kv_g2)
    assert d == D_MODEL and x_sample.shape[1] == 1 and l % ATTN_ROWS == 0 and PROJ_W % IN_PROJ_TN == 0
    assert all(win == BAND * dil for win, dil in DIL_GROUPS)
    assert all(c.shape[2] == win for c, (win, _) in zip(caches, DIL_GROUPS))
    assert bs % SUBLANES == 0 and bs <= LANES

    w_in_r = _reorder_w_in(w_in)
    w_oa, w_os, w_o, w_u, w_d = (w.astype(BF16) for w in (w_o_attn, w_o_ssd, w_out, w_up, w_down))
    n1, n2 = norm1_g[:, None, :], norm2_g[:, None, :]
    conv_b3, ng3 = conv_b[:, None, :], ssd_norm_g[:, None, :]
    dtb3, alog3 = _pad_lanes(dt_bias, LANES), _pad_lanes(a_log, LANES)
    dskip3 = jnp.repeat(d_skip, SSD_P, axis=-1)[:, None, :]
    fg = final_g[None, :]

    rows = b + bs
    rows_pad = -(-rows // SUBLANES) * SUBLANES
    c_all = jnp.pad(jnp.concatenate([c_prompt, c_sample], axis=0), ((0, rows_pad - rows), (0, 0)))
    mod = _ada_mod(c_all, w_ada, b_ada)
    mod_p = mod[:, :b, None, :]
    mod_s = mod[:, None, b:rows, :]

    bias = _bias_lookup(rel_bias, _bucket_tiles())
    head_rows = ((0, SUBLANES - HEADS_PER_GROUP), (0, 0))
    bias_rows = _bias_lookup(rel_bias, _bucket_rows())[:, 0, :]
    bias_dec = [jnp.pad(bias_rows[HEADS_PER_GROUP * g:HEADS_PER_GROUP * (g + 1), :win], head_rows)
                for g, (win, _) in enumerate(DIL_GROUPS)]
    self_bias = bias[:, 0, BAND].reshape(N_GROUPS, HEADS_PER_GROUP, 1)
    bias_self = jnp.pad(jnp.broadcast_to(self_bias, (N_GROUPS, HEADS_PER_GROUP, LANES)), ((0, 0),) + head_rows)

    xp = x_prompt
    xs = x_sample.reshape(1, bs, d)
    conv_state_t = jnp.swapaxes(state_conv, 1, 2)
    state_r = state_ssm.reshape(depth, bs, D_INNER, D_STATE)
    caches_t = [jnp.transpose(c, (0, 1, 3, 4, 5, 2)) for c in caches]
    tm_p = 1024 if l % 1024 == 0 else BAND

    kv_p = [[] for _ in DIL_GROUPS]
    ssm_p, conv_p, conv_s = [], [], []
    others = [g for g in range(N_GROUPS) if g != RIDE_GROUP]
    kv_ride, kv_rest, st_s = None, None, None
    for layer in range(depth):
        final = layer == depth - 1
        proj_s = _in_proj(xs, mod_s, n1, w_in_r, layer, bs)
        proj = _in_proj(xp, mod_p, n1, w_in_r, layer, tm_p)
        attn = _prompt_attention(proj, bias)
        ssd, st, cs = _prompt_ssd(proj, conv_w, conv_b3, dtb3, alog3, dskip3, ng3, layer)
        x1 = _out_proj(attn, ssd, proj, xp, mod_p, w_oa, w_os, w_o, layer, min(512, l))
        xp, kv_ride, part = _mlp(x1, mod_p, n2, w_u, w_d, fg, layer, min(512, l), final,
                                 ride=(proj_s, caches_t[RIDE_GROUP], bias_dec[RIDE_GROUP], bias_self, kv_ride))
        for g, (win, _) in enumerate(DIL_GROUPS):
            keep = min(win, l)
            kk = proj[:, l - keep:, OFF_K + g * GROUP_W:OFF_K + (g + 1) * GROUP_W]
            vv = proj[:, l - keep:, OFF_V + g * GROUP_W:OFF_V + (g + 1) * GROUP_W]
            kv_p[g].append(jnp.stack([kk, vv], axis=2).reshape(b, keep, 2, HEADS_PER_GROUP, HEAD_DIM))
        ssm_p.append(st.reshape(b, SSD_HEADS, SSD_P, D_STATE))
        conv_p.append(cs)
        attn_s, kv_rest = _decode_attention(proj_s, others, [caches_t[g] for g in others],
                                            [bias_dec[g] for g in others], bias_self, part, kv_rest, layer)
        ssd_s, st_s, cs_s = _decode_ssd(proj_s, conv_state_t, state_r, conv_w, conv_b3, dtb3, alog3,
                                        dskip3, ng3, st_s, layer)
        x1s = _out_proj(attn_s, ssd_s, proj_s, xs, mod_s, w_oa, w_os, w_o, layer, bs)
        xs = _mlp(x1s, mod_s, n2, w_u, w_d, fg, layer, bs, final)
        conv_s.append(jnp.swapaxes(cs_s, 0, 1))

    kv_s = dict(zip(others, kv_rest))
    kv_s[RIDE_GROUP] = kv_ride
    kv_s = [jnp.transpose(kv_s[g], (0, 1, 5, 2, 3, 4)) for g in range(N_GROUPS)]
    return (xp, xs.reshape(bs, 1, d),
            jnp.stack(kv_p[0], axis=0), jnp.stack(kv_p[1], axis=0), jnp.stack(kv_p[2], axis=0),
            jnp.stack(ssm_p, axis=0), jnp.stack(conv_p, axis=0),
            kv_s[0], kv_s[1], kv_s[2],
            st_s.reshape(depth, bs, SSD_HEADS, SSD_P, D_STATE), jnp.stack(conv_s, axis=0))
```

```python
import functools
import math

import jax
import jax.numpy as jnp
import numpy as np
from jax import lax
from jax.experimental import pallas as pl
from jax.experimental.pallas import tpu as pltpu

F32 = jnp.float32
BF16 = jnp.bfloat16

D_MODEL = 1024
HEAD_DIM = 64
HEADS_PER_GROUP = 4
DIL_GROUPS = ((128, 1), (512, 4), (2048, 16))
N_GROUPS = len(DIL_GROUPS)
GROUP_W = HEADS_PER_GROUP * HEAD_DIM
QKV_W = N_GROUPS * GROUP_W
BAND = 128
NUM_BUCKETS = 32
MAX_DISTANCE = 2048
D_INNER = 2 * D_MODEL
SSD_HEADS = 32
SSD_P = 64
SSD_GROUPS = 8
SSD_GROUP_W = D_INNER // SSD_GROUPS
D_STATE = 128
D_CONV = 4
CONV_DIM = D_INNER + 2 * SSD_GROUPS * D_STATE
D_FF = 4 * D_MODEL
EPS = 1e-6
ATTN_SCALE = HEAD_DIM ** -0.5
NEG = -1e30
LOG2_E = 1.4426950408889634

SUBLANES = 8
LANES = 128
V7X_VMEM_LIMIT = 52 * 1024 * 1024

OFF_Z = 0
OFF_GATE = D_INNER
OFF_XBC = 2 * D_INNER
OFF_Q = OFF_XBC + CONV_DIM
OFF_K = OFF_Q + QKV_W
OFF_V = OFF_K + QKV_W
OFF_DT = OFF_V + QKV_W
DT_W = 2 * LANES
PROJ_W = OFF_DT + DT_W

NT_DIMS = (((1,), (1,)), ((), ()))
TN_DIMS = (((0,), (0,)), ((), ()))


def _params(semantics, vmem=V7X_VMEM_LIMIT):
    return pltpu.CompilerParams(dimension_semantics=semantics, vmem_limit_bytes=vmem)


def _sigmoid(x):
    return 0.5 * (jnp.tanh(0.5 * x) + 1.0)


def _silu(x):
    h = 0.5 * x
    return h + h * jnp.tanh(h)


def _softplus(x):
    return jnp.maximum(x, 0.0) + jnp.log(1.0 + jnp.exp(-jnp.abs(x)))


def _rms(x):
    return x * lax.rsqrt(jnp.mean(x * x, axis=-1, keepdims=True) + EPS)


def _ada_kernel(c_ref, w_ref, b_ref, o_ref):
    s = _silu(c_ref[...]).astype(BF16)
    o_ref[...] = jnp.dot(s, w_ref[...].astype(BF16), preferred_element_type=F32) + b_ref[...]


def _ada_mod(c_all, w_ada, b_ada):
    depth, d, n = w_ada.shape
    rows = c_all.shape[0]
    tn = 1536
    return pl.pallas_call(
        _ada_kernel,
        grid=(depth, n // tn),
        in_specs=[
            pl.BlockSpec((rows, d), lambda l, j: (0, 0)),
            pl.BlockSpec((None, d, tn), lambda l, j: (l, 0, j)),
            pl.BlockSpec((None, 1, tn), lambda l, j: (l, 0, j)),
        ],
        out_specs=pl.BlockSpec((None, rows, tn), lambda l, j: (l, 0, j)),
        out_shape=jax.ShapeDtypeStruct((depth, rows, n), F32),
        compiler_params=_params(("arbitrary", "arbitrary")),
        name="ada_mod",
    )(c_all, w_ada, b_ada.reshape(depth, 1, n))


IN_PROJ_TN = 1792


def _in_proj_kernel(x_ref, mod_ref, g_ref, w_ref, o_ref, h_scr, *, tn):
    n = pl.program_id(2)

    @pl.when(n == 0)
    def _():
        y = _rms(x_ref[...]) * g_ref[...]
        h = y * (1.0 + mod_ref[:, D_MODEL:2 * D_MODEL]) + mod_ref[:, 0:D_MODEL]
        h_scr[...] = h.astype(BF16)

    w = w_ref[pl.ds(pl.multiple_of(n * tn, tn), tn), :]
    o_ref[...] = lax.dot_general(h_scr[...], w, NT_DIMS, preferred_element_type=F32)


def _in_proj(x, mod, norm_g, w_in_r, layer, tm):
    bx, lx, d = x.shape
    r = mod.shape[2]
    tn = IN_PROJ_TN
    return pl.pallas_call(
        functools.partial(_in_proj_kernel, tn=tn),
        grid=(bx, lx // tm, PROJ_W // tn),
        in_specs=[
            pl.BlockSpec((None, tm, d), lambda b, i, n: (b, i, 0)),
            pl.BlockSpec((None, None, r, 2 * d), lambda b, i, n: (layer, b, 0, 0)),
            pl.BlockSpec((None, 1, d), lambda b, i, n: (layer, 0, 0)),
            pl.BlockSpec((None, PROJ_W, d), lambda b, i, n: (layer, 0, 0), pipeline_mode=pl.Buffered(1)),
        ],
        out_specs=pl.BlockSpec((None, tm, tn), lambda b, i, n: (b, i, n)),
        out_shape=jax.ShapeDtypeStruct((bx, lx, PROJ_W), F32),
        scratch_shapes=[pltpu.VMEM((tm, d), BF16)],
        compiler_params=_params(("arbitrary", "arbitrary", "arbitrary")),
        name="in_proj",
    )(x, mod, norm_g, w_in_r)


def _t5_bucket_np(dist):
    max_exact = NUM_BUCKETS // 2
    df = np.maximum(dist, 1).astype(np.float32)
    ratio = np.log(df / np.float32(max_exact)) / np.float32(math.log(MAX_DISTANCE / max_exact))
    large = max_exact + (ratio * np.float32(NUM_BUCKETS - max_exact)).astype(np.int32)
    large = np.minimum(large, NUM_BUCKETS - 1)
    return np.where(dist < max_exact, dist, large).astype(np.int32)


def _bucket_tiles():
    q = np.arange(BAND)[:, None]
    c = np.arange(2 * BAND)[None, :]
    j = q + BAND - c
    valid = (j >= 0) & (j <= BAND)
    tiles = []
    for _, dil in DIL_GROUPS:
        b = _t5_bucket_np(np.clip(j, 0, BAND) * dil)
        tiles.append(np.where(valid, b, -1))
    return np.stack(tiles).astype(np.int32)


def _bucket_rows():
    width = max(win for win, _ in DIL_GROUPS)
    i = np.arange(width)
    rows = []
    for win, dil in DIL_GROUPS:
        dist = win - i
        valid = (i < win) & (dist % dil == 0)
        rows.append(np.where(valid, _t5_bucket_np(np.clip(dist, 0, win)), -1))
    return np.broadcast_to(np.stack(rows)[:, None, :], (N_GROUPS, SUBLANES, width)).astype(np.int32)


def _bias_kernel(rb_ref, bk_ref, o_ref):
    hh = pl.program_id(0)
    bk = bk_ref[...]
    acc = jnp.full(bk.shape, NEG, F32)
    for b in range(NUM_BUCKETS):
        acc = jnp.where(bk == b, rb_ref[b, hh], acc)
    o_ref[...] = acc


def _bias_lookup(rel_bias, buckets):
    n_heads = rel_bias.shape[1]
    blk = (None,) + buckets.shape[1:]
    return pl.pallas_call(
        _bias_kernel,
        grid=(n_heads,),
        in_specs=[
            pl.BlockSpec(memory_space=pltpu.SMEM),
            pl.BlockSpec(blk, lambda h: (h // HEADS_PER_GROUP, 0, 0)),
        ],
        out_specs=pl.BlockSpec(blk, lambda h: (h, 0, 0)),
        out_shape=jax.ShapeDtypeStruct((n_heads,) + buckets.shape[1:], F32),
        compiler_params=_params(("arbitrary",)),
        name="bias_lookup",
    )(rel_bias, jnp.asarray(buckets))


HEADS_PER_TILE = LANES // HEAD_DIM
ATTN_ROWS = BAND * max(dil for _, dil in DIL_GROUPS)
ATTN_TILES_IN_FLIGHT = 8
ATTN_MERGE_ROWS = 256


def _attn_kernel(*refs):
    n = N_GROUPS
    bias_ref, o_ref, og_scr, lse_scr = refs[5 * n:]
    first = pl.program_id(1) == 0
    head0 = pl.program_id(2) * HEADS_PER_TILE
    col = lax.broadcasted_iota(jnp.int32, (BAND, 2 * BAND), 1)
    pen_first = jnp.where(col < BAND, jnp.where(first, NEG, 0.0), 0.0)
    lane = lax.broadcasted_iota(jnp.int32, (BAND, LANES), 1)

    for g, (_, dil) in enumerate(DIL_GROUPS):
        q_ref, kp_ref, kc_ref, vp_ref, vc_ref = refs[5 * g:5 * g + 5]
        period = BAND * dil
        periods = ATTN_ROWS // period

        def rows_of(start, dil=dil):
            if dil > 1:
                return pl.ds(start, BAND, stride=dil)
            return pl.ds(pl.multiple_of(start, BAND), BAND)

        def tile(rows, kprev, vprev, leading, g=g, q_ref=q_ref, kc_ref=kc_ref, vc_ref=vc_ref):
            q = q_ref[rows, :]
            kb = jnp.concatenate([kprev, kc_ref[rows, :]], axis=0).astype(BF16)
            vb = jnp.concatenate([vprev, vc_ref[rows, :]], axis=0).astype(BF16)
            o, lse = None, None
            for h in range(HEADS_PER_TILE):
                mine = jnp.logical_and(lane >= h * HEAD_DIM, lane < (h + 1) * HEAD_DIM)
                qh = jnp.where(mine, q, 0.0).astype(BF16)
                s = lax.dot_general(qh, kb, NT_DIMS, preferred_element_type=F32)
                s = s * ATTN_SCALE + bias_ref[HEADS_PER_GROUP * g + head0 + h]
                if leading:
                    s = s + pen_first
                m = jnp.max(s, axis=-1, keepdims=True)
                p = jnp.exp(s - m)
                l = jnp.sum(p, axis=-1, keepdims=True)
                oh = jnp.dot(p.astype(BF16), vb, preferred_element_type=F32) / l
                lh = jnp.broadcast_to(m + jnp.log(l), (BAND, LANES))
                o = oh if h == 0 else jnp.where(mine, oh, o)
                lse = lh if h == 0 else jnp.where(mine, lh, lse)
            og_scr[g, rows, :] = o
            lse_scr[g, rows, :] = lse

        def lead(r, carry, rows_of=rows_of, tile=tile, kp_ref=kp_ref, vp_ref=vp_ref):
            rows = rows_of(r)
            tile(rows, kp_ref[rows, :], vp_ref[rows, :], True)
            return carry

        def body(j, carry, rows_of=rows_of, tile=tile, kc_ref=kc_ref, vc_ref=vc_ref, dil=dil, period=period):
            r, tau = j % dil, 1 + j // dil
            above = rows_of(r + (tau - 1) * period)
            tile(rows_of(r + tau * period), kc_ref[above, :], vc_ref[above, :], False)
            return carry

        lax.fori_loop(0, dil, lead, 0, unroll=min(dil, ATTN_TILES_IN_FLIGHT))
        n_body = dil * (periods - 1)
        if n_body:
            unroll = max(u for u in range(1, ATTN_TILES_IN_FLIGHT + 1) if n_body % u == 0)
            lax.fori_loop(0, n_body, body, 0, unroll=unroll)

    def merge(c, carry):
        rs = pl.ds(pl.multiple_of(c * ATTN_MERGE_ROWS, ATTN_MERGE_ROWS), ATTN_MERGE_ROWS)
        lse = [lse_scr[g, rs, :] for g in range(n)]
        m = functools.reduce(jnp.maximum, lse)
        e = [jnp.exp(v - m) for v in lse]
        o_ref[rs, :] = sum(e[g] * og_scr[g, rs, :] for g in range(n)) / sum(e)
        return carry

    lax.fori_loop(0, ATTN_ROWS // ATTN_MERGE_ROWS, merge, 0)


def _prompt_attention(proj, bias):
    b, l, _ = proj.shape
    tiles = GROUP_W // LANES
    in_specs = []
    for g, (_, dil) in enumerate(DIL_GROUPS):
        period = BAND * dil
        qb, kb, vb = (off // LANES + g * tiles for off in (OFF_Q, OFF_K, OFF_V))
        cur = lambda cb: pl.BlockSpec((None, ATTN_ROWS, LANES), lambda bb, i, t, cb=cb: (bb, i, cb + t))
        prev = lambda cb, per=ATTN_ROWS // period, period=period: pl.BlockSpec(
            (None, period, LANES), lambda bb, i, t: (bb, jnp.maximum(i * per - 1, 0), cb + t))
        in_specs += [cur(qb), prev(kb), cur(kb), prev(vb), cur(vb)]
    in_specs.append(pl.BlockSpec(bias.shape, lambda bb, i, t: (0, 0, 0), pipeline_mode=pl.Buffered(1)))
    return pl.pallas_call(
        _attn_kernel,
        grid=(b, l // ATTN_ROWS, tiles),
        in_specs=in_specs,
        out_specs=pl.BlockSpec((None, ATTN_ROWS, LANES), lambda bb, i, t: (bb, i, t)),
        out_shape=jax.ShapeDtypeStruct((b, l, GROUP_W), F32),
        scratch_shapes=[pltpu.VMEM((N_GROUPS, ATTN_ROWS, LANES), F32)] * 2,
        compiler_params=_params(("arbitrary", "arbitrary", "arbitrary")),
        name="prompt_attn",
    )(*([proj] * (5 * N_GROUPS)), bias)


def _expand4(arr, g, lane):
    rows = arr.shape[0]
    c = [jnp.broadcast_to(arr[:, 4 * g + e:4 * g + e + 1], (rows, SSD_GROUP_W)) for e in range(4)]
    return jnp.where(lane < SSD_P, c[0], jnp.where(lane < 2 * SSD_P, c[1],
                                                    jnp.where(lane < 3 * SSD_P, c[2], c[3])))


def _cumsum_rows(a):
    row = lax.broadcasted_iota(jnp.int32, a.shape, 0)
    s = 1
    while s < a.shape[0]:
        a = a + jnp.where(row >= s, pltpu.roll(a, s, axis=0), 0.0)
        s *= 2
    return a


def _ssd_kernel(xbc_ref, dt_ref, z_ref, cw_ref, cb_ref, dtb_ref, alog_ref, dskip_ref, ng_ref,
                y_ref, st_ref, cs_ref, xp_scr, xc_scr, stt_scr):
    q = BAND
    rows_step = xbc_ref.shape[0]
    c = pl.program_id(1)
    last = c == pl.num_programs(1) - 1

    @pl.when(c == 0)
    def _():
        xp_scr[0:SUBLANES, :] = jnp.zeros((SUBLANES, CONV_DIM), F32)
        stt_scr[...] = jnp.zeros(stt_scr.shape, F32)

    xp_scr[SUBLANES:SUBLANES + rows_step, :] = xbc_ref[...]
    cblk = 512
    nt = rows_step // SUBLANES
    sub = lax.broadcasted_iota(jnp.int32, (nt, SUBLANES, cblk), 1)
    for j in range(CONV_DIM // cblk):
        cs = slice(j * cblk, (j + 1) * cblk)
        x3 = xp_scr[:, cs].reshape(nt + 1, SUBLANES, cblk)
        acc = cb_ref[:, cs] + x3[1:] * cw_ref[D_CONV - 1:D_CONV, cs]
        for k in range(1, D_CONV):
            rot = pltpu.roll(x3, k, axis=1)
            back = jnp.where(sub < k, rot[:-1], rot[1:])
            acc = acc + back * cw_ref[D_CONV - 1 - k:D_CONV - k, cs]
        xc_scr[:, cs] = _silu(acc).reshape(rows_step, cblk)
    xp_scr[0:SUBLANES, :] = xp_scr[rows_step:rows_step + SUBLANES, :]

    for ch in range(rows_step // q):
        _ssd_chunk(slice(ch * q, (ch + 1) * q), xc_scr, dt_ref, z_ref, dtb_ref, alog_ref, dskip_ref, ng_ref,
                   y_ref, stt_scr)

    @pl.when(last)
    def _():
        cs_ref[...] = xp_scr[SUBLANES + rows_step - (D_CONV - 1):SUBLANES + rows_step, :]
        for k in range(D_INNER // LANES):
            st_ref[k * LANES:(k + 1) * LANES, :] = stt_scr[:, k * LANES:(k + 1) * LANES].T


def _ssd_chunk(rows, xc_scr, dt_ref, z_ref, dtb_ref, alog_ref, dskip_ref, ng_ref, y_ref, stt_scr):
    q = BAND
    dt = _softplus(dt_ref[rows, 0:LANES] + dtb_ref[...])
    a = dt * (-jnp.exp(alog_ref[...]))
    acs = _cumsum_rows(a) * LOG2_E
    acs_t = acs.T
    dt_t = dt.T
    ea_last = jnp.exp2(acs[q - 1:q, :])
    w_end_t = dt_t * jnp.exp2(jnp.broadcast_to(acs_t[:, q - 1:q], (q, q)) - acs_t)
    src_t = acs_t - jnp.log2(dt_t)

    row = lax.broadcasted_iota(jnp.int32, (q, q), 0)
    colq = lax.broadcasted_iota(jnp.int32, (q, q), 1)
    tril = row >= colq
    lane = lax.broadcasted_iota(jnp.int32, (q, SSD_GROUP_W), 1)
    lane1 = lax.broadcasted_iota(jnp.int32, (1, SSD_GROUP_W), 1)

    for g in range(SSD_GROUPS):
        gs = slice(g * SSD_GROUP_W, (g + 1) * SSD_GROUP_W)
        bg = xc_scr[rows, D_INNER + g * D_STATE:D_INNER + (g + 1) * D_STATE]
        cg = xc_scr[rows, D_INNER + (SSD_GROUPS + g) * D_STATE:D_INNER + (SSD_GROUPS + g + 1) * D_STATE]
        cbm = lax.dot_general(cg.astype(BF16), bg.astype(BF16), NT_DIMS, preferred_element_type=F32)
        bg_t = bg.T
        xg = xc_scr[rows, gs]
        xgb = xg.astype(BF16)
        stg = stt_scr[:, gs]
        rhs = jnp.concatenate([xgb, stg.astype(BF16)], axis=0)
        y, snew = None, None
        for e in range(4):
            h = 4 * g + e
            col_h = jnp.broadcast_to(acs[:, h:h + 1], (q, q))
            row_h = lambda v: jnp.broadcast_to(v[h:h + 1, :], (q, q))
            intra = cbm * jnp.exp2(jnp.where(tril, col_h - row_h(src_t), NEG))
            inter = cg * jnp.exp2(col_h)
            r = jnp.dot(jnp.concatenate([intra, inter], axis=1).astype(BF16), rhs,
                        preferred_element_type=F32)
            sr = jnp.dot((bg_t * row_h(w_end_t)).astype(BF16), xgb, preferred_element_type=F32)
            mine = lane >= e * SSD_P
            y = r if e == 0 else jnp.where(mine, r, y)
            snew = sr if e == 0 else jnp.where(mine, sr, snew)
        y = y + dskip_ref[:, gs] * xg
        stt_scr[:, gs] = stg * _expand4(ea_last, g, lane1) + snew
        hg = y * _silu(z_ref[rows, gs])
        y_ref[rows, gs] = (_rms(hg) * ng_ref[:, gs]).astype(BF16)


SSD_CHUNKS_PER_STEP = 4


def _prompt_ssd(proj, conv_w, conv_b, dt_bias, a_log, d_skip, norm_g, layer):
    b, l, _ = proj.shape
    q = BAND * SSD_CHUNKS_PER_STEP
    vec = lambda w: pl.BlockSpec((None, 1, w), lambda bb, c: (layer, 0, 0))
    return pl.pallas_call(
        _ssd_kernel,
        grid=(b, l // q),
        in_specs=[
            pl.BlockSpec((None, q, CONV_DIM), lambda bb, c: (bb, c, OFF_XBC // CONV_DIM)),
            pl.BlockSpec((None, q, DT_W), lambda bb, c: (bb, c, OFF_DT // DT_W)),
            pl.BlockSpec((None, q, D_INNER), lambda bb, c: (bb, c, OFF_Z // D_INNER)),
            pl.BlockSpec((None, D_CONV, CONV_DIM), lambda bb, c: (layer, 0, 0)),
            vec(CONV_DIM), vec(LANES), vec(LANES), vec(D_INNER), vec(D_INNER),
        ],
        out_specs=[
            pl.BlockSpec((None, q, D_INNER), lambda bb, c: (bb, c, 0)),
            pl.BlockSpec((None, D_INNER, D_STATE), lambda bb, c: (bb, 0, 0)),
            pl.BlockSpec((None, D_CONV - 1, CONV_DIM), lambda bb, c: (bb, 0, 0)),
        ],
        out_shape=[
            jax.ShapeDtypeStruct((b, l, D_INNER), BF16),
            jax.ShapeDtypeStruct((b, D_INNER, D_STATE), F32),
            jax.ShapeDtypeStruct((b, D_CONV - 1, CONV_DIM), F32),
        ],
        scratch_shapes=[
            pltpu.VMEM((q + SUBLANES, CONV_DIM), F32),
            pltpu.VMEM((q, CONV_DIM), F32),
            pltpu.VMEM((D_STATE, D_INNER), F32),
        ],
        compiler_params=_params(("arbitrary", "arbitrary")),
        name="prompt_ssd",
    )(proj, proj, proj, conv_w, conv_b, dt_bias, a_log, d_skip, norm_g)


def _out_proj_kernel(attn_ref, ssd_ref, gate_ref, x_ref, g1_ref, wa_ref, ws_ref, wo_ref, o_ref):
    pa = jnp.dot(attn_ref[...].astype(BF16), wa_ref[...], preferred_element_type=F32)
    ps = jnp.dot(ssd_ref[...], ws_ref[...], preferred_element_type=F32)
    merged = _sigmoid(gate_ref[:, 0:D_MODEL]) * pa + _sigmoid(gate_ref[:, D_MODEL:2 * D_MODEL]) * ps
    o_ref[...] = x_ref[...] + g1_ref[...] * jnp.dot(
        merged.astype(BF16), wo_ref[...], preferred_element_type=F32)


def _out_proj(attn, ssd, proj, x, mod, w_o_attn, w_o_ssd, w_out, layer, tm):
    bx, lx, d = x.shape
    r = mod.shape[2]
    row = lambda w, cb=0: pl.BlockSpec((None, tm, w), lambda b, i: (b, i, cb))
    wgt = lambda k, n: pl.BlockSpec((None, k, n), lambda b, i: (layer, 0, 0), pipeline_mode=pl.Buffered(1))
    return pl.pallas_call(
        _out_proj_kernel,
        grid=(bx, lx // tm),
        in_specs=[
            row(GROUP_W), row(D_INNER), row(2 * D_MODEL, OFF_GATE // (2 * D_MODEL)), row(d),
            pl.BlockSpec((None, None, r, d), lambda b, i: (layer, b, 0, 2)),
            wgt(GROUP_W, d), wgt(D_INNER, d), wgt(d, d),
        ],
        out_specs=row(d),
        out_shape=jax.ShapeDtypeStruct((bx, lx, d), F32),
        compiler_params=_params(("arbitrary", "arbitrary")),
        name="out_proj",
    )(attn, ssd, proj, x, mod, w_o_attn, w_o_ssd, w_out)


MLP_FF_TILE = 1024


def _mlp_kernel(*refs, final, ride):
    x_ref, mod_ref, g_ref, wu_ref, wd_ref, fg_ref = refs[:6]
    o_ref = refs[-3] if ride else refs[-1]
    x = x_ref[...]
    h = (_rms(x) * g_ref[...]) * (1.0 + mod_ref[:, D_MODEL:2 * D_MODEL]) + mod_ref[:, 0:D_MODEL]
    hb = h.astype(BF16)

    ride_done = None
    if ride:
        proj_ref, c_ref, b_ref, bself_ref = refs[6:10]
        oc_ref, part_ref = refs[-2:]
        g = RIDE_GROUP
        consts = _dec_consts()
        for j in range(proj_ref.shape[0]):
            m, l, out = _dec_group(g, proj_ref[j], c_ref.at[j], oc_ref.at[j], b_ref[...],
                                   bself_ref[g][:, 0:1], consts, on_vpu=True)
            part_ref[j, :, 0:GROUP_W] = out
            part_ref[j, :, GROUP_W:GROUP_W + LANES] = jnp.broadcast_to(m, (SUBLANES, LANES))
            part_ref[j, :, GROUP_W + LANES:PART_W] = jnp.broadcast_to(l, (SUBLANES, LANES))
            zero = jnp.minimum(l[0:1, 0:1], 0.0)
            ride_done = zero if ride_done is None else ride_done + zero

    n_ff = D_FF // MLP_FF_TILE
    acc = None
    for f in range(n_ff):
        fs = slice(f * MLP_FF_TILE, (f + 1) * MLP_FF_TILE)
        lhs = hb + ride_done.astype(BF16) if (ride_done is not None and f == n_ff - 1) else hb
        u = jnp.maximum(jnp.dot(lhs, wu_ref[:, fs], preferred_element_type=F32), 0.0)
        part = jnp.dot((u * u).astype(BF16), wd_ref[fs, :], preferred_element_type=F32)
        acc = part if f == 0 else acc + part
    x2 = x + mod_ref[:, 2 * D_MODEL:3 * D_MODEL] * acc
    if final:
        x2 = _rms(x2) * fg_ref[...]
    o_ref[...] = x2


def _mlp(x, mod, norm_g, w_up, w_down, final_g, layer, tm, final, ride=None):
    bx, lx, d = x.shape
    r = mod.shape[2]
    nt = lx // tm
    resident = pl.Buffered(1)
    in_specs = [
        pl.BlockSpec((None, tm, d), lambda b, i: (b, i, 0)),
        pl.BlockSpec((None, None, r, 3 * d), lambda b, i: (layer, b, 0, 1)),
        pl.BlockSpec((None, 1, d), lambda b, i: (layer, 0, 0)),
        pl.BlockSpec((None, d, D_FF), lambda b, i: (layer, 0, 0), pipeline_mode=resident),
        pl.BlockSpec((None, D_FF, d), lambda b, i: (layer, 0, 0), pipeline_mode=resident),
        pl.BlockSpec((1, d), lambda b, i: (0, 0)),
    ]
    out_specs = [pl.BlockSpec((None, tm, d), lambda b, i: (b, i, 0))]
    out_shape = [jax.ShapeDtypeStruct((bx, lx, d), F32)]
    args = [x, mod, norm_g, w_up, w_down, final_g]
    aliases = {}
    if ride is not None:
        proj_s, cache, bias, bias_self, prev = ride
        bs = proj_s.shape[1]
        k = bs // (bx * nt)
        assert k * bx * nt == bs
        cspec = pl.BlockSpec((None, k) + cache.shape[2:], lambda b, i: (layer, b * nt + i, 0, 0, 0, 0))
        in_specs += [pl.BlockSpec((k, 1, PROJ_W), lambda b, i: (b * nt + i, 0, 0)), cspec,
                     pl.BlockSpec(bias.shape, lambda b, i: (0, 0)),
                     pl.BlockSpec(bias_self.shape, lambda b, i: (0, 0, 0))]
        args += [proj_s.reshape(bs, 1, PROJ_W), cache, bias, bias_self]
        if prev is not None:
            in_specs.append(pl.BlockSpec(memory_space=pl.ANY))
            aliases = {len(args): 1}
            args.append(prev)
        out_specs += [cspec, pl.BlockSpec((k, SUBLANES, PART_W), lambda b, i: (b * nt + i, 0, 0))]
        out_shape += [jax.ShapeDtypeStruct(cache.shape, cache.dtype),
                      jax.ShapeDtypeStruct((bs, SUBLANES, PART_W), F32)]
    out = pl.pallas_call(
        functools.partial(_mlp_kernel, final=final, ride=ride is not None),
        grid=(bx, nt),
        in_specs=in_specs,
        out_specs=out_specs,
        out_shape=out_shape,
        input_output_aliases=aliases,
        compiler_params=_params(("arbitrary", "arbitrary")),
        name="mlp",
    )(*args)
    return out if ride is not None else out[0]


RIDE_GROUP = N_GROUPS - 1
PART_W = GROUP_W + 2 * LANES


def _dec_consts():
    row = lax.broadcasted_iota(jnp.int32, (SUBLANES, GROUP_W), 0)
    lane = lax.broadcasted_iota(jnp.int32, (SUBLANES, GROUP_W), 1)
    hmask = jnp.logical_and(lane >= row * HEAD_DIM, lane < (row + 1) * HEAD_DIM)
    sel = (lax.shift_right_logical(lax.broadcasted_iota(jnp.int32, (SUBLANES, 3 * LANES), 1), 7)
           == lax.broadcasted_iota(jnp.int32, (SUBLANES, 3 * LANES), 0)).astype(F32)
    last_lane = lax.broadcasted_iota(jnp.int32, (GROUP_W, LANES), 1) == LANES - 1
    return row, hmask, sel, last_lane


def _dec_group(g, proj_row, c_ref, oc_ref, bias, bias_self, consts, on_vpu=False):
    row, hmask, sel, last_lane = consts
    lb = c_ref.shape[-1]
    heads = [slice(h * HEAD_DIM, (h + 1) * HEAD_DIM) for h in range(HEADS_PER_GROUP)]
    qv = proj_row[:, OFF_Q + g * GROUP_W:OFF_Q + (g + 1) * GROUP_W]
    kn = proj_row[:, OFF_K + g * GROUP_W:OFF_K + (g + 1) * GROUP_W]
    vn = proj_row[:, OFF_V + g * GROUP_W:OFF_V + (g + 1) * GROUP_W]
    qbd = jnp.where(hmask, jnp.broadcast_to(qv, (SUBLANES, GROUP_W)), 0.0)
    kt = jnp.concatenate([c_ref[0, h] for h in range(HEADS_PER_GROUP)], axis=0)
    vt = jnp.concatenate([c_ref[1, h] for h in range(HEADS_PER_GROUP)], axis=0)
    rows3 = jnp.where(row == 0, jnp.broadcast_to(kn, (SUBLANES, GROUP_W)),
                      jnp.where(row == 1, jnp.broadcast_to(vn, (SUBLANES, GROUP_W)),
                                jnp.where(row == 2, jnp.broadcast_to(qv, (SUBLANES, GROUP_W)), 0.0)))
    cols = lax.dot_general(rows3, sel, TN_DIMS, precision=lax.Precision.HIGHEST,
                           preferred_element_type=F32)
    if on_vpu:
        qk = kt * jnp.concatenate([cols[:, 2 * LANES:3 * LANES]] * (lb // LANES), axis=1)
        per_head = [jnp.broadcast_to(jnp.sum(qk[hs, :], axis=0, keepdims=True), (SUBLANES, lb)) for hs in heads]
        row_lb = lax.broadcasted_iota(jnp.int32, (SUBLANES, lb), 0)
        s = functools.reduce(lambda acc, h: jnp.where(row_lb == h, per_head[h], acc),
                             range(HEADS_PER_GROUP - 1), per_head[-1])
    else:
        s = jnp.dot(qbd.astype(BF16), kt.astype(BF16), preferred_element_type=F32)
    s = s * ATTN_SCALE + bias
    s_self = jnp.sum(qbd * kn, axis=-1, keepdims=True) * ATTN_SCALE + bias_self
    m = jnp.maximum(jnp.max(s, axis=-1, keepdims=True), s_self)
    p = jnp.exp(s - m)
    p_self = jnp.exp(s_self - m)
    l = jnp.sum(p, axis=-1, keepdims=True) + p_self
    if on_vpu:
        mixed = [jnp.broadcast_to(jnp.sum(vt[hs, :] * p[h:h + 1, :], axis=1, keepdims=True), (HEAD_DIM, LANES))
                 + jnp.broadcast_to(p_self[h:h + 1, :], (HEAD_DIM, LANES)) * cols[hs, LANES:2 * LANES]
                 for h, hs in enumerate(heads)]
        col = jnp.concatenate(mixed, axis=0)
        acc = jnp.broadcast_to(jnp.concatenate(
            [col[k * LANES:(k + 1) * LANES, :].T[0:1, :] for k in range(GROUP_W // LANES)], axis=1),
            (SUBLANES, GROUP_W))
    else:
        acc = lax.dot_general(p.astype(BF16), vt.astype(BF16), NT_DIMS,
                              preferred_element_type=F32) + p_self * vn
    for kv, t in ((0, kt), (1, vt)):
        rolled = pltpu.roll(t, lb - 1, axis=1)
        tail = jnp.where(last_lane, cols[:, kv * LANES:(kv + 1) * LANES], rolled[:, lb - LANES:lb])
        new = tail if lb == LANES else jnp.concatenate([rolled[:, 0:lb - LANES], tail], axis=1)
        for h in range(HEADS_PER_GROUP):
            oc_ref[kv, h] = new[h * HEAD_DIM:(h + 1) * HEAD_DIM, :]
    return m, l, acc


def _dec_attn_kernel(*refs, groups, n_alias):
    n = len(groups)
    proj_ref, c_refs, b_refs = refs[0], refs[1:1 + n], refs[1 + n:1 + 2 * n]
    bself_ref, part_ref = refs[1 + 2 * n], refs[2 + 2 * n]
    o_ref, oc_refs = refs[3 + 2 * n + n_alias], refs[4 + 2 * n + n_alias:4 + 3 * n + n_alias]
    consts = _dec_consts()
    hmask = consts[1]
    for j in range(proj_ref.shape[0]):
        parts = [_dec_group(g, proj_ref[j], c_refs[i].at[j], oc_refs[i].at[j], b_refs[i][...],
                            bself_ref[g][:, 0:1], consts) for i, g in enumerate(groups)]
        parts.append((part_ref[j, :, GROUP_W:GROUP_W + 1], part_ref[j, :, GROUP_W + LANES:GROUP_W + LANES + 1],
                      part_ref[j, :, 0:GROUP_W]))
        m_all = functools.reduce(jnp.maximum, [p[0] for p in parts])
        num = sum(jnp.exp(m - m_all) * acc for m, _, acc in parts)
        den = sum(jnp.exp(m - m_all) * l for m, l, _ in parts)
        o_ref[j] = jnp.sum(jnp.where(hmask, num / den, 0.0), axis=0, keepdims=True)


DEC_SEQS_PER_STEP = 4


def _decode_attention(proj_s, groups, caches_t, bias_dec, bias_self, part, prev_out, layer):
    bs = proj_s.shape[1]
    n = len(groups)
    k = DEC_SEQS_PER_STEP if bs % DEC_SEQS_PER_STEP == 0 else 1
    cspec = lambda c: pl.BlockSpec((None, k) + c.shape[2:], lambda b: (layer, b, 0, 0, 0, 0))
    in_specs = [pl.BlockSpec((k, 1, PROJ_W), lambda b: (b, 0, 0))]
    in_specs += [cspec(c) for c in caches_t]
    in_specs += [pl.BlockSpec(bd.shape, lambda b: (0, 0)) for bd in bias_dec]
    in_specs += [pl.BlockSpec(bias_self.shape, lambda b: (0, 0, 0)),
                 pl.BlockSpec((k, SUBLANES, PART_W), lambda b: (b, 0, 0))]
    args = [proj_s.reshape(bs, 1, PROJ_W), *caches_t, *bias_dec, bias_self, part]
    aliases = {}
    if prev_out is not None:
        in_specs += [pl.BlockSpec(memory_space=pl.ANY)] * n
        aliases = {len(args) + j: 1 + j for j in range(n)}
        args += list(prev_out)
    out = pl.pallas_call(
        functools.partial(_dec_attn_kernel, groups=tuple(groups), n_alias=len(aliases)),
        grid=(bs // k,),
        in_specs=in_specs,
        out_specs=[pl.BlockSpec((k, 1, GROUP_W), lambda b: (b, 0, 0))] + [cspec(c) for c in caches_t],
        out_shape=[jax.ShapeDtypeStruct((bs, 1, GROUP_W), F32)] + [
            jax.ShapeDtypeStruct(c.shape, c.dtype) for c in caches_t],
        input_output_aliases=aliases,
        compiler_params=_params(("arbitrary",)),
        name="decode_attn",
    )(*args)
    return out[0].reshape(1, bs, GROUP_W), out[1:]


def _dec_ssd_kernel(*refs, n_alias):
    proj_ref, cs_ref, st_ref, cw_ref, cb_ref, dtb_ref, alog_ref, dskip_ref, ng_ref = refs[:9]
    y_ref, nst_ref, ncs_ref, xc_scr, xdt_t_scr, da_t_scr, ct_scr, yt_scr = refs[9 + n_alias:]
    b = pl.program_id(0)
    bs = proj_ref.shape[0]
    nblk = D_INNER // LANES
    pad_rows = lambda v: jnp.concatenate([v, jnp.zeros((LANES - bs, v.shape[1]), v.dtype)], axis=0)

    @pl.when(b == 0)
    def _():
        xr = proj_ref[:, OFF_XBC:OFF_XBC + CONV_DIM]
        acc = cb_ref[...]
        for t in range(D_CONV - 1):
            acc = acc + cs_ref[t] * cw_ref[t:t + 1, :]
        acc = acc + xr * cw_ref[D_CONV - 1:D_CONV, :]
        xc = pad_rows(_silu(acc))
        xc_scr[...] = xc
        for t in range(D_CONV - 2):
            ncs_ref[t] = cs_ref[t + 1]
        ncs_ref[D_CONV - 2] = xr
        dt = _softplus(proj_ref[:, OFF_DT:OFF_DT + LANES] + dtb_ref[...])
        da = jnp.exp(dt * (-jnp.exp(alog_ref[...])))
        onehot = (lax.shift_right_logical(lax.broadcasted_iota(jnp.int32, (LANES, D_INNER), 1), 6)
                  == lax.broadcasted_iota(jnp.int32, (LANES, D_INNER), 0)).astype(F32)
        expand = lambda v: pad_rows(jnp.dot(v, onehot, precision=lax.Precision.HIGHEST,
                                            preferred_element_type=F32))
        xdt = xc[:, 0:D_INNER] * expand(dt)
        dae = expand(da)
        for k in range(nblk):
            ks = slice(k * LANES, (k + 1) * LANES)
            xdt_t_scr[ks, :] = xdt[:, ks].T.astype(BF16)
            da_t = dae[:, ks].T
            da_hi = da_t.astype(BF16)
            da_t_scr[ks, 0:LANES] = da_hi
            da_t_scr[ks, LANES:2 * LANES] = (da_t - da_hi.astype(F32)).astype(BF16)
        for g in range(SSD_GROUPS):
            ct_scr[g] = xc[:, D_INNER + (SSD_GROUPS + g) * D_STATE:D_INNER + (SSD_GROUPS + g + 1) * D_STATE].T
        yt_scr[...] = jnp.zeros(yt_scr.shape, F32)

    row = lax.broadcasted_iota(jnp.int32, (LANES, LANES), 0)
    lane = lax.broadcasted_iota(jnp.int32, (LANES, LANES), 1)
    for j in range(st_ref.shape[0]):
        seq = b * st_ref.shape[0] + j
        pick = jnp.where(row == seq, 1.0, 0.0).astype(BF16)
        decay = jnp.dot(da_t_scr[...], jnp.concatenate([pick, pick], axis=0), preferred_element_type=F32)
        for g in range(SSD_GROUPS):
            rs = slice(g * SSD_GROUP_W, (g + 1) * SSD_GROUP_W)
            bg = xc_scr[:, D_INNER + g * D_STATE:D_INNER + (g + 1) * D_STATE]
            b_sel = jnp.where(row == seq, bg, 0.0).astype(BF16)
            hn = st_ref[j, rs, :] * decay[rs, :] + jnp.dot(xdt_t_scr[rs, :], b_sel, preferred_element_type=F32)
            nst_ref[j, rs, :] = hn
            c_sel = jnp.where(lane == seq, ct_scr[g], 0.0).astype(BF16)
            yt_scr[rs, :] += jnp.dot(hn.astype(BF16), c_sel, preferred_element_type=F32)

    @pl.when(b == pl.num_programs(0) - 1)
    def _():
        for g in range(SSD_GROUPS):
            gs = slice(g * SSD_GROUP_W, (g + 1) * SSD_GROUP_W)
            yg = jnp.concatenate([yt_scr[k * LANES:(k + 1) * LANES, :].T[0:bs, :]
                                  for k in range(g * SSD_GROUP_W // LANES, (g + 1) * SSD_GROUP_W // LANES)],
                                 axis=1)
            y = yg + dskip_ref[:, gs] * xc_scr[0:bs, gs]
            hg = y * _silu(proj_ref[:, OFF_Z + g * SSD_GROUP_W:OFF_Z + (g + 1) * SSD_GROUP_W])
            y_ref[:, gs] = (_rms(hg) * ng_ref[:, gs]).astype(BF16)


def _decode_ssd(proj_s, conv_state_t, state, conv_w, conv_b, dt_bias, a_log, d_skip, norm_g, prev_state, layer):
    bs = proj_s.shape[1]
    vec = lambda w: pl.BlockSpec((None, 1, w), lambda b: (layer, 0, 0))
    k = DEC_SEQS_PER_STEP if bs % DEC_SEQS_PER_STEP == 0 else 1
    st_spec = pl.BlockSpec((None, k, D_INNER, D_STATE), lambda b: (layer, b, 0, 0))
    args = [proj_s, conv_state_t, state, conv_w, conv_b, dt_bias, a_log, d_skip, norm_g]
    extra, aliases = [], {}
    if prev_state is not None:
        extra, aliases = [pl.BlockSpec(memory_space=pl.ANY)], {len(args): 1}
        args.append(prev_state)
    return pl.pallas_call(
        functools.partial(_dec_ssd_kernel, n_alias=len(aliases)),
        grid=(bs // k,),
        in_specs=[
            pl.BlockSpec((None, bs, PROJ_W), lambda b: (0, 0, 0)),
            pl.BlockSpec((None, D_CONV - 1, bs, CONV_DIM), lambda b: (layer, 0, 0, 0)),
            st_spec,
            pl.BlockSpec((None, D_CONV, CONV_DIM), lambda b: (layer, 0, 0)),
            vec(CONV_DIM), vec(LANES), vec(LANES), vec(D_INNER), vec(D_INNER),
        ] + extra,
        out_specs=[
            pl.BlockSpec((None, bs, D_INNER), lambda b: (0, 0, 0)),
            st_spec,
            pl.BlockSpec((D_CONV - 1, bs, CONV_DIM), lambda b: (0, 0, 0)),
        ],
        out_shape=[
            jax.ShapeDtypeStruct((1, bs, D_INNER), BF16),
            jax.ShapeDtypeStruct(state.shape, F32),
            jax.ShapeDtypeStruct((D_CONV - 1, bs, CONV_DIM), F32),
        ],
        input_output_aliases=aliases,
        scratch_shapes=[
            pltpu.VMEM((LANES, CONV_DIM), F32),
            pltpu.VMEM((D_INNER, LANES), BF16),
            pltpu.VMEM((D_INNER, 2 * LANES), BF16),
            pltpu.VMEM((SSD_GROUPS, D_STATE, LANES), F32),
            pltpu.VMEM((D_INNER, LANES), F32),
        ],
        compiler_params=_params(("arbitrary",)),
        name="decode_ssd",
    )(*args)


def _reorder_w_in(w_in):
    q0, z0, x0, d0, g0 = 0, 3 * QKV_W, 3 * QKV_W + D_INNER, 3 * QKV_W + D_INNER + CONV_DIM, \
        3 * QKV_W + D_INNER + CONV_DIM + SSD_HEADS
    wt = jnp.swapaxes(w_in, 1, 2)
    pieces = [wt[:, z0:x0], wt[:, g0:g0 + 2 * D_MODEL], wt[:, x0:d0], wt[:, q0:z0], wt[:, d0:g0],
              jnp.zeros((wt.shape[0], DT_W - SSD_HEADS, wt.shape[2]), wt.dtype)]
    return jnp.concatenate(pieces, axis=1).astype(BF16)


def _pad_lanes(v, width):
    return jnp.pad(v, ((0, 0), (0, width - v.shape[-1])))[:, None, :]


def kernel(x_prompt, x_sample, cache_kv_g0, cache_kv_g1, cache_kv_g2, state_ssm, state_conv, c_prompt,
           c_sample, rel_bias, w_ada, b_ada, norm1_g, norm2_g, w_in, conv_w, conv_b, dt_bias, a_log, d_skip,
           ssd_norm_g, w_o_attn, w_o_ssd, w_out, w_up, w_down, final_g):
    depth = w_in.shape[0]
    b, l, d = x_prompt.shape
    bs = x_sample.shape[0]
    caches = (cache_kv_g0, cache_kv_g1, cache_kv_g2)
    assert d == D_MODEL and x_sample.shape[1] == 1 and l % ATTN_ROWS == 0 and PROJ_W % IN_PROJ_TN == 0
    assert all(win == BAND * dil for win, dil in DIL_GROUPS)
    assert all(c.shape[2] == win for c, (win, _) in zip(caches, DIL_GROUPS))
    assert bs % SUBLANES == 0 and bs <= LANES

    w_in_r = _reorder_w_in(w_in)
    w_oa, w_os, w_o, w_u, w_d = (w.astype(BF16) for w in (w_o_attn, w_o_ssd, w_out, w_up, w_down))
    n1, n2 = norm1_g[:, None, :], norm2_g[:, None, :]
    conv_b3, ng3 = conv_b[:, None, :], ssd_norm_g[:, None, :]
    dtb3, alog3 = _pad_lanes(dt_bias, LANES), _pad_lanes(a_log, LANES)
    dskip3 = jnp.repeat(d_skip, SSD_P, axis=-1)[:, None, :]
    fg = final_g[None, :]

    rows = b + bs
    rows_pad = -(-rows // SUBLANES) * SUBLANES
    c_all = jnp.pad(jnp.concatenate([c_prompt, c_sample], axis=0), ((0, rows_pad - rows), (0, 0)))
    mod = _ada_mod(c_all, w_ada, b_ada)
    mod_p = mod[:, :b, None, :]
    mod_s = mod[:, None, b:rows, :]

    bias = _bias_lookup(rel_bias, _bucket_tiles())
    head_rows = ((0, SUBLANES - HEADS_PER_GROUP), (0, 0))
    bias_rows = _bias_lookup(rel_bias, _bucket_rows())[:, 0, :]
    bias_dec = [jnp.pad(bias_rows[HEADS_PER_GROUP * g:HEADS_PER_GROUP * (g + 1), :win], head_rows)
                for g, (win, _) in enumerate(DIL_GROUPS)]
    self_bias = bias[:, 0, BAND].reshape(N_GROUPS, HEADS_PER_GROUP, 1)
    bias_self = jnp.pad(jnp.broadcast_to(self_bias, (N_GROUPS, HEADS_PER_GROUP, LANES)), ((0, 0),) + head_rows)

    xp = x_prompt
    xs = x_sample.reshape(1, bs, d)
    conv_state_t = jnp.swapaxes(state_conv, 1, 2)
    state_r = state_ssm.reshape(depth, bs, D_INNER, D_STATE)
    caches_t = [jnp.transpose(c, (0, 1, 3, 4, 5, 2)) for c in caches]
    tm_p = 1024 if l % 1024 == 0 else BAND

    kv_p = [[] for _ in DIL_GROUPS]
    ssm_p, conv_p, conv_s = [], [], []
    others = [g for g in range(N_GROUPS) if g != RIDE_GROUP]
    kv_ride, kv_rest, st_s = None, None, None
    for layer in range(depth):
        final = layer == depth - 1
        proj_s = _in_proj(xs, mod_s, n1, w_in_r, layer, bs)
        proj = _in_proj(xp, mod_p, n1, w_in_r, layer, tm_p)
        attn = _prompt_attention(proj, bias)
        ssd, st, cs = _prompt_ssd(proj, conv_w, conv_b3, dtb3, alog3, dskip3, ng3, layer)
        x1 = _out_proj(attn, ssd, proj, xp, mod_p, w_oa, w_os, w_o, layer, min(512, l))
        xp, kv_ride, part = _mlp(x1, mod_p, n2, w_u, w_d, fg, layer, min(512, l), final,
                                 ride=(proj_s, caches_t[RIDE_GROUP], bias_dec[RIDE_GROUP], bias_self, kv_ride))
        for g, (win, _) in enumerate(DIL_GROUPS):
            keep = min(win, l)
            kk = proj[:, l - keep:, OFF_K + g * GROUP_W:OFF_K + (g + 1) * GROUP_W]
            vv = proj[:, l - keep:, OFF_V + g * GROUP_W:OFF_V + (g + 1) * GROUP_W]
            kv_p[g].append(jnp.stack([kk, vv], axis=2).reshape(b, keep, 2, HEADS_PER_GROUP, HEAD_DIM))
        ssm_p.append(st.reshape(b, SSD_HEADS, SSD_P, D_STATE))
        conv_p.append(cs)
        attn_s, kv_rest = _decode_attention(proj_s, others, [caches_t[g] for g in others],
                                            [bias_dec[g] for g in others], bias_self, part, kv_rest, layer)
        ssd_s, st_s, cs_s = _decode_ssd(proj_s, conv_state_t, state_r, conv_w, conv_b3, dtb3, alog3,
                                        dskip3, ng3, st_s, layer)
        x1s = _out_proj(attn_s, ssd_s, proj_s, xs, mod_s, w_oa, w_os, w_o, layer, bs)
        xs = _mlp(x1s, mod_s, n2, w_u, w_d, fg, layer, bs, final)
        conv_s.append(jnp.swapaxes(cs_s, 0, 1))

    kv_s = dict(zip(others, kv_rest))
    kv_s[RIDE_GROUP] = kv_ride
    kv_s = [jnp.transpose(kv_s[g], (0, 1, 5, 2, 3, 4)) for g in range(N_GROUPS)]
    return (xp, xs.reshape(bs, 1, d),
            jnp.stack(kv_p[0], axis=0), jnp.stack(kv_p[1], axis=0), jnp.stack(kv_p[2], axis=0),
            jnp.stack(ssm_p, axis=0), jnp.stack(conv_p, axis=0),
            kv_s[0], kv_s[1], kv_s[2],
            st_s.reshape(depth, bs, SSD_HEADS, SSD_P, D_STATE), jnp.stack(conv_s, axis=0))
```

```python
import functools
import math

import jax
import jax.numpy as jnp
import numpy as np
from jax import lax
from jax.experimental import pallas as pl
from jax.experimental.pallas import tpu as pltpu

F32 = jnp.float32
BF16 = jnp.bfloat16

D_MODEL = 1024
HEAD_DIM = 64
HEADS_PER_GROUP = 4
DIL_GROUPS = ((128, 1), (512, 4), (2048, 16))
N_GROUPS = len(DIL_GROUPS)
GROUP_W = HEADS_PER_GROUP * HEAD_DIM
QKV_W = N_GROUPS * GROUP_W
BAND = 128
NUM_BUCKETS = 32
MAX_DISTANCE = 2048
D_INNER = 2 * D_MODEL
SSD_HEADS = 32
SSD_P = 64
SSD_GROUPS = 8
SSD_GROUP_W = D_INNER // SSD_GROUPS
D_STATE = 128
D_CONV = 4
CONV_DIM = D_INNER + 2 * SSD_GROUPS * D_STATE
D_FF = 4 * D_MODEL
EPS = 1e-6
ATTN_SCALE = HEAD_DIM ** -0.5
NEG = -1e30
LOG2_E = 1.4426950408889634

SUBLANES = 8
LANES = 128
V7X_VMEM_LIMIT = 52 * 1024 * 1024

OFF_Z = 0
OFF_GATE = D_INNER
OFF_XBC = 2 * D_INNER
OFF_Q = OFF_XBC + CONV_DIM
OFF_K = OFF_Q + QKV_W
OFF_V = OFF_K + QKV_W
OFF_DT = OFF_V + QKV_W
DT_W = 2 * LANES
PROJ_W = OFF_DT + DT_W

NT_DIMS = (((1,), (1,)), ((), ()))
TN_DIMS = (((0,), (0,)), ((), ()))


def _params(semantics, vmem=V7X_VMEM_LIMIT):
    return pltpu.CompilerParams(dimension_semantics=semantics, vmem_limit_bytes=vmem)


def _sigmoid(x):
    return 0.5 * (jnp.tanh(0.5 * x) + 1.0)


def _silu(x):
    h = 0.5 * x
    return h + h * jnp.tanh(h)


def _softplus(x):
    return jnp.maximum(x, 0.0) + jnp.log(1.0 + jnp.exp(-jnp.abs(x)))


def _rms(x):
    return x * lax.rsqrt(jnp.mean(x * x, axis=-1, keepdims=True) + EPS)


def _ada_kernel(c_ref, w_ref, b_ref, o_ref):
    s = _silu(c_ref[...]).astype(BF16)
    o_ref[...] = jnp.dot(s, w_ref[...].astype(BF16), preferred_element_type=F32) + b_ref[...]


def _ada_mod(c_all, w_ada, b_ada):
    depth, d, n = w_ada.shape
    rows = c_all.shape[0]
    tn = 1536
    return pl.pallas_call(
        _ada_kernel,
        grid=(depth, n // tn),
        in_specs=[
            pl.BlockSpec((rows, d), lambda l, j: (0, 0)),
            pl.BlockSpec((None, d, tn), lambda l, j: (l, 0, j)),
            pl.BlockSpec((None, 1, tn), lambda l, j: (l, 0, j)),
        ],
        out_specs=pl.BlockSpec((None, rows, tn), lambda l, j: (l, 0, j)),
        out_shape=jax.ShapeDtypeStruct((depth, rows, n), F32),
        compiler_params=_params(("arbitrary", "arbitrary")),
        name="ada_mod",
    )(c_all, w_ada, b_ada.reshape(depth, 1, n))


IN_PROJ_TN = 1792


def _in_proj_kernel(x_ref, mod_ref, g_ref, w_ref, o_ref, h_scr, *, tn):
    n = pl.program_id(2)

    @pl.when(n == 0)
    def _():
        y = _rms(x_ref[...]) * g_ref[...]
        h = y * (1.0 + mod_ref[:, D_MODEL:2 * D_MODEL]) + mod_ref[:, 0:D_MODEL]
        h_scr[...] = h.astype(BF16)

    w = w_ref[pl.ds(pl.multiple_of(n * tn, tn), tn), :]
    o_ref[...] = lax.dot_general(h_scr[...], w, NT_DIMS, preferred_element_type=F32)


def _in_proj(x, mod, norm_g, w_in_r, layer, tm):
    bx, lx, d = x.shape
    r = mod.shape[2]
    tn = IN_PROJ_TN
    return pl.pallas_call(
        functools.partial(_in_proj_kernel, tn=tn),
        grid=(bx, lx // tm, PROJ_W // tn),
        in_specs=[
            pl.BlockSpec((None, tm, d), lambda b, i, n: (b, i, 0)),
            pl.BlockSpec((None, None, r, 2 * d), lambda b, i, n: (layer, b, 0, 0)),
            pl.BlockSpec((None, 1, d), lambda b, i, n: (layer, 0, 0)),
            pl.BlockSpec((None, PROJ_W, d), lambda b, i, n: (layer, 0, 0), pipeline_mode=pl.Buffered(1)),
        ],
        out_specs=pl.BlockSpec((None, tm, tn), lambda b, i, n: (b, i, n)),
        out_shape=jax.ShapeDtypeStruct((bx, lx, PROJ_W), F32),
        scratch_shapes=[pltpu.VMEM((tm, d), BF16)],
        compiler_params=_params(("arbitrary", "arbitrary", "arbitrary")),
        name="in_proj",
    )(x, mod, norm_g, w_in_r)


def _t5_bucket_np(dist):
    max_exact = NUM_BUCKETS // 2
    df = np.maximum(dist, 1).astype(np.float32)
    ratio = np.log(df / np.float32(max_exact)) / np.float32(math.log(MAX_DISTANCE / max_exact))
    large = max_exact + (ratio * np.float32(NUM_BUCKETS - max_exact)).astype(np.int32)
    large = np.minimum(large, NUM_BUCKETS - 1)
    return np.where(dist < max_exact, dist, large).astype(np.int32)


def _bucket_tiles():
    q = np.arange(BAND)[:, None]
    c = np.arange(2 * BAND)[None, :]
    j = q + BAND - c
    valid = (j >= 0) & (j <= BAND)
    tiles = []
    for _, dil in DIL_GROUPS:
        b = _t5_bucket_np(np.clip(j, 0, BAND) * dil)
        tiles.append(np.where(valid, b, -1))
    return np.stack(tiles).astype(np.int32)


def _bucket_rows():
    width = max(win for win, _ in DIL_GROUPS)
    i = np.arange(width)
    rows = []
    for win, dil in DIL_GROUPS:
        dist = win - i
        valid = (i < win) & (dist % dil == 0)
        rows.append(np.where(valid, _t5_bucket_np(np.clip(dist, 0, win)), -1))
    return np.broadcast_to(np.stack(rows)[:, None, :], (N_GROUPS, SUBLANES, width)).astype(np.int32)


def _bias_kernel(rb_ref, bk_ref, o_ref):
    hh = pl.program_id(0)
    bk = bk_ref[...]
    acc = jnp.full(bk.shape, NEG, F32)
    for b in range(NUM_BUCKETS):
        acc = jnp.where(bk == b, rb_ref[b, hh], acc)
    o_ref[...] = acc


def _bias_lookup(rel_bias, buckets):
    n_heads = rel_bias.shape[1]
    blk = (None,) + buckets.shape[1:]
    return pl.pallas_call(
        _bias_kernel,
        grid=(n_heads,),
        in_specs=[
            pl.BlockSpec(memory_space=pltpu.SMEM),
            pl.BlockSpec(blk, lambda h: (h // HEADS_PER_GROUP, 0, 0)),
        ],
        out_specs=pl.BlockSpec(blk, lambda h: (h, 0, 0)),
        out_shape=jax.ShapeDtypeStruct((n_heads,) + buckets.shape[1:], F32),
        compiler_params=_params(("arbitrary",)),
        name="bias_lookup",
    )(rel_bias, jnp.asarray(buckets))


HEADS_PER_TILE = LANES // HEAD_DIM
ATTN_ROWS = BAND * max(dil for _, dil in DIL_GROUPS)
ATTN_TILES_IN_FLIGHT = 8
ATTN_MERGE_ROWS = 256


def _attn_kernel(*refs):
    n = N_GROUPS
    bias_ref, o_ref, og_scr, lse_scr = refs[5 * n:]
    first = pl.program_id(1) == 0
    head0 = pl.program_id(2) * HEADS_PER_TILE
    col = lax.broadcasted_iota(jnp.int32, (BAND, 2 * BAND), 1)
    pen_first = jnp.where(col < BAND, jnp.where(first, NEG, 0.0), 0.0)
    lane = lax.broadcasted_iota(jnp.int32, (BAND, LANES), 1)

    for g, (_, dil) in enumerate(DIL_GROUPS):
        q_ref, kp_ref, kc_ref, vp_ref, vc_ref = refs[5 * g:5 * g + 5]
        period = BAND * dil
        periods = ATTN_ROWS // period

        def rows_of(start, dil=dil):
            if dil > 1:
                return pl.ds(start, BAND, stride=dil)
            return pl.ds(pl.multiple_of(start, BAND), BAND)

        def tile(rows, kprev, vprev, leading, g=g, q_ref=q_ref, kc_ref=kc_ref, vc_ref=vc_ref):
            q = q_ref[rows, :]
            kb = jnp.concatenate([kprev, kc_ref[rows, :]], axis=0).astype(BF16)
            vb = jnp.concatenate([vprev, vc_ref[rows, :]], axis=0).astype(BF16)
            mine = [jnp.logical_and(lane >= h * HEAD_DIM, lane < (h + 1) * HEAD_DIM) for h in range(HEADS_PER_TILE)]
            qa = q * ATTN_SCALE
            qs = jnp.concatenate([jnp.where(mk, qa, 0.0) for mk in mine], axis=0).astype(BF16)
            s_all = lax.dot_general(qs, kb, NT_DIMS, preferred_element_type=F32)
            ps, ls, lses = [], [], []
            for h in range(HEADS_PER_TILE):
                s = s_all[h * BAND:(h + 1) * BAND, :] + bias_ref[HEADS_PER_GROUP * g + head0 + h]
                if leading:
                    s = s + pen_first
                m = jnp.max(s, axis=-1, keepdims=True)
                p = jnp.exp(s - m)
                l = jnp.sum(p, axis=-1, keepdims=True)
                ps.append(p.astype(BF16))
                ls.append(l)
                lses.append(jnp.broadcast_to(m + jnp.log(l), (BAND, LANES)))
            o_all = jnp.dot(jnp.concatenate(ps, axis=0), vb, preferred_element_type=F32)
            o, lse = o_all[0:BAND, :] / ls[0], lses[0]
            for h in range(1, HEADS_PER_TILE):
                o = jnp.where(mine[h], o_all[h * BAND:(h + 1) * BAND, :] / ls[h], o)
                lse = jnp.where(mine[h], lses[h], lse)
            og_scr[g, rows, :] = o
            lse_scr[g, rows, :] = lse

        def lead(r, carry, rows_of=rows_of, tile=tile, kp_ref=kp_ref, vp_ref=vp_ref):
            rows = rows_of(r)
            tile(rows, kp_ref[rows, :], vp_ref[rows, :], True)
            return carry

        def body(j, carry, rows_of=rows_of, tile=tile, kc_ref=kc_ref, vc_ref=vc_ref, dil=dil, period=period):
            r, tau = j % dil, 1 + j // dil
            above = rows_of(r + (tau - 1) * period)
            tile(rows_of(r + tau * period), kc_ref[above, :], vc_ref[above, :], False)
            return carry

        lax.fori_loop(0, dil, lead, 0, unroll=min(dil, ATTN_TILES_IN_FLIGHT))
        n_body = dil * (periods - 1)
        if n_body:
            unroll = max(u for u in range(1, ATTN_TILES_IN_FLIGHT + 1) if n_body % u == 0)
            lax.fori_loop(0, n_body, body, 0, unroll=unroll)

    def merge(c, carry):
        rs = pl.ds(pl.multiple_of(c * ATTN_MERGE_ROWS, ATTN_MERGE_ROWS), ATTN_MERGE_ROWS)
        lse = [lse_scr[g, rs, :] for g in range(n)]
        m = functools.reduce(jnp.maximum, lse)
        e = [jnp.exp(v - m) for v in lse]
        o_ref[rs, :] = sum(e[g] * og_scr[g, rs, :] for g in range(n)) / sum(e)
        return carry

    lax.fori_loop(0, ATTN_ROWS // ATTN_MERGE_ROWS, merge, 0)


def _prompt_attention(proj, bias):
    b, l, _ = proj.shape
    tiles = GROUP_W // LANES
    in_specs = []
    for g, (_, dil) in enumerate(DIL_GROUPS):
        period = BAND * dil
        qb, kb, vb = (off // LANES + g * tiles for off in (OFF_Q, OFF_K, OFF_V))
        cur = lambda cb: pl.BlockSpec((None, ATTN_ROWS, LANES), lambda bb, i, t, cb=cb: (bb, i, cb + t))
        prev = lambda cb, per=ATTN_ROWS // period, period=period: pl.BlockSpec(
            (None, period, LANES), lambda bb, i, t: (bb, jnp.maximum(i * per - 1, 0), cb + t))
        in_specs += [cur(qb), prev(kb), cur(kb), prev(vb), cur(vb)]
    in_specs.append(pl.BlockSpec(bias.shape, lambda bb, i, t: (0, 0, 0), pipeline_mode=pl.Buffered(1)))
    return pl.pallas_call(
        _attn_kernel,
        grid=(b, l // ATTN_ROWS, tiles),
        in_specs=in_specs,
        out_specs=pl.BlockSpec((None, ATTN_ROWS, LANES), lambda bb, i, t: (bb, i, t)),
        out_shape=jax.ShapeDtypeStruct((b, l, GROUP_W), F32),
        scratch_shapes=[pltpu.VMEM((N_GROUPS, ATTN_ROWS, LANES), F32)] * 2,
        compiler_params=_params(("arbitrary", "arbitrary", "arbitrary")),
        name="prompt_attn",
    )(*([proj] * (5 * N_GROUPS)), bias)


def _expand4(arr, g, lane):
    rows = arr.shape[0]
    c = [jnp.broadcast_to(arr[:, 4 * g + e:4 * g + e + 1], (rows, SSD_GROUP_W)) for e in range(4)]
    return jnp.where(lane < SSD_P, c[0], jnp.where(lane < 2 * SSD_P, c[1],
                                                    jnp.where(lane < 3 * SSD_P, c[2], c[3])))


def _cumsum_rows(a):
    row = lax.broadcasted_iota(jnp.int32, a.shape, 0)
    s = 1
    while s < a.shape[0]:
        a = a + jnp.where(row >= s, pltpu.roll(a, s, axis=0), 0.0)
        s *= 2
    return a


def _ssd_kernel(xbc_ref, dt_ref, z_ref, cw_ref, cb_ref, dtb_ref, alog_ref, dskip_ref, ng_ref,
                y_ref, st_ref, cs_ref, xp_scr, xc_scr, stt_scr):
    q = BAND
    rows_step = xbc_ref.shape[0]
    c = pl.program_id(1)
    last = c == pl.num_programs(1) - 1

    @pl.when(c == 0)
    def _():
        xp_scr[0:SUBLANES, :] = jnp.zeros((SUBLANES, CONV_DIM), F32)
        stt_scr[...] = jnp.zeros(stt_scr.shape, F32)

    xp_scr[SUBLANES:SUBLANES + rows_step, :] = xbc_ref[...]
    cblk = 512
    nt = rows_step // SUBLANES
    sub = lax.broadcasted_iota(jnp.int32, (nt, SUBLANES, cblk), 1)
    for j in range(CONV_DIM // cblk):
        cs = slice(j * cblk, (j + 1) * cblk)
        x3 = xp_scr[:, cs].reshape(nt + 1, SUBLANES, cblk)
        acc = cb_ref[:, cs] + x3[1:] * cw_ref[D_CONV - 1:D_CONV, cs]
        for k in range(1, D_CONV):
            rot = pltpu.roll(x3, k, axis=1)
            back = jnp.where(sub < k, rot[:-1], rot[1:])
            acc = acc + back * cw_ref[D_CONV - 1 - k:D_CONV - k, cs]
        xc_scr[:, cs] = _silu(acc).reshape(rows_step, cblk)
    xp_scr[0:SUBLANES, :] = xp_scr[rows_step:rows_step + SUBLANES, :]

    for ch in range(rows_step // q):
        _ssd_chunk(slice(ch * q, (ch + 1) * q), xc_scr, dt_ref, z_ref, dtb_ref, alog_ref, dskip_ref, ng_ref,
                   y_ref, stt_scr)

    @pl.when(last)
    def _():
        cs_ref[...] = xp_scr[SUBLANES + rows_step - (D_CONV - 1):SUBLANES + rows_step, :]
        for k in range(D_INNER // LANES):
            st_ref[k * LANES:(k + 1) * LANES, :] = stt_scr[:, k * LANES:(k + 1) * LANES].T


def _ssd_chunk(rows, xc_scr, dt_ref, z_ref, dtb_ref, alog_ref, dskip_ref, ng_ref, y_ref, stt_scr):
    q = BAND
    dt = _softplus(dt_ref[rows, 0:LANES] + dtb_ref[...])
    a = dt * (-jnp.exp(alog_ref[...]))
    acs = _cumsum_rows(a) * LOG2_E
    acs_t = acs.T
    dt_t = dt.T
    ea_last = jnp.exp2(acs[q - 1:q, :])
    w_end_t = dt_t * jnp.exp2(jnp.broadcast_to(acs_t[:, q - 1:q], (q, q)) - acs_t)
    src_t = acs_t - jnp.log2(dt_t)

    row = lax.broadcasted_iota(jnp.int32, (q, q), 0)
    colq = lax.broadcasted_iota(jnp.int32, (q, q), 1)
    tril = row >= colq
    lane = lax.broadcasted_iota(jnp.int32, (q, SSD_GROUP_W), 1)
    lane1 = lax.broadcasted_iota(jnp.int32, (1, SSD_GROUP_W), 1)

    for g in range(SSD_GROUPS):
        gs = slice(g * SSD_GROUP_W, (g + 1) * SSD_GROUP_W)
        bg = xc_scr[rows, D_INNER + g * D_STATE:D_INNER + (g + 1) * D_STATE]
        cg = xc_scr[rows, D_INNER + (SSD_GROUPS + g) * D_STATE:D_INNER + (SSD_GROUPS + g + 1) * D_STATE]
        cbm = lax.dot_general(cg.astype(BF16), bg.astype(BF16), NT_DIMS, preferred_element_type=F32)
        bg_t = bg.T
        xg = xc_scr[rows, gs]
        xgb = xg.astype(BF16)
        stg = stt_scr[:, gs]
        rhs = jnp.concatenate([xgb, stg.astype(BF16)], axis=0)
        y, snew = None, None
        for e in range(4):
            h = 4 * g + e
            col_h = jnp.broadcast_to(acs[:, h:h + 1], (q, q))
            row_h = lambda v: jnp.broadcast_to(v[h:h + 1, :], (q, q))
            intra = cbm * jnp.exp2(jnp.where(tril, col_h - row_h(src_t), NEG))
            inter = cg * jnp.exp2(col_h)
            r = jnp.dot(jnp.concatenate([intra, inter], axis=1).astype(BF16), rhs,
                        preferred_element_type=F32)
            sr = jnp.dot((bg_t * row_h(w_end_t)).astype(BF16), xgb, preferred_element_type=F32)
            mine = lane >= e * SSD_P
            y = r if e == 0 else jnp.where(mine, r, y)
            snew = sr if e == 0 else jnp.where(mine, sr, snew)
        y = y + dskip_ref[:, gs] * xg
        stt_scr[:, gs] = stg * _expand4(ea_last, g, lane1) + snew
        hg = y * _silu(z_ref[rows, gs])
        y_ref[rows, gs] = (_rms(hg) * ng_ref[:, gs]).astype(BF16)


SSD_CHUNKS_PER_STEP = 4


def _prompt_ssd(proj, conv_w, conv_b, dt_bias, a_log, d_skip, norm_g, layer):
    b, l, _ = proj.shape
    q = BAND * SSD_CHUNKS_PER_STEP
    vec = lambda w: pl.BlockSpec((None, 1, w), lambda bb, c: (layer, 0, 0))
    return pl.pallas_call(
        _ssd_kernel,
        grid=(b, l // q),
        in_specs=[
            pl.BlockSpec((None, q, CONV_DIM), lambda bb, c: (bb, c, OFF_XBC // CONV_DIM)),
            pl.BlockSpec((None, q, DT_W), lambda bb, c: (bb, c, OFF_DT // DT_W)),
            pl.BlockSpec((None, q, D_INNER), lambda bb, c: (bb, c, OFF_Z // D_INNER)),
            pl.BlockSpec((None, D_CONV, CONV_DIM), lambda bb, c: (layer, 0, 0)),
            vec(CONV_DIM), vec(LANES), vec(LANES), vec(D_INNER), vec(D_INNER),
        ],
        out_specs=[
            pl.BlockSpec((None, q, D_INNER), lambda bb, c: (bb, c, 0)),
            pl.BlockSpec((None, D_INNER, D_STATE), lambda bb, c: (bb, 0, 0)),
            pl.BlockSpec((None, D_CONV - 1, CONV_DIM), lambda bb, c: (bb, 0, 0)),
        ],
        out_shape=[
            jax.ShapeDtypeStruct((b, l, D_INNER), BF16),
            jax.ShapeDtypeStruct((b, D_INNER, D_STATE), F32),
            jax.ShapeDtypeStruct((b, D_CONV - 1, CONV_DIM), F32),
        ],
        scratch_shapes=[
            pltpu.VMEM((q + SUBLANES, CONV_DIM), F32),
            pltpu.VMEM((q, CONV_DIM), F32),
            pltpu.VMEM((D_STATE, D_INNER), F32),
        ],
        compiler_params=_params(("arbitrary", "arbitrary")),
        name="prompt_ssd",
    )(proj, proj, proj, conv_w, conv_b, dt_bias, a_log, d_skip, norm_g)


def _out_proj_kernel(attn_ref, ssd_ref, gate_ref, x_ref, g1_ref, wa_ref, ws_ref, wo_ref, o_ref):
    pa = jnp.dot(attn_ref[...].astype(BF16), wa_ref[...], preferred_element_type=F32)
    ps = jnp.dot(ssd_ref[...], ws_ref[...], preferred_element_type=F32)
    merged = _sigmoid(gate_ref[:, 0:D_MODEL]) * pa + _sigmoid(gate_ref[:, D_MODEL:2 * D_MODEL]) * ps
    o_ref[...] = x_ref[...] + g1_ref[...] * jnp.dot(
        merged.astype(BF16), wo_ref[...], preferred_element_type=F32)


def _out_proj(attn, ssd, proj, x, mod, w_o_attn, w_o_ssd, w_out, layer, tm):
    bx, lx, d = x.shape
    r = mod.shape[2]
    row = lambda w, cb=0: pl.BlockSpec((None, tm, w), lambda b, i: (b, i, cb))
    wgt = lambda k, n: pl.BlockSpec((None, k, n), lambda b, i: (layer, 0, 0), pipeline_mode=pl.Buffered(1))
    return pl.pallas_call(
        _out_proj_kernel,
        grid=(bx, lx // tm),
        in_specs=[
            row(GROUP_W), row(D_INNER), row(2 * D_MODEL, OFF_GATE // (2 * D_MODEL)), row(d),
            pl.BlockSpec((None, None, r, d), lambda b, i: (layer, b, 0, 2)),
            wgt(GROUP_W, d), wgt(D_INNER, d), wgt(d, d),
        ],
        out_specs=row(d),
        out_shape=jax.ShapeDtypeStruct((bx, lx, d), F32),
        compiler_params=_params(("arbitrary", "arbitrary")),
        name="out_proj",
    )(attn, ssd, proj, x, mod, w_o_attn, w_o_ssd, w_out)


MLP_FF_TILE = 1024


def _mlp_kernel(*refs, final, ride):
    x_ref, mod_ref, g_ref, wu_ref, wd_ref, fg_ref = refs[:6]
    o_ref = refs[-3] if ride else refs[-1]
    x = x_ref[...]
    h = (_rms(x) * g_ref[...]) * (1.0 + mod_ref[:, D_MODEL:2 * D_MODEL]) + mod_ref[:, 0:D_MODEL]
    hb = h.astype(BF16)

    ride_done = None
    if ride:
        proj_ref, c_ref, b_ref, bself_ref = refs[6:10]
        oc_ref, part_ref = refs[-2:]
        g = RIDE_GROUP
        consts = _dec_consts()
        for j in range(proj_ref.shape[0]):
            m, l, out = _dec_group(g, proj_ref[j], c_ref.at[j], oc_ref.at[j], b_ref[...],
                                   bself_ref[g][:, 0:1], consts, on_vpu=True)
            part_ref[j, :, 0:GROUP_W] = out
            part_ref[j, :, GROUP_W:GROUP_W + LANES] = jnp.broadcast_to(m, (SUBLANES, LANES))
            part_ref[j, :, GROUP_W + LANES:PART_W] = jnp.broadcast_to(l, (SUBLANES, LANES))
            zero = jnp.minimum(l[0:1, 0:1], 0.0)
            ride_done = zero if ride_done is None else ride_done + zero

    n_ff = D_FF // MLP_FF_TILE
    acc = None
    for f in range(n_ff):
        fs = slice(f * MLP_FF_TILE, (f + 1) * MLP_FF_TILE)
        lhs = hb + ride_done.astype(BF16) if (ride_done is not None and f == n_ff - 1) else hb
        u = jnp.maximum(jnp.dot(lhs, wu_ref[:, fs], preferred_element_type=F32), 0.0)
        part = jnp.dot((u * u).astype(BF16), wd_ref[fs, :], preferred_element_type=F32)
        acc = part if f == 0 else acc + part
    x2 = x + mod_ref[:, 2 * D_MODEL:3 * D_MODEL] * acc
    if final:
        x2 = _rms(x2) * fg_ref[...]
    o_ref[...] = x2


def _mlp(x, mod, norm_g, w_up, w_down, final_g, layer, tm, final, ride=None):
    bx, lx, d = x.shape
    r = mod.shape[2]
    nt = lx // tm
    resident = pl.Buffered(1)
    in_specs = [
        pl.BlockSpec((None, tm, d), lambda b, i: (b, i, 0)),
        pl.BlockSpec((None, None, r, 3 * d), lambda b, i: (layer, b, 0, 1)),
        pl.BlockSpec((None, 1, d), lambda b, i: (layer, 0, 0)),
        pl.BlockSpec((None, d, D_FF), lambda b, i: (layer, 0, 0), pipeline_mode=resident),
        pl.BlockSpec((None, D_FF, d), lambda b, i: (layer, 0, 0), pipeline_mode=resident),
        pl.BlockSpec((1, d), lambda b, i: (0, 0)),
    ]
    out_specs = [pl.BlockSpec((None, tm, d), lambda b, i: (b, i, 0))]
    out_shape = [jax.ShapeDtypeStruct((bx, lx, d), F32)]
    args = [x, mod, norm_g, w_up, w_down, final_g]
    aliases = {}
    if ride is not None:
        proj_s, cache, bias, bias_self, prev = ride
        bs = proj_s.shape[1]
        k = bs // (bx * nt)
        assert k * bx * nt == bs
        cspec = pl.BlockSpec((None, k) + cache.shape[2:], lambda b, i: (layer, b * nt + i, 0, 0, 0, 0))
        in_specs += [pl.BlockSpec((k, 1, PROJ_W), lambda b, i: (b * nt + i, 0, 0)), cspec,
                     pl.BlockSpec(bias.shape, lambda b, i: (0, 0)),
                     pl.BlockSpec(bias_self.shape, lambda b, i: (0, 0, 0))]
        args += [proj_s.reshape(bs, 1, PROJ_W), cache, bias, bias_self]
        if prev is not None:
            in_specs.append(pl.BlockSpec(memory_space=pl.ANY))
            aliases = {len(args): 1}
            args.append(prev)
        out_specs += [cspec, pl.BlockSpec((k, SUBLANES, PART_W), lambda b, i: (b * nt + i, 0, 0))]
        out_shape += [jax.ShapeDtypeStruct(cache.shape, cache.dtype),
                      jax.ShapeDtypeStruct((bs, SUBLANES, PART_W), F32)]
    out = pl.pallas_call(
        functools.partial(_mlp_kernel, final=final, ride=ride is not None),
        grid=(bx, nt),
        in_specs=in_specs,
        out_specs=out_specs,
        out_shape=out_shape,
        input_output_aliases=aliases,
        compiler_params=_params(("arbitrary", "arbitrary")),
        name="mlp",
    )(*args)
    return out if ride is not None else out[0]


RIDE_GROUP = N_GROUPS - 1
PART_W = GROUP_W + 2 * LANES


def _dec_consts():
    row = lax.broadcasted_iota(jnp.int32, (SUBLANES, GROUP_W), 0)
    lane = lax.broadcasted_iota(jnp.int32, (SUBLANES, GROUP_W), 1)
    hmask = jnp.logical_and(lane >= row * HEAD_DIM, lane < (row + 1) * HEAD_DIM)
    sel = (lax.shift_right_logical(lax.broadcasted_iota(jnp.int32, (SUBLANES, 3 * LANES), 1), 7)
           == lax.broadcasted_iota(jnp.int32, (SUBLANES, 3 * LANES), 0)).astype(F32)
    last_lane = lax.broadcasted_iota(jnp.int32, (GROUP_W, LANES), 1) == LANES - 1
    return row, hmask, sel, last_lane


def _dec_group(g, proj_row, c_ref, oc_ref, bias, bias_self, consts, on_vpu=False):
    row, hmask, sel, last_lane = consts
    lb = c_ref.shape[-1]
    heads = [slice(h * HEAD_DIM, (h + 1) * HEAD_DIM) for h in range(HEADS_PER_GROUP)]
    qv = proj_row[:, OFF_Q + g * GROUP_W:OFF_Q + (g + 1) * GROUP_W]
    kn = proj_row[:, OFF_K + g * GROUP_W:OFF_K + (g + 1) * GROUP_W]
    vn = proj_row[:, OFF_V + g * GROUP_W:OFF_V + (g + 1) * GROUP_W]
    qbd = jnp.where(hmask, jnp.broadcast_to(qv, (SUBLANES, GROUP_W)), 0.0)
    kt = jnp.concatenate([c_ref[0, h] for h in range(HEADS_PER_GROUP)], axis=0)
    vt = jnp.concatenate([c_ref[1, h] for h in range(HEADS_PER_GROUP)], axis=0)
    rows3 = jnp.where(row == 0, jnp.broadcast_to(kn, (SUBLANES, GROUP_W)),
                      jnp.where(row == 1, jnp.broadcast_to(vn, (SUBLANES, GROUP_W)),
                                jnp.where(row == 2, jnp.broadcast_to(qv, (SUBLANES, GROUP_W)), 0.0)))
    cols = lax.dot_general(rows3, sel, TN_DIMS, precision=lax.Precision.HIGHEST,
                           preferred_element_type=F32)
    if on_vpu:
        qk = kt * jnp.concatenate([cols[:, 2 * LANES:3 * LANES]] * (lb // LANES), axis=1)
        per_head = [jnp.broadcast_to(jnp.sum(qk[hs, :], axis=0, keepdims=True), (SUBLANES, lb)) for hs in heads]
        row_lb = lax.broadcasted_iota(jnp.int32, (SUBLANES, lb), 0)
        s = functools.reduce(lambda acc, h: jnp.where(row_lb == h, per_head[h], acc),
                             range(HEADS_PER_GROUP - 1), per_head[-1])
    else:
        s = jnp.dot(qbd.astype(BF16), kt.astype(BF16), preferred_element_type=F32)
    s = s * ATTN_SCALE + bias
    s_self = jnp.sum(qbd * kn, axis=-1, keepdims=True) * ATTN_SCALE + bias_self
    m = jnp.maximum(jnp.max(s, axis=-1, keepdims=True), s_self)
    p = jnp.exp(s - m)
    p_self = jnp.exp(s_self - m)
    l = jnp.sum(p, axis=-1, keepdims=True) + p_self
    if on_vpu:
        mixed = [jnp.broadcast_to(jnp.sum(vt[hs, :] * p[h:h + 1, :], axis=1, keepdims=True), (HEAD_DIM, LANES))
                 + jnp.broadcast_to(p_self[h:h + 1, :], (HEAD_DIM, LANES)) * cols[hs, LANES:2 * LANES]
                 for h, hs in enumerate(heads)]
        col = jnp.concatenate(mixed, axis=0)
        acc = jnp.broadcast_to(jnp.concatenate(
            [col[k * LANES:(k + 1) * LANES, :].T[0:1, :] for k in range(GROUP_W // LANES)], axis=1),
            (SUBLANES, GROUP_W))
    else:
        acc = lax.dot_general(p.astype(BF16), vt.astype(BF16), NT_DIMS,
                              preferred_element_type=F32) + p_self * vn
    for kv, t in ((0, kt), (1, vt)):
        rolled = pltpu.roll(t, lb - 1, axis=1)
        tail = jnp.where(last_lane, cols[:, kv * LANES:(kv + 1) * LANES], rolled[:, lb - LANES:lb])
        new = tail if lb == LANES else jnp.concatenate([rolled[:, 0:lb - LANES], tail], axis=1)
        for h in range(HEADS_PER_GROUP):
            oc_ref[kv, h] = new[h * HEAD_DIM:(h + 1) * HEAD_DIM, :]
    return m, l, acc


def _dec_attn_kernel(*refs, groups, n_alias):
    n = len(groups)
    proj_ref, c_refs, b_refs = refs[0], refs[1:1 + n], refs[1 + n:1 + 2 * n]
    bself_ref, part_ref = refs[1 + 2 * n], refs[2 + 2 * n]
    o_ref, oc_refs = refs[3 + 2 * n + n_alias], refs[4 + 2 * n + n_alias:4 + 3 * n + n_alias]
    consts = _dec_consts()
    hmask = consts[1]
    for j in range(proj_ref.shape[0]):
        parts = [_dec_group(g, proj_ref[j], c_refs[i].at[j], oc_refs[i].at[j], b_refs[i][...],
                            bself_ref[g][:, 0:1], consts) for i, g in enumerate(groups)]
        parts.append((part_ref[j, :, GROUP_W:GROUP_W + 1], part_ref[j, :, GROUP_W + LANES:GROUP_W + LANES + 1],
                      part_ref[j, :, 0:GROUP_W]))
        m_all = functools.reduce(jnp.maximum, [p[0] for p in parts])
        num = sum(jnp.exp(m - m_all) * acc for m, _, acc in parts)
        den = sum(jnp.exp(m - m_all) * l for m, l, _ in parts)
        o_ref[j] = jnp.sum(jnp.where(hmask, num / den, 0.0), axis=0, keepdims=True)


DEC_SEQS_PER_STEP = 4


def _decode_attention(proj_s, groups, caches_t, bias_dec, bias_self, part, prev_out, layer):
    bs = proj_s.shape[1]
    n = len(groups)
    k = DEC_SEQS_PER_STEP if bs % DEC_SEQS_PER_STEP == 0 else 1
    cspec = lambda c: pl.BlockSpec((None, k) + c.shape[2:], lambda b: (layer, b, 0, 0, 0, 0))
    in_specs = [pl.BlockSpec((k, 1, PROJ_W), lambda b: (b, 0, 0))]
    in_specs += [cspec(c) for c in caches_t]
    in_specs += [pl.BlockSpec(bd.shape, lambda b: (0, 0)) for bd in bias_dec]
    in_specs += [pl.BlockSpec(bias_self.shape, lambda b: (0, 0, 0)),
                 pl.BlockSpec((k, SUBLANES, PART_W), lambda b: (b, 0, 0))]
    args = [proj_s.reshape(bs, 1, PROJ_W), *caches_t, *bias_dec, bias_self, part]
    aliases = {}
    if prev_out is not None:
        in_specs += [pl.BlockSpec(memory_space=pl.ANY)] * n
        aliases = {len(args) + j: 1 + j for j in range(n)}
        args += list(prev_out)
    out = pl.pallas_call(
        functools.partial(_dec_attn_kernel, groups=tuple(groups), n_alias=len(aliases)),
        grid=(bs // k,),
        in_specs=in_specs,
        out_specs=[pl.BlockSpec((k, 1, GROUP_W), lambda b: (b, 0, 0))] + [cspec(c) for c in caches_t],
        out_shape=[jax.ShapeDtypeStruct((bs, 1, GROUP_W), F32)] + [
            jax.ShapeDtypeStruct(c.shape, c.dtype) for c in caches_t],
        input_output_aliases=aliases,
        compiler_params=_params(("arbitrary",)),
        name="decode_attn",
    )(*args)
    return out[0].reshape(1, bs, GROUP_W), out[1:]


def _dec_ssd_kernel(*refs, n_alias):
    proj_ref, cs_ref, st_ref, cw_ref, cb_ref, dtb_ref, alog_ref, dskip_ref, ng_ref = refs[:9]
    y_ref, nst_ref, ncs_ref, xc_scr, xdt_t_scr, da_t_scr, ct_scr, yt_scr = refs[9 + n_alias:]
    b = pl.program_id(0)
    bs = proj_ref.shape[0]
    nblk = D_INNER // LANES
    pad_rows = lambda v: jnp.concatenate([v, jnp.zeros((LANES - bs, v.shape[1]), v.dtype)], axis=0)

    @pl.when(b == 0)
    def _():
        xr = proj_ref[:, OFF_XBC:OFF_XBC + CONV_DIM]
        acc = cb_ref[...]
        for t in range(D_CONV - 1):
            acc = acc + cs_ref[t] * cw_ref[t:t + 1, :]
        acc = acc + xr * cw_ref[D_CONV - 1:D_CONV, :]
        xc = pad_rows(_silu(acc))
        xc_scr[...] = xc
        for t in range(D_CONV - 2):
            ncs_ref[t] = cs_ref[t + 1]
        ncs_ref[D_CONV - 2] = xr
        dt = _softplus(proj_ref[:, OFF_DT:OFF_DT + LANES] + dtb_ref[...])
        da = jnp.exp(dt * (-jnp.exp(alog_ref[...])))
        onehot = (lax.shift_right_logical(lax.broadcasted_iota(jnp.int32, (LANES, D_INNER), 1), 6)
                  == lax.broadcasted_iota(jnp.int32, (LANES, D_INNER), 0)).astype(F32)
        expand = lambda v: pad_rows(jnp.dot(v, onehot, precision=lax.Precision.HIGHEST,
                                            preferred_element_type=F32))
        xdt = xc[:, 0:D_INNER] * expand(dt)
        dae = expand(da)
        for k in range(nblk):
            ks = slice(k * LANES, (k + 1) * LANES)
            xdt_t_scr[ks, :] = xdt[:, ks].T.astype(BF16)
            da_t = dae[:, ks].T
            da_hi = da_t.astype(BF16)
            da_t_scr[ks, 0:LANES] = da_hi
            da_t_scr[ks, LANES:2 * LANES] = (da_t - da_hi.astype(F32)).astype(BF16)
        for g in range(SSD_GROUPS):
            ct_scr[g] = xc[:, D_INNER + (SSD_GROUPS + g) * D_STATE:D_INNER + (SSD_GROUPS + g + 1) * D_STATE].T
        yt_scr[...] = jnp.zeros(yt_scr.shape, F32)

    row = lax.broadcasted_iota(jnp.int32, (LANES, LANES), 0)
    lane = lax.broadcasted_iota(jnp.int32, (LANES, LANES), 1)
    for j in range(st_ref.shape[0]):
        seq = b * st_ref.shape[0] + j
        pick = jnp.where(row == seq, 1.0, 0.0).astype(BF16)
        decay = jnp.dot(da_t_scr[...], jnp.concatenate([pick, pick], axis=0), preferred_element_type=F32)
        for g in range(SSD_GROUPS):
            rs = slice(g * SSD_GROUP_W, (g + 1) * SSD_GROUP_W)
            bg = xc_scr[:, D_INNER + g * D_STATE:D_INNER + (g + 1) * D_STATE]
            b_sel = jnp.where(row == seq, bg, 0.0).astype(BF16)
            hn = st_ref[j, rs, :] * decay[rs, :] + jnp.dot(xdt_t_scr[rs, :], b_sel, preferred_element_type=F32)
            nst_ref[j, rs, :] = hn
            c_sel = jnp.where(lane == seq, ct_scr[g], 0.0).astype(BF16)
            yt_scr[rs, :] += jnp.dot(hn.astype(BF16), c_sel, preferred_element_type=F32)

    @pl.when(b == pl.num_programs(0) - 1)
    def _():
        for g in range(SSD_GROUPS):
            gs = slice(g * SSD_GROUP_W, (g + 1) * SSD_GROUP_W)
            yg = jnp.concatenate([yt_scr[k * LANES:(k + 1) * LANES, :].T[0:bs, :]
                                  for k in range(g * SSD_GROUP_W // LANES, (g + 1) * SSD_GROUP_W // LANES)],
                                 axis=1)
            y = yg + dskip_ref[:, gs] * xc_scr[0:bs, gs]
            hg = y * _silu(proj_ref[:, OFF_Z + g * SSD_GROUP_W:OFF_Z + (g + 1) * SSD_GROUP_W])
            y_ref[:, gs] = (_rms(hg) * ng_ref[:, gs]).astype(BF16)


def _decode_ssd(proj_s, conv_state_t, state, conv_w, conv_b, dt_bias, a_log, d_skip, norm_g, prev_state, layer):
    bs = proj_s.shape[1]
    vec = lambda w: pl.BlockSpec((None, 1, w), lambda b: (layer, 0, 0))
    k = DEC_SEQS_PER_STEP if bs % DEC_SEQS_PER_STEP == 0 else 1
    st_spec = pl.BlockSpec((None, k, D_INNER, D_STATE), lambda b: (layer, b, 0, 0))
    args = [proj_s, conv_state_t, state, conv_w, conv_b, dt_bias, a_log, d_skip, norm_g]
    extra, aliases = [], {}
    if prev_state is not None:
        extra, aliases = [pl.BlockSpec(memory_space=pl.ANY)], {len(args): 1}
        args.append(prev_state)
    return pl.pallas_call(
        functools.partial(_dec_ssd_kernel, n_alias=len(aliases)),
        grid=(bs // k,),
        in_specs=[
            pl.BlockSpec((None, bs, PROJ_W), lambda b: (0, 0, 0)),
            pl.BlockSpec((None, D_CONV - 1, bs, CONV_DIM), lambda b: (layer, 0, 0, 0)),
            st_spec,
            pl.BlockSpec((None, D_CONV, CONV_DIM), lambda b: (layer, 0, 0)),
            vec(CONV_DIM), vec(LANES), vec(LANES), vec(D_INNER), vec(D_INNER),
        ] + extra,
        out_specs=[
            pl.BlockSpec((None, bs, D_INNER), lambda b: (0, 0, 0)),
            st_spec,
            pl.BlockSpec((D_CONV - 1, bs, CONV_DIM), lambda b: (0, 0, 0)),
        ],
        out_shape=[
            jax.ShapeDtypeStruct((1, bs, D_INNER), BF16),
            jax.ShapeDtypeStruct(state.shape, F32),
            jax.ShapeDtypeStruct((D_CONV - 1, bs, CONV_DIM), F32),
        ],
        input_output_aliases=aliases,
        scratch_shapes=[
            pltpu.VMEM((LANES, CONV_DIM), F32),
            pltpu.VMEM((D_INNER, LANES), BF16),
            pltpu.VMEM((D_INNER, 2 * LANES), BF16),
            pltpu.VMEM((SSD_GROUPS, D_STATE, LANES), F32),
            pltpu.VMEM((D_INNER, LANES), F32),
        ],
        compiler_params=_params(("arbitrary",)),
        name="decode_ssd",
    )(*args)


def _reorder_w_in(w_in):
    q0, z0, x0, d0, g0 = 0, 3 * QKV_W, 3 * QKV_W + D_INNER, 3 * QKV_W + D_INNER + CONV_DIM, \
        3 * QKV_W + D_INNER + CONV_DIM + SSD_HEADS
    wt = jnp.swapaxes(w_in, 1, 2)
    pieces = [wt[:, z0:x0], wt[:, g0:g0 + 2 * D_MODEL], wt[:, x0:d0], wt[:, q0:z0], wt[:, d0:g0],
              jnp.zeros((wt.shape[0], DT_W - SSD_HEADS, wt.shape[2]), wt.dtype)]
    return jnp.concatenate(pieces, axis=1).astype(BF16)


def _pad_lanes(v, width):
    return jnp.pad(v, ((0, 0), (0, width - v.shape[-1])))[:, None, :]


def kernel(x_prompt, x_sample, cache_kv_g0, cache_kv_g1, cache_kv_g2, state_ssm, state_conv, c_prompt,
           c_sample, rel_bias, w_ada, b_ada, norm1_g, norm2_g, w_in, conv_w, conv_b, dt_bias, a_log, d_skip,
           ssd_norm_g, w_o_attn, w_o_ssd, w_out, w_up, w_down, final_g):
    depth = w_in.shape[0]
    b, l, d = x_prompt.shape
    bs = x_sample.shape[0]
    caches = (cache_kv_g0, cache_kv_g1, cache_kv_g2)
    assert d == D_MODEL and x_sample.shape[1] == 1 and l % ATTN_ROWS == 0 and PROJ_W % IN_PROJ_TN == 0
    assert all(win == BAND * dil for win, dil in DIL_GROUPS)
    assert all(c.shape[2] == win for c, (win, _) in zip(caches, DIL_GROUPS))
    assert bs % SUBLANES == 0 and bs <= LANES
    assert math.frexp(ATTN_SCALE)[0] == 0.5

    w_in_r = _reorder_w_in(w_in)
    w_oa, w_os, w_o, w_u, w_d = (w.astype(BF16) for w in (w_o_attn, w_o_ssd, w_out, w_up, w_down))
    n1, n2 = norm1_g[:, None, :], norm2_g[:, None, :]
    conv_b3, ng3 = conv_b[:, None, :], ssd_norm_g[:, None, :]
    dtb3, alog3 = _pad_lanes(dt_bias, LANES), _pad_lanes(a_log, LANES)
    dskip3 = jnp.repeat(d_skip, SSD_P, axis=-1)[:, None, :]
    fg = final_g[None, :]

    rows = b + bs
    rows_pad = -(-rows // SUBLANES) * SUBLANES
    c_all = jnp.pad(jnp.concatenate([c_prompt, c_sample], axis=0), ((0, rows_pad - rows), (0, 0)))
    mod = _ada_mod(c_all, w_ada, b_ada)
    mod_p = mod[:, :b, None, :]
    mod_s = mod[:, None, b:rows, :]

    bias = _bias_lookup(rel_bias, _bucket_tiles())
    head_rows = ((0, SUBLANES - HEADS_PER_GROUP), (0, 0))
    bias_rows = _bias_lookup(rel_bias, _bucket_rows())[:, 0, :]
    bias_dec = [jnp.pad(bias_rows[HEADS_PER_GROUP * g:HEADS_PER_GROUP * (g + 1), :win], head_rows)
                for g, (win, _) in enumerate(DIL_GROUPS)]
    self_bias = bias[:, 0, BAND].reshape(N_GROUPS, HEADS_PER_GROUP, 1)
    bias_self = jnp.pad(jnp.broadcast_to(self_bias, (N_GROUPS, HEADS_PER_GROUP, LANES)), ((0, 0),) + head_rows)

    xp = x_prompt
    xs = x_sample.reshape(1, bs, d)
    conv_state_t = jnp.swapaxes(state_conv, 1, 2)
    state_r = state_ssm.reshape(depth, bs, D_INNER, D_STATE)
    caches_t = [jnp.transpose(c, (0, 1, 3, 4, 5, 2)) for c in caches]
    tm_p = 1024 if l % 1024 == 0 else BAND

    kv_p = [[] for _ in DIL_GROUPS]
    ssm_p, conv_p, conv_s = [], [], []
    others = [g for g in range(N_GROUPS) if g != RIDE_GROUP]
    kv_ride, kv_rest, st_s = None, None, None
    for layer in range(depth):
        final = layer == depth - 1
        proj_s = _in_proj(xs, mod_s, n1, w_in_r, layer, bs)
        proj = _in_proj(xp, mod_p, n1, w_in_r, layer, tm_p)
        attn = _prompt_attention(proj, bias)
        ssd, st, cs = _prompt_ssd(proj, conv_w, conv_b3, dtb3, alog3, dskip3, ng3, layer)
        x1 = _out_proj(attn, ssd, proj, xp, mod_p, w_oa, w_os, w_o, layer, min(512, l))
        xp, kv_ride, part = _mlp(x1, mod_p, n2, w_u, w_d, fg, layer, min(512, l), final,
                                 ride=(proj_s, caches_t[RIDE_GROUP], bias_dec[RIDE_GROUP], bias_self, kv_ride))
        for g, (win, _) in enumerate(DIL_GROUPS):
            keep = min(win, l)
            kk = proj[:, l - keep:, OFF_K + g * GROUP_W:OFF_K + (g + 1) * GROUP_W]
            vv = proj[:, l - keep:, OFF_V + g * GROUP_W:OFF_V + (g + 1) * GROUP_W]
            kv_p[g].append(jnp.stack([kk, vv], axis=2).reshape(b, keep, 2, HEADS_PER_GROUP, HEAD_DIM))
        ssm_p.append(st.reshape(b, SSD_HEADS, SSD_P, D_STATE))
        conv_p.append(cs)
        attn_s, kv_rest = _decode_attention(proj_s, others, [caches_t[g] for g in others],
                                            [bias_dec[g] for g in others], bias_self, part, kv_rest, layer)
        ssd_s, st_s, cs_s = _decode_ssd(proj_s, conv_state_t, state_r, conv_w, conv_b3, dtb3, alog3,
                                        dskip3, ng3, st_s, layer)
        x1s = _out_proj(attn_s, ssd_s, proj_s, xs, mod_s, w_oa, w_os, w_o, layer, bs)
        xs = _mlp(x1s, mod_s, n2, w_u, w_d, fg, layer, bs, final)
        conv_s.append(jnp.swapaxes(cs_s, 0, 1))

    kv_s = dict(zip(others, kv_rest))
    kv_s[RIDE_GROUP] = kv_ride
    kv_s = [jnp.transpose(kv_s[g], (0, 1, 5, 2, 3, 4)) for g in range(N_GROUPS)]
    return (xp, xs.reshape(bs, 1, d),
            jnp.stack(kv_p[0], axis=0), jnp.stack(kv_p[1], axis=0), jnp.stack(kv_p[2], axis=0),
            jnp.stack(ssm_p, axis=0), jnp.stack(conv_p, axis=0),
            kv_s[0], kv_s[1], kv_s[2],
            st_s.reshape(depth, bs, SSD_HEADS, SSD_P, D_STATE), jnp.stack(conv_s, axis=0))
```
